```python
import jax, jax.numpy as jnp
from jax import lax
import numpy as np

D_MODEL = 2048
BATCH = 2
SEQ = 8192
DEPTH = 2

GRID_W = 64
CTX_LEN = 256

MLA_HEADS = 6
MLA_NOPE = 128
MLA_ROPE = 64
MLA_V = 128
MLA_Q_LORA = 512
MLA_KV_LORA = 256
ML_HEADS = 4
ML_DH = 192
ML_CHUNK = 64
ML_CONV = 3
NA_HEADS = 4
NA_DH = 128
NA_KR = 8
NA_KC = 16
D_FF = 256 * ((8 * D_MODEL // 3 + 255) // 256)
FFN_CONV = 3

ROPE_BASE = 10000.0
EPS = 1e-6
Q_BLOCK = 128
N_MOD = 6

MLA_W = MLA_HEADS * MLA_V
ML_W = ML_HEADS * ML_DH
NA_W = NA_HEADS * NA_DH
D_MIX = MLA_W + ML_W + NA_W
IN_SIZES = (MLA_Q_LORA, MLA_KV_LORA, MLA_ROPE, ML_W, ML_W, ML_W, ML_W, 2 * ML_HEADS, 2 * ML_HEADS, NA_W, NA_W, NA_W)
D_IN = sum(IN_SIZES)
IN_OFFSETS = tuple(sum(IN_SIZES[:i + 1]) for i in range(len(IN_SIZES) - 1))

kernel_name = 'hybrid_mla_mlstm_natten_convffn_dit'


def rmsnorm(x, g):
    xf = x.astype(jnp.float32)
    y = xf * lax.rsqrt(jnp.mean(xf * xf, axis=-1, keepdims=True) + EPS)
    return (y * g.astype(jnp.float32)).astype(x.dtype)


def dwconv(x, w, b):
    k = w.shape[0]
    pad = k // 2
    y = lax.conv_general_dilated(x, w[:, None, :].astype(x.dtype), (1,), [(pad, pad)],
                                 dimension_numbers=('NWC', 'WIO', 'NWC'), feature_group_count=x.shape[-1])
    return y + b


def axial_rope(n_tokens, dim):
    t = jnp.arange(n_tokens)
    row = (t // GRID_W).astype(jnp.float32)
    col = (t % GRID_W).astype(jnp.float32)
    n_freq = dim // 4
    inv = ROPE_BASE ** (-jnp.arange(n_freq, dtype=jnp.float32) / n_freq)
    ang = jnp.concatenate([row[:, None] * inv, col[:, None] * inv], axis=-1)
    return jnp.cos(ang), jnp.sin(ang)


def apply_rope(x, cos, sin):
    xp = x.astype(jnp.float32).reshape(x.shape[:-1] + (x.shape[-1] // 2, 2))
    x1, x2 = xp[..., 0], xp[..., 1]
    y = jnp.stack([x1 * cos - x2 * sin, x1 * sin + x2 * cos], axis=-1)
    return y.reshape(x.shape).astype(x.dtype)


def ada_mod(cond, w, b):
    return jnp.split(jax.nn.silu(cond) @ w + b, N_MOD, axis=-1)


def block_attention(q, k, v, scale):
    B, S, H, dq = q.shape
    dv = v.shape[-1]
    nb = S // Q_BLOCK
    qb = jnp.moveaxis(q.reshape(B, nb, Q_BLOCK, H, dq), 1, 0)

    def attend(qi):
        s = jnp.einsum('bqhd,bkhd->bhqk', qi, k).astype(jnp.float32) * scale
        p = jax.nn.softmax(s, axis=-1).astype(v.dtype)
        return jnp.einsum('bhqk,bkhd->bqhd', p, v)

    o = lax.map(attend, qb)
    return jnp.moveaxis(o, 0, 1).reshape(B, S, H * dv)


def mla_qkv(q_lat_in, kv_lat_in, k_rope_in, q_norm, kv_norm, w_uq, w_uk, w_uv, rope):
    B, S, _ = q_lat_in.shape
    q = (rmsnorm(q_lat_in, q_norm) @ w_uq).reshape(B, S, MLA_HEADS, MLA_NOPE + MLA_ROPE)
    c_kv = rmsnorm(kv_lat_in, kv_norm)
    k_nope = (c_kv @ w_uk).reshape(B, S, MLA_HEADS, MLA_NOPE)
    v = (c_kv @ w_uv).reshape(B, S, MLA_HEADS, MLA_V)
    q_nope, q_rope = q[..., :MLA_NOPE], q[..., MLA_NOPE:]
    k_rope = k_rope_in
    if rope is not None:
        cos, sin = rope
        q_rope = apply_rope(q_rope, cos[:, None, :], sin[:, None, :])
        k_rope = apply_rope(k_rope, cos, sin)
    k_rope = jnp.broadcast_to(k_rope[:, :, None, :], (B, S, MLA_HEADS, MLA_ROPE))
    return (jnp.concatenate([q_nope, q_rope], axis=-1), jnp.concatenate([k_nope, k_rope], axis=-1), v)


def mlstm_prepare(q_in, k_in, v_in, o_in, i_in, f_in, conv_w, conv_b, i_bias, f_bias):
    B, S, _ = q_in.shape
    qk = jax.nn.silu(dwconv(jnp.concatenate([q_in, k_in], axis=-1), conv_w, conv_b))

    def heads(a):
        return a.reshape(B, S, ML_HEADS, ML_DH).transpose(0, 2, 1, 3).astype(jnp.float32)

    def gates(a, bias):
        return (a.reshape(B, S, 2, ML_HEADS) + bias).astype(jnp.float32).transpose(2, 0, 3, 1)

    q = heads(qk[..., :ML_W])
    k = heads(qk[..., ML_W:]) * (ML_DH ** -0.5)
    v = heads(v_in)
    log_i = gates(i_in, i_bias)
    log_f = jax.nn.log_sigmoid(gates(f_in, f_bias))
    o = jax.nn.sigmoid(o_in)
    return q, k, v, log_i, log_f, o


def mlstm_scan(q, k, v, log_i, log_f, state):
    B, H, S, d = q.shape
    L = ML_CHUNK
    nc = S // L

    def chunks(a):
        return jnp.moveaxis(a.reshape((B, H, nc, L) + a.shape[3:]), 2, 0)

    lower = jnp.tril(jnp.ones((L, L), dtype=bool))

    def step(carry, inp):
        C, n, m = carry
        qc, kc, vc, li, lf = inp
        b = jnp.cumsum(lf, axis=-1)
        dmat = jnp.where(lower, b[..., :, None] - b[..., None, :] + li[..., None, :], -jnp.inf)
        g = b + m[..., None]
        m_t = jnp.maximum(g, dmat.max(axis=-1))
        w_loc = jnp.exp(dmat - m_t[..., None])
        w_st = jnp.exp(g - m_t)
        a = jnp.einsum('bhtd,bhsd->bhts', qc, kc) * w_loc
        num = w_st[..., None] * jnp.einsum('bhvd,bhtd->bhtv', C, qc) + jnp.einsum('bhts,bhsv->bhtv', a, vc)
        den = w_st * jnp.einsum('bhd,bhtd->bht', n, qc) + a.sum(axis=-1)
        h = num / jnp.maximum(jnp.abs(den), jnp.exp(-m_t))[..., None]
        b_end = b[..., -1]
        e_log = b_end[..., None] - b + li
        m_new = jnp.maximum(b_end + m, e_log.max(axis=-1))
        decay = jnp.exp(b_end + m - m_new)
        e = jnp.exp(e_log - m_new[..., None])
        C_new = decay[..., None, None] * C + jnp.einsum('bhs,bhsv,bhsd->bhvd', e, vc, kc)
        n_new = decay[..., None] * n + jnp.einsum('bhs,bhsd->bhd', e, kc)
        return (C_new, n_new, m_new), h

    state, h = lax.scan(step, state, (chunks(q), chunks(k), chunks(v), chunks(log_i), chunks(log_f)))
    return state, jnp.moveaxis(h, 0, 2).reshape(B, H, S, d)


def mlstm_bidirectional(ctx_in, lat_in):
    qc, kc, vc, lic, lfc, oc = ctx_in
    ql, kl, vl, lil, lfl, ol = lat_in
    B, H, _, d = ql.shape
    h_ctx = jnp.zeros_like(qc)
    h_lat = jnp.zeros_like(ql)
    for direction in range(2):
        if direction == 1:
            flip = lambda a: jnp.flip(a, axis=2)
        else:
            flip = lambda a: a
        init = (jnp.zeros((B, H, d, d), jnp.float32), jnp.zeros((B, H, d), jnp.float32), jnp.zeros((B, H), jnp.float32))
        st_ctx, hc = mlstm_scan(flip(qc), flip(kc), flip(vc), flip(lic[direction]), flip(lfc[direction]), init)
        _, hl = mlstm_scan(flip(ql), flip(kl), flip(vl), flip(lil[direction]), flip(lfl[direction]), st_ctx)
        h_ctx = h_ctx + flip(hc)
        h_lat = h_lat + flip(hl)

    def merge(h, o):
        return (h.transpose(0, 2, 1, 3).reshape(o.shape) * o).astype(o.dtype)

    return merge(h_ctx, oc), merge(h_lat, ol)


def neighbourhood_attention(q, k, v, k_ctx, v_ctx, rpb):
    B, S, H, d = q.shape
    rows = S // GRID_W
    kr = min(NA_KR, rows)
    kc = NA_KC
    scale = d ** -0.5
    qg = q.reshape(B, rows, GRID_W, H, d)
    kg = k.reshape(B, rows, GRID_W, H, d)
    vg = v.reshape(B, rows, GRID_W, H, d)
    cols = jnp.arange(GRID_W)
    col_idx = jnp.clip(cols - kc // 2, 0, GRID_W - kc)[:, None] + jnp.arange(kc)[None, :]
    dc = col_idx - cols[:, None] + (NA_KC - 1)
    n_loc = kr * kc

    def one_row(r):
        r0 = jnp.clip(r - kr // 2, 0, rows - kr)
        kn = lax.dynamic_slice_in_dim(kg, r0, kr, axis=1)[:, :, col_idx]
        vn = lax.dynamic_slice_in_dim(vg, r0, kr, axis=1)[:, :, col_idx]
        qr = lax.dynamic_index_in_dim(qg, r, axis=1, keepdims=False)
        dr = r0 + jnp.arange(kr) - r + (NA_KR - 1)
        bias = rpb[:, dr[None, :, None], dc[:, None, :]].astype(jnp.float32)
        s_loc = jnp.einsum('bqhd,brqchd->bhqrc', qr, kn).astype(jnp.float32) * scale + bias
        s_ctx = jnp.einsum('bqhd,bkhd->bhqk', qr, k_ctx).astype(jnp.float32) * scale
        s = jnp.concatenate([s_loc.reshape(B, H, GRID_W, n_loc), s_ctx], axis=-1)
        p = jax.nn.softmax(s, axis=-1).astype(v.dtype)
        p_loc = p[..., :n_loc].reshape(B, H, GRID_W, kr, kc)
        return (jnp.einsum('bhqrc,brqchd->bqhd', p_loc, vn)
                + jnp.einsum('bhqk,bkhd->bqhd', p[..., n_loc:], v_ctx))

    o = lax.map(one_row, jnp.arange(rows))
    return jnp.moveaxis(o, 0, 1).reshape(B, S, H * d)


def token_mixers(hn_ctx, hn_lat, w_in, mla_q_norm, mla_kv_norm, mla_w_uq, mla_w_uk, mla_w_uv,
                 ml_conv_w, ml_conv_b, ml_i_bias, ml_f_bias, na_rpb, w_out, rope, need_ctx_out):
    pc = jnp.split(hn_ctx @ w_in, IN_OFFSETS, axis=-1)
    pl = jnp.split(hn_lat @ w_in, IN_OFFSETS, axis=-1)
    mq_c, mk_c, mv_c = mla_qkv(pc[0], pc[1], pc[2], mla_q_norm, mla_kv_norm, mla_w_uq, mla_w_uk, mla_w_uv, None)
    mq_l, mk_l, mv_l = mla_qkv(pl[0], pl[1], pl[2], mla_q_norm, mla_kv_norm, mla_w_uq, mla_w_uk, mla_w_uv, rope)
    mla_scale = (MLA_NOPE + MLA_ROPE) ** -0.5
    mla_lat = block_attention(mq_l, jnp.concatenate([mk_c, mk_l], axis=1), jnp.concatenate([mv_c, mv_l], axis=1), mla_scale)
    ml_ctx, ml_lat = mlstm_bidirectional(
        mlstm_prepare(pc[3], pc[4], pc[5], pc[6], pc[7], pc[8], ml_conv_w, ml_conv_b, ml_i_bias, ml_f_bias),
        mlstm_prepare(pl[3], pl[4], pl[5], pl[6], pl[7], pl[8], ml_conv_w, ml_conv_b, ml_i_bias, ml_f_bias))
    def na_heads(a):
        return a.reshape(a.shape[0], a.shape[1], NA_HEADS, NA_DH)
    nq_c, nk_c, nv_c = na_heads(pc[9]), na_heads(pc[10]), na_heads(pc[11])
    na_lat = neighbourhood_attention(na_heads(pl[9]), na_heads(pl[10]), na_heads(pl[11]), nk_c, nv_c, na_rpb)
    out_lat = jnp.concatenate([mla_lat, ml_lat, na_lat], axis=-1) @ w_out
    if not need_ctx_out:
        return None, out_lat
    mla_ctx = block_attention(mq_c, mk_c, mv_c, mla_scale)
    na_ctx = block_attention(nq_c, nk_c, nv_c, NA_DH ** -0.5)
    out_ctx = jnp.concatenate([mla_ctx, ml_ctx, na_ctx], axis=-1) @ w_out
    return out_ctx, out_lat


def conv_ffn(hn, w_up, conv_w, conv_b, w_down):
    u = dwconv(hn @ w_up, conv_w, conv_b)
    gate, val = jnp.split(u, 2, axis=-1)
    return (jax.nn.silu(gate) * val) @ w_down


def setup_inputs(seed: int = 0) -> dict:
    key = jax.random.key(seed)
    ks = jax.random.split(key, 25)
    f32 = jnp.float32
    L = DEPTH

    def nrm(i, shape, scale):
        return jax.random.normal(ks[i], shape, f32) * scale

    return {
        'x': nrm(0, (BATCH, SEQ, D_MODEL), 1.0),
        'c': nrm(1, (BATCH, D_MODEL), 1.0),
        'ctx': nrm(2, (BATCH, CTX_LEN, D_MODEL), 1.0),
        'c_ctx': nrm(3, (D_MODEL,), 1.0),
        'ada_w': nrm(4, (L, D_MODEL, N_MOD * D_MODEL), 0.5 * D_MODEL ** -0.5),
        'ada_b': nrm(5, (L, N_MOD * D_MODEL), 0.02),
        'norm1_g': 1.0 + nrm(6, (L, D_MODEL), 0.02),
        'norm2_g': 1.0 + nrm(7, (L, D_MODEL), 0.02),
        'w_in': nrm(8, (L, D_MODEL, D_IN), D_MODEL ** -0.5),
        'mla_q_norm': 1.0 + nrm(9, (L, MLA_Q_LORA), 0.02),
        'mla_kv_norm': 1.0 + nrm(10, (L, MLA_KV_LORA), 0.02),
        'mla_w_uq': nrm(11, (L, MLA_Q_LORA, MLA_HEADS * (MLA_NOPE + MLA_ROPE)), MLA_Q_LORA ** -0.5),
        'mla_w_uk': nrm(12, (L, MLA_KV_LORA, MLA_HEADS * MLA_NOPE), MLA_KV_LORA ** -0.5),
        'mla_w_uv': nrm(13, (L, MLA_KV_LORA, MLA_HEADS * MLA_V), MLA_KV_LORA ** -0.5),
        'ml_conv_w': nrm(14, (L, ML_CONV, 2 * ML_W), ML_CONV ** -0.5),
        'ml_conv_b': nrm(15, (L, 2 * ML_W), 0.02),
        'ml_i_bias': nrm(16, (L, 2, ML_HEADS), 0.1),
        'ml_f_bias': jnp.linspace(3.0, 6.0, ML_HEADS, dtype=f32) + nrm(17, (L, 2, ML_HEADS), 0.1),
        'na_rpb': nrm(18, (L, NA_HEADS, 2 * NA_KR - 1, 2 * NA_KC - 1), 0.1),
        'w_out': nrm(19, (L, D_MIX, D_MODEL), D_MIX ** -0.5),
        'ffn_w_up': nrm(20, (L, D_MODEL, 2 * D_FF), D_MODEL ** -0.5),
        'ffn_conv_w': nrm(21, (L, FFN_CONV, 2 * D_FF), FFN_CONV ** -0.5),
        'ffn_conv_b': nrm(22, (L, 2 * D_FF), 0.02),
        'ffn_w_down': nrm(23, (L, D_FF, D_MODEL), D_FF ** -0.5),
        'final_norm_g': 1.0 + nrm(24, (D_MODEL,), 0.02),
    }


def reference(x, c, ctx, c_ctx, ada_w, ada_b, norm1_g, norm2_g, w_in, mla_q_norm, mla_kv_norm,
              mla_w_uq, mla_w_uk, mla_w_uv, ml_conv_w, ml_conv_b, ml_i_bias, ml_f_bias, na_rpb, w_out,
              ffn_w_up, ffn_conv_w, ffn_conv_b, ffn_w_down, final_norm_g):
    rope = axial_rope(x.shape[1], MLA_ROPE)
    h = x
    hc = ctx
    for l in range(DEPTH):
        last = l == DEPTH - 1
        sh1, sc1, g1, sh2, sc2, g2 = ada_mod(c[:, None, :], ada_w[l], ada_b[l])
        sh1c, sc1c, g1c, sh2c, sc2c, g2c = ada_mod(c_ctx[None, :], ada_w[l], ada_b[l])
        hn_lat = rmsnorm(h, norm1_g[l]) * (1.0 + sc1) + sh1
        hn_ctx = rmsnorm(hc, norm1_g[l]) * (1.0 + sc1c) + sh1c
        mix_ctx, mix_lat = token_mixers(hn_ctx, hn_lat, w_in[l], mla_q_norm[l], mla_kv_norm[l], mla_w_uq[l],
                                        mla_w_uk[l], mla_w_uv[l], ml_conv_w[l], ml_conv_b[l], ml_i_bias[l],
                                        ml_f_bias[l], na_rpb[l], w_out[l], rope, not last)
        h = h + g1 * mix_lat
        h = h + g2 * conv_ffn(rmsnorm(h, norm2_g[l]) * (1.0 + sc2) + sh2,
                              ffn_w_up[l], ffn_conv_w[l], ffn_conv_b[l], ffn_w_down[l])
        if not last:
            hc = hc + g1c * mix_ctx
            hc = hc + g2c * conv_ffn(rmsnorm(hc, norm2_g[l]) * (1.0 + sc2c) + sh2c,
                                     ffn_w_up[l], ffn_conv_w[l], ffn_conv_b[l], ffn_w_down[l])
    return rmsnorm(h, final_norm_g)
```

```python
import functools
import math

import numpy as np
import jax
import jax.numpy as jnp
from jax import lax
from jax.experimental import pallas as pl
from jax.experimental.pallas import tpu as pltpu

F32 = jnp.float32
BF16 = jnp.bfloat16

D_MODEL = 2048
GRID_W = 64
CTX_LEN = 256
MLA_HEADS = 6
MLA_NOPE = 128
MLA_ROPE = 64
MLA_V = 128
MLA_Q_LORA = 512
MLA_KV_LORA = 256
ML_HEADS = 4
ML_DH = 192
NA_HEADS = 4
NA_DH = 128
NA_KR = 8
NA_KC = 16
D_FF = 256 * ((8 * D_MODEL // 3 + 255) // 256)
ROPE_BASE = 10000.0
EPS = 1e-6
N_MOD = 6
ML_W = ML_HEADS * ML_DH
NA_W = NA_HEADS * NA_DH
MLA_W = MLA_HEADS * MLA_V
IN_SIZES = (MLA_Q_LORA, MLA_KV_LORA, MLA_ROPE, ML_W, ML_W, ML_W, ML_W, 2 * ML_HEADS, 2 * ML_HEADS, NA_W, NA_W, NA_W)
IN_OFFSETS = tuple(sum(IN_SIZES[:i + 1]) for i in range(len(IN_SIZES) - 1))

LANE = 128
ML_DP = 256
ML_WP = ML_HEADS * ML_DP
ML_L = 256
MLA_DQ = 256
TM = 512
VMEM_LIMIT = 52 * 1024 * 1024

C_MLQ = 0
C_MLK = C_MLQ + ML_WP
C_MLV = C_MLK + ML_WP
C_MLO = C_MLV + ML_WP
C_QLAT = C_MLO + ML_WP
C_NAQ = C_QLAT + MLA_Q_LORA
C_NAK = C_NAQ + NA_W
C_NAV = C_NAK + NA_W
C_KVLAT = C_NAV + NA_W
C_KROPE = C_KVLAT + MLA_KV_LORA
C_MAIN = 6656
C_GATES = ML_HEADS * LANE
TN_IN = 512


def _cp(sem):
    return pltpu.CompilerParams(dimension_semantics=sem, vmem_limit_bytes=VMEM_LIMIT)


def _sigmoid(x):
    return 1.0 / (1.0 + jnp.exp(-x))


def _rms(x, g):
    return x * lax.rsqrt(jnp.mean(x * x, axis=-1, keepdims=True) + EPS) * g


def _ada_kernel(c_ref, w_ref, b_ref, o_ref):
    c = c_ref[...]
    a = c * _sigmoid(c)
    o_ref[...] = jnp.dot(a, w_ref[...], preferred_element_type=F32,
                         precision=lax.Precision.HIGHEST) + b_ref[...]


def _ada(cond8, w, b):
    d, n = w.shape
    tn = 1024
    return pl.pallas_call(
        _ada_kernel,
        grid=(n // tn,),
        in_specs=[pl.BlockSpec((8, d), lambda j: (0, 0)),
                  pl.BlockSpec((d, tn), lambda j: (0, j)),
                  pl.BlockSpec((1, tn), lambda j: (0, j))],
        out_specs=pl.BlockSpec((8, tn), lambda j: (0, j)),
        out_shape=jax.ShapeDtypeStruct((8, n), F32),
        compiler_params=_cp(("arbitrary",)),
    )(cond8, w, b.reshape(1, n))


def _nm_mm_kernel(x_ref, g_ref, sc_ref, sh_ref, w_ref, o_ref, *rest, has_tail):
    if has_tail:
        t_ref, hn_ref = rest
    else:
        (hn_ref,) = rest
    j = pl.program_id(1)

    @pl.when(j == 0)
    def _():
        y = _rms(x_ref[...], g_ref[...])
        hn_ref[...] = (y * (1.0 + sc_ref[...]) + sh_ref[...]).astype(BF16)

    acc = jnp.dot(hn_ref[...], w_ref[...], preferred_element_type=F32)
    if has_tail:
        nj = pl.num_programs(1)

        @pl.when(j < nj - 1)
        def _():
            o_ref[...] = acc.astype(o_ref.dtype)

        @pl.when(j == nj - 1)
        def _():
            t_ref[...] = acc
    else:
        o_ref[...] = acc.astype(o_ref.dtype)


def _mod_spec(chunk, rowf):
    return pl.BlockSpec((None, None, 1, D_MODEL), lambda i, j: (rowf(i), chunk, 0, 0))


def _nm_mm(x, g, mod4, sc_chunk, sh_chunk, rowf, w, tn, tail_cols=0):
    m, d = x.shape
    n = w.shape[1]
    tm = min(TM, m)
    nj = n // tn
    has_tail = tail_cols > 0
    n_main = n - tail_cols
    if has_tail:
        o_spec = [pl.BlockSpec((tm, tn), lambda i, j: (i, jnp.minimum(j, nj - 2))),
                  pl.BlockSpec((tm, tn), lambda i, j: (i, 0))]
        o_shape = [jax.ShapeDtypeStruct((m, n_main), BF16), jax.ShapeDtypeStruct((m, tail_cols), F32)]
    else:
        o_spec = pl.BlockSpec((tm, tn), lambda i, j: (i, j))
        o_shape = jax.ShapeDtypeStruct((m, n), BF16)
    return pl.pallas_call(
        functools.partial(_nm_mm_kernel, has_tail=has_tail),
        grid=(m // tm, nj),
        in_specs=[pl.BlockSpec((tm, d), lambda i, j: (i, 0)),
                  pl.BlockSpec((1, d), lambda i, j: (0, 0)),
                  _mod_spec(sc_chunk, rowf), _mod_spec(sh_chunk, rowf),
                  pl.BlockSpec((d, tn), lambda i, j: (0, j))],
        out_specs=o_spec,
        out_shape=o_shape,
        scratch_shapes=[pltpu.VMEM((tm, d), BF16)],
        compiler_params=_cp(("parallel", "arbitrary")),
    )(x, g.reshape(1, d), mod4, mod4, w)


def _mla_up_kernel(ql_ref, kvl_ref, kr_ref, tab_ref, qg_ref, kvg_ref, wuq_ref, wuk_ref, wuv_ref,
                   q_ref, k_ref, v_ref, *, scale):
    tab = tab_ref[...]
    qn = _rms(ql_ref[...].astype(F32), qg_ref[...]).astype(BF16)
    qf = jnp.dot(qn, wuq_ref[...], preferred_element_type=F32)
    for h in range(MLA_HEADS):
        c0 = h * MLA_DQ
        y = qf[:, c0 + MLA_NOPE:c0 + MLA_DQ] * tab
        z = y + pltpu.roll(y, MLA_ROPE, axis=1)
        q_ref[:, c0:c0 + MLA_NOPE] = (qf[:, c0:c0 + MLA_NOPE] * scale).astype(BF16)
        q_ref[:, c0 + MLA_NOPE:c0 + MLA_DQ] = (z * scale).astype(BF16)
    cn = _rms(kvl_ref[...].astype(F32), kvg_ref[...]).astype(BF16)
    kn = jnp.dot(cn, wuk_ref[...], preferred_element_type=F32)
    v_ref[...] = jnp.dot(cn, wuv_ref[...], preferred_element_type=F32).astype(BF16)
    ky = kr_ref[...].astype(F32) * tab
    kz = ky + pltpu.roll(ky, MLA_ROPE, axis=1)
    lane = lax.broadcasted_iota(jnp.int32, kz.shape, 1)
    kz = jnp.where(lane < MLA_ROPE, kz, 0.0).astype(BF16)
    for h in range(MLA_HEADS):
        c0 = h * MLA_DQ
        k_ref[:, c0:c0 + MLA_NOPE] = kn[:, h * MLA_NOPE:(h + 1) * MLA_NOPE].astype(BF16)
        k_ref[:, c0 + MLA_NOPE:c0 + MLA_DQ] = kz


def _mla_up(proj, tab, qg, kvg, wuq, wuk, wuv):
    m = proj.shape[0]
    tm = min(TM, tab.shape[0])
    nt = tab.shape[0] // tm
    full = lambda a: pl.BlockSpec(a.shape, lambda i: (0, 0))
    return pl.pallas_call(
        functools.partial(_mla_up_kernel, scale=(MLA_NOPE + MLA_ROPE) ** -0.5),
        grid=(m // tm,),
        in_specs=[pl.BlockSpec((tm, MLA_Q_LORA), lambda i: (i, C_QLAT // MLA_Q_LORA)),
                  pl.BlockSpec((tm, MLA_KV_LORA), lambda i: (i, C_KVLAT // MLA_KV_LORA)),
                  pl.BlockSpec((tm, LANE), lambda i: (i, C_KROPE // LANE)),
                  pl.BlockSpec((tm, LANE), lambda i: (i % nt, 0)),
                  full(qg), full(kvg), full(wuq), full(wuk), full(wuv)],
        out_specs=[pl.BlockSpec((tm, MLA_HEADS * MLA_DQ), lambda i: (i, 0)),
                   pl.BlockSpec((tm, MLA_HEADS * MLA_DQ), lambda i: (i, 0)),
                   pl.BlockSpec((tm, MLA_W), lambda i: (i, 0))],
        out_shape=[jax.ShapeDtypeStruct((m, MLA_HEADS * MLA_DQ), BF16),
                   jax.ShapeDtypeStruct((m, MLA_HEADS * MLA_DQ), BF16),
                   jax.ShapeDtypeStruct((m, MLA_W), BF16)],
        compiler_params=_cp(("parallel",)),
    )(proj, proj, proj, tab, qg, kvg, wuq, wuk, wuv)


def _flash_kernel(*refs, has_lat, scale):
    if has_lat:
        q_ref, kc_ref, vc_ref, kl_ref, vl_ref, o_ref, m_scr, l_scr, acc_scr = refs
    else:
        q_ref, kc_ref, vc_ref, o_ref, m_scr, l_scr, acc_scr = refs
    s_idx = pl.program_id(3)
    ns = pl.num_programs(3)

    def scores(k_ref):
        s = lax.dot_general(q_ref[...], k_ref[...], (((1,), (1,)), ((), ())), preferred_element_type=F32)
        return s if scale == 1.0 else s * scale

    @pl.when(s_idx == 0)
    def _():
        s = scores(kc_ref)
        m = jnp.max(s, axis=-1, keepdims=True)
        p = jnp.exp(s - m)
        m_scr[...] = m
        l_scr[...] = jnp.sum(p, axis=-1, keepdims=True)
        acc_scr[...] = jnp.dot(p.astype(BF16), vc_ref[...], preferred_element_type=F32)

    if has_lat:
        @pl.when(s_idx > 0)
        def _():
            s = scores(kl_ref)
            m_prev = m_scr[...]
            m_new = jnp.maximum(m_prev, jnp.max(s, axis=-1, keepdims=True))
            alpha = jnp.exp(m_prev - m_new)
            p = jnp.exp(s - m_new)
            m_scr[...] = m_new
            l_scr[...] = alpha * l_scr[...] + jnp.sum(p, axis=-1, keepdims=True)
            acc_scr[...] = alpha * acc_scr[...] + jnp.dot(p.astype(BF16), vl_ref[...],
                                                           preferred_element_type=F32)

    @pl.when(s_idx == ns - 1)
    def _():
        o_ref[...] = (acc_scr[...] / l_scr[...]).astype(o_ref.dtype)


def _flash(q, qcol, kc, kccol, vc, vccol, heads, dq, dv, batch, lat=None, scale=1.0, tq=512, tk=512):
    mq = q.shape[0]
    sq = mq // batch
    tq = min(tq, sq)
    nq = sq // tq
    tc = kc.shape[0] // batch
    has_lat = lat is not None
    in_specs = [pl.BlockSpec((tq, dq), lambda b, h, i, s: (b * nq + i, qcol + h)),
                pl.BlockSpec((tc, dq), lambda b, h, i, s: (b, kccol + h)),
                pl.BlockSpec((tc, dv), lambda b, h, i, s: (b, vccol + h))]
    args = [q, kc, vc]
    ns = 1
    if has_lat:
        kl, klcol, vl, vlcol = lat
        sk = kl.shape[0] // batch
        tk = min(tk, sk)
        nk = sk // tk
        ns = 1 + nk
        in_specs += [pl.BlockSpec((tk, dq), lambda b, h, i, s: (b * nk + jnp.maximum(s - 1, 0), klcol + h)),
                     pl.BlockSpec((tk, dv), lambda b, h, i, s: (b * nk + jnp.maximum(s - 1, 0), vlcol + h))]
        args += [kl, vl]
    return pl.pallas_call(
        functools.partial(_flash_kernel, has_lat=has_lat, scale=scale),
        grid=(batch, heads, nq, ns),
        in_specs=in_specs,
        out_specs=pl.BlockSpec((tq, dv), lambda b, h, i, s: (b * nq + i, h)),
        out_shape=jax.ShapeDtypeStruct((mq, heads * dv), BF16),
        scratch_shapes=[pltpu.VMEM((tq, 1), F32), pltpu.VMEM((tq, 1), F32), pltpu.VMEM((tq, dv), F32)],
        compiler_params=_cp(("parallel", "parallel", "parallel", "arbitrary")),
    )(*args)


NA_RB = 8
NA_BLK = NA_RB * GRID_W


def _na_kernel(q_ref, kp_ref, kc_ref, kn_ref, vp_ref, vc_ref, vn_ref, kx_ref, vx_ref, bias_ref, o_ref,
               kw_scr, vw_scr, *, rows, scale):
    j = pl.program_id(1)
    for t, (kr, vr) in enumerate(((kp_ref, vp_ref), (kc_ref, vc_ref), (kn_ref, vn_ref))):
        kw_scr[t * NA_BLK:(t + 1) * NA_BLK, :] = kr[...]
        vw_scr[t * NA_BLK:(t + 1) * NA_BLK, :] = vr[...]
    nwin = NA_KR * GRID_W
    dn = (((1,), (1,)), ((), ()))
    for i in range(NA_RB):
        r = NA_RB * j + i
        r0 = jnp.clip(r - NA_KR // 2, 0, rows - NA_KR)
        start = pl.multiple_of((r0 - NA_RB * j + NA_RB) * GRID_W, GRID_W)
        var = r - r0
        for h in range(NA_HEADS):
            cs = slice(h * NA_DH, (h + 1) * NA_DH)
            q = q_ref[i * GRID_W:(i + 1) * GRID_W, cs]
            kwin = kw_scr[pl.ds(start, nwin), cs]
            vwin = vw_scr[pl.ds(start, nwin), cs]
            s_loc = lax.dot_general(q, kwin, dn, preferred_element_type=F32) * scale + bias_ref[h, var]
            s_ctx = lax.dot_general(q, kx_ref[:, cs], dn, preferred_element_type=F32) * scale
            m = jnp.maximum(jnp.max(s_loc, axis=-1, keepdims=True), jnp.max(s_ctx, axis=-1, keepdims=True))
            p_loc = jnp.exp(s_loc - m)
            p_ctx = jnp.exp(s_ctx - m)
            l = jnp.sum(p_loc, axis=-1, keepdims=True) + jnp.sum(p_ctx, axis=-1, keepdims=True)
            o = (jnp.dot(p_loc.astype(BF16), vwin, preferred_element_type=F32)
                 + jnp.dot(p_ctx.astype(BF16), vx_ref[:, cs], preferred_element_type=F32))
            o_ref[i * GRID_W:(i + 1) * GRID_W, cs] = (o / l).astype(o_ref.dtype)


def _na_bias_table(rpb):
    c = np.arange(GRID_W)
    cs = np.clip(c - NA_KC // 2, 0, GRID_W - NA_KC)
    valid = (c[None, :] >= cs[:, None]) & (c[None, :] < cs[:, None] + NA_KC)
    dc = np.clip(c[None, :] - c[:, None] + (NA_KC - 1), 0, 2 * NA_KC - 2)
    dr = np.arange(NA_KR)[None, :] - np.arange(NA_KR)[:, None] + (NA_KR - 1)
    tab = rpb.astype(F32)[:, dr[:, None, :, None], dc[None, :, None, :]]
    tab = jnp.where(valid[None, None, :, None, :], tab, -1e30)
    return tab.reshape(NA_HEADS, NA_KR, GRID_W, NA_KR * GRID_W)


def _na(proj_l, proj_c, bias, batch):
    m = proj_l.shape[0]
    s = m // batch
    rows = s // GRID_W
    nb = rows // NA_RB
    cq, ck, cv = C_NAQ // NA_W, C_NAK // NA_W, C_NAV // NA_W
    tc = proj_c.shape[0] // batch
    prev = lambda b, j: b * nb + jnp.maximum(j - 1, 0)
    cur = lambda b, j: b * nb + j
    nxt = lambda b, j: b * nb + jnp.minimum(j + 1, nb - 1)
    blk = lambda rf, col: pl.BlockSpec((NA_BLK, NA_W), lambda b, j: (rf(b, j), col))
    return pl.pallas_call(
        functools.partial(_na_kernel, rows=rows, scale=NA_DH ** -0.5),
        grid=(batch, nb),
        in_specs=[blk(cur, cq), blk(prev, ck), blk(cur, ck), blk(nxt, ck),
                  blk(prev, cv), blk(cur, cv), blk(nxt, cv),
                  pl.BlockSpec((tc, NA_W), lambda b, j: (b, ck)),
                  pl.BlockSpec((tc, NA_W), lambda b, j: (b, cv)),
                  pl.BlockSpec(bias.shape, lambda b, j: (0, 0, 0, 0))],
        out_specs=pl.BlockSpec((NA_BLK, NA_W), lambda b, j: (b * nb + j, 0)),
        out_shape=jax.ShapeDtypeStruct((m, NA_W), BF16),
        scratch_shapes=[pltpu.VMEM((3 * NA_BLK, NA_W), BF16), pltpu.VMEM((3 * NA_BLK, NA_W), BF16)],
        compiler_params=_cp(("parallel", "arbitrary")),
    )(proj_l, proj_l, proj_l, proj_l, proj_l, proj_l, proj_l, proj_c, proj_c, bias)


def _conv3(x, prev_row, next_row, w):
    tm = x.shape[0]
    rid = lax.broadcasted_iota(jnp.int32, x.shape, 0)
    xp = jnp.where(rid == 0, prev_row, pltpu.roll(x, 1, axis=0))
    xn = jnp.where(rid == tm - 1, next_row, pltpu.roll(x, tm - 1, axis=0))
    return xp * w[0:1, :] + x * w[1:2, :] + xn * w[2:3, :]


HALO = 16


def _halo_rows(p_ref, n_ref, tiles_per_seg):
    i = pl.program_id(0)
    t = i % tiles_per_seg
    prev_row = jnp.where(t == 0, 0.0, p_ref[HALO - 1:HALO, :].astype(F32))
    next_row = jnp.where(t == tiles_per_seg - 1, 0.0, n_ref[0:1, :].astype(F32))
    return prev_row, next_row


def _conv_silu_kernel(x_ref, p_ref, n_ref, w_ref, b_ref, sc_ref, o_ref, *, tiles_per_seg):
    prev_row, next_row = _halo_rows(p_ref, n_ref, tiles_per_seg)
    y = _conv3(x_ref[...].astype(F32), prev_row, next_row, w_ref[...]) + b_ref[...]
    o_ref[...] = (y * _sigmoid(y) * sc_ref[...]).astype(o_ref.dtype)


def _conv_glu_kernel(g_ref, gp_ref, gn_ref, v_ref, vp_ref, vn_ref, wg_ref, wv_ref, bg_ref, bv_ref, o_ref,
                     *, tiles_per_seg):
    gp, gn = _halo_rows(gp_ref, gn_ref, tiles_per_seg)
    vp, vn = _halo_rows(vp_ref, vn_ref, tiles_per_seg)
    g = _conv3(g_ref[...].astype(F32), gp, gn, wg_ref[...]) + bg_ref[...]
    v = _conv3(v_ref[...].astype(F32), vp, vn, wv_ref[...]) + bv_ref[...]
    o_ref[...] = (g * _sigmoid(g) * v).astype(o_ref.dtype)


def _conv_specs(m, tm, tc, coff):
    hb = tm // HALO
    last = m // HALO - 1
    return [pl.BlockSpec((tm, tc), lambda i, j: (i, coff + j)),
            pl.BlockSpec((HALO, tc), lambda i, j: (jnp.maximum(i * hb - 1, 0), coff + j)),
            pl.BlockSpec((HALO, tc), lambda i, j: (jnp.minimum((i + 1) * hb, last), coff + j))]


def _conv_silu(x, ncols, w, b, colscale, seg, tc=512):
    m = x.shape[0]
    tm = min(TM, seg)
    vec = lambda r: pl.BlockSpec((r, tc), lambda i, j: (0, j))
    return pl.pallas_call(
        functools.partial(_conv_silu_kernel, tiles_per_seg=seg // tm),
        grid=(m // tm, ncols // tc),
        in_specs=_conv_specs(m, tm, tc, 0) + [vec(3), vec(1), vec(1)],
        out_specs=pl.BlockSpec((tm, tc), lambda i, j: (i, j)),
        out_shape=jax.ShapeDtypeStruct((m, ncols), BF16),
        compiler_params=_cp(("parallel", "parallel")),
    )(x, x, x, w, b, colscale)


def _conv_glu(u, w, b, seg, tc=512):
    m = u.shape[0]
    dff = u.shape[1] // 2
    tm = min(TM, seg)
    nj = dff // tc
    vec = lambda r, off: pl.BlockSpec((r, tc), lambda i, j: (0, off + j))
    return pl.pallas_call(
        functools.partial(_conv_glu_kernel, tiles_per_seg=seg // tm),
        grid=(m // tm, nj),
        in_specs=_conv_specs(m, tm, tc, 0) + _conv_specs(m, tm, tc, nj)
        + [vec(3, 0), vec(3, nj), vec(1, 0), vec(1, nj)],
        out_specs=pl.BlockSpec((tm, tc), lambda i, j: (i, j)),
        out_shape=jax.ShapeDtypeStruct((m, dff), BF16),
        compiler_params=_cp(("parallel", "parallel")),
    )(u, u, u, u, u, u, w, w, b, b)


def _split3(x):
    x1 = x.astype(BF16)
    r = x - x1.astype(F32)
    x2 = r.astype(BF16)
    x3 = (r - x2.astype(F32)).astype(BF16)
    return x1, x2, x3


def _log_sigmoid(x):
    return jnp.minimum(x, 0.0) - jnp.log(1.0 + jnp.exp(-jnp.abs(x)))


def _mlstm_dir(d, q_ref, k_ref, v_ref, gc_ref, gr_ref, bc_ref, br_ref, h_ref, c_scr, m_scr):
    L = ML_L
    ri = lax.broadcasted_iota(jnp.int32, (L, L), 0)
    ci = lax.broadcasted_iota(jnp.int32, (L, L), 1)
    mask = (ci <= ri) if d == 0 else (ci >= ri)
    tri = jnp.where(mask, 1.0, 0.0).astype(BF16)
    tri_t = jnp.where((ri <= ci) if d == 0 else (ri >= ci), 1.0, 0.0).astype(BF16)

    q = q_ref[...]
    k = k_ref[...]
    lane = lax.broadcasted_iota(jnp.int32, (L, ML_DP), 1)
    v = jnp.where(lane == ML_DH, 1.0, v_ref[...].astype(F32)).astype(BF16)

    gc = gc_ref[...] + bc_ref[...]
    gr = gr_ref[...] + br_ref[...]
    b_col = sum(jnp.dot(tri, p, preferred_element_type=F32) for p in _split3(_log_sigmoid(gc)))[:, 2 + d:3 + d]
    b_row = sum(jnp.dot(p, tri_t, preferred_element_type=F32) for p in _split3(_log_sigmoid(gr)))[2 + d:3 + d, :]
    li_col = gc[:, d:d + 1]
    li_row = gr[d:d + 1, :]

    m_prev = m_scr[d][0:1, 0:1]
    ct = c_scr[d]

    dmat = jnp.where(mask, b_col - b_row + li_row, -jnp.inf)
    g = b_col + m_prev
    m_t = jnp.maximum(g, jnp.max(dmat, axis=-1, keepdims=True))
    w_loc = jnp.exp(dmat - m_t)
    w_st = jnp.exp(g - m_t)
    a = lax.dot_general(q, k, (((1,), (1,)), ((), ())), preferred_element_type=F32) * w_loc
    num = (w_st * jnp.dot(q, ct.astype(BF16), preferred_element_type=F32)
           + jnp.dot(a.astype(BF16), v, preferred_element_type=F32))
    den = num[:, ML_DH:ML_DH + 1]
    hv = num / jnp.maximum(jnp.abs(den), jnp.exp(-m_t))
    h_ref[...] = jnp.where(lane < ML_DH, hv, 0.0).astype(h_ref.dtype)

    b_end = b_col[L - 1:L, :] if d == 0 else b_col[0:1, :]
    e_log = b_end - b_col + li_col
    m_new = jnp.maximum(b_end + m_prev, jnp.max(e_log, axis=0, keepdims=True))
    decay = jnp.exp(b_end + m_prev - m_new)
    ek = (jnp.exp(e_log - m_new) * k.astype(F32)).astype(BF16)
    c_scr[d] = decay * ct + lax.dot_general(ek, v, (((0,), (0,)), ((), ())), preferred_element_type=F32)
    m_scr[d] = jnp.broadcast_to(m_new, m_scr.shape[1:])


def _mlstm_kernel(qf_ref, kf_ref, vf_ref, gcf_ref, grf_ref, qb_ref, kb_ref, vb_ref, gcb_ref, grb_ref,
                  bc_ref, br_ref, c0_ref, m0_ref, hf_ref, hb_ref, c1_ref, m1_ref, c_scr, m_scr):
    c = pl.program_id(2)

    @pl.when(c == 0)
    def _():
        c_scr[...] = c0_ref[...]
        m_scr[...] = m0_ref[...]

    _mlstm_dir(0, qf_ref, kf_ref, vf_ref, gcf_ref, grf_ref, bc_ref, br_ref, hf_ref, c_scr, m_scr)
    _mlstm_dir(1, qb_ref, kb_ref, vb_ref, gcb_ref, grb_ref, bc_ref, br_ref, hb_ref, c_scr, m_scr)

    @pl.when(c == pl.num_programs(2) - 1)
    def _():
        c1_ref[...] = c_scr[...]
        m1_ref[...] = m_scr[...]


def _mlstm(qk, proj, gates, gates_row, bias_col, bias_row, c0, m0, batch):
    m = qk.shape[0]
    s = m // batch
    nc = s // ML_L
    fwd = lambda b, c: b * nc + c
    bwd = lambda b, c: b * nc + (nc - 1 - c)
    hq, hk, hv = C_MLQ // ML_DP, C_MLK // ML_DP, C_MLV // ML_DP

    def dir_specs(rf):
        return [pl.BlockSpec((ML_L, ML_DP), lambda b, h, c: (rf(b, c), hq + h)),
                pl.BlockSpec((ML_L, ML_DP), lambda b, h, c: (rf(b, c), hk + h)),
                pl.BlockSpec((ML_L, ML_DP), lambda b, h, c: (rf(b, c), hv + h)),
                pl.BlockSpec((ML_L, LANE), lambda b, h, c: (rf(b, c), h)),
                pl.BlockSpec((None, 8, ML_L), lambda b, h, c: (h, 0, rf(b, c)))]

    st_c = pl.BlockSpec((None, None, 2, ML_DP, ML_DP), lambda b, h, c: (b, h, 0, 0, 0))
    st_m = pl.BlockSpec((None, None, 2, 8, LANE), lambda b, h, c: (b, h, 0, 0, 0))
    return pl.pallas_call(
        _mlstm_kernel,
        grid=(batch, ML_HEADS, nc),
        in_specs=dir_specs(fwd) + dir_specs(bwd)
        + [pl.BlockSpec((None, 1, LANE), lambda b, h, c: (h, 0, 0)),
           pl.BlockSpec((None, 8, 1), lambda b, h, c: (h, 0, 0)), st_c, st_m],
        out_specs=[pl.BlockSpec((ML_L, ML_DP), lambda b, h, c: (fwd(b, c), h)),
                   pl.BlockSpec((ML_L, ML_DP), lambda b, h, c: (bwd(b, c), h)), st_c, st_m],
        out_shape=[jax.ShapeDtypeStruct((m, ML_WP), BF16), jax.ShapeDtypeStruct((m, ML_WP), BF16),
                   jax.ShapeDtypeStruct(c0.shape, F32), jax.ShapeDtypeStruct(m0.shape, F32)],
        scratch_shapes=[pltpu.VMEM((2, ML_DP, ML_DP), F32), pltpu.VMEM((2, 8, LANE), F32)],
        compiler_params=_cp(("parallel", "parallel", "arbitrary")),
    )(qk, qk, proj, gates, gates_row, qk, qk, proj, gates, gates_row, bias_col, bias_row, c0, m0)


def _outproj_kernel(mla_ref, hf_ref, hb_ref, og_ref, na_ref, w1_ref, w2_ref, w3_ref, res_ref, gate_ref,
                    out_ref, ml_scr):
    @pl.when(pl.program_id(1) == 0)
    def _():
        hsum = hf_ref[...].astype(F32) + hb_ref[...].astype(F32)
        ml_scr[...] = (hsum * _sigmoid(og_ref[...].astype(F32))).astype(BF16)

    acc = (jnp.dot(mla_ref[...], w1_ref[...], preferred_element_type=F32)
           + jnp.dot(ml_scr[...], w2_ref[...], preferred_element_type=F32)
           + jnp.dot(na_ref[...], w3_ref[...], preferred_element_type=F32))
    out_ref[...] = res_ref[...] + gate_ref[...] * acc


def _gate_spec(chunk, rowf, tn):
    return pl.BlockSpec((None, None, 1, tn), lambda i, j: (rowf(i), chunk, 0, j))


def _outproj(mla, hf, hb, proj, na, w1, w2, w3, res, mod4, gate_chunk, rowf, tn=1024):
    m, d = res.shape
    tm = min(TM, m)
    rows = lambda width, col: pl.BlockSpec((tm, width), lambda i, j: (i, col))
    wcol = lambda w: pl.BlockSpec((w.shape[0], tn), lambda i, j: (0, j))
    return pl.pallas_call(
        _outproj_kernel,
        grid=(m // tm, d // tn),
        in_specs=[rows(MLA_W, 0), rows(ML_WP, 0), rows(ML_WP, 0), rows(ML_WP, C_MLO // ML_WP), rows(NA_W, 0),
                  wcol(w1), wcol(w2), wcol(w3),
                  pl.BlockSpec((tm, tn), lambda i, j: (i, j)), _gate_spec(gate_chunk, rowf, tn)],
        out_specs=pl.BlockSpec((tm, tn), lambda i, j: (i, j)),
        out_shape=jax.ShapeDtypeStruct((m, d), F32),
        scratch_shapes=[pltpu.VMEM((tm, ML_WP), BF16)],
        compiler_params=_cp(("parallel", "arbitrary")),
    )(mla, hf, hb, proj, na, w1, w2, w3, res, mod4)


def _down_kernel(a_ref, w_ref, res_ref, gate_ref, out_ref):
    acc = jnp.dot(a_ref[...], w_ref[...], preferred_element_type=F32)
    out_ref[...] = res_ref[...] + gate_ref[...] * acc


def _down(act, w, res, mod4, gate_chunk, rowf, tn=512):
    m, d = res.shape
    kdim = act.shape[1]
    tm = min(TM, m)
    return pl.pallas_call(
        _down_kernel,
        grid=(m // tm, d // tn),
        in_specs=[pl.BlockSpec((tm, kdim), lambda i, j: (i, 0)),
                  pl.BlockSpec((kdim, tn), lambda i, j: (0, j)),
                  pl.BlockSpec((tm, tn), lambda i, j: (i, j)), _gate_spec(gate_chunk, rowf, tn)],
        out_specs=pl.BlockSpec((tm, tn), lambda i, j: (i, j)),
        out_shape=jax.ShapeDtypeStruct((m, d), F32),
        compiler_params=_cp(("parallel", "arbitrary")),
    )(act, w, res, mod4)


def _final_norm_kernel(x_ref, g_ref, o_ref):
    o_ref[...] = _rms(x_ref[...], g_ref[...])


def _final_norm(x, g):
    m, d = x.shape
    tm = min(TM, m)
    return pl.pallas_call(
        _final_norm_kernel,
        grid=(m // tm,),
        in_specs=[pl.BlockSpec((tm, d), lambda i: (i, 0)), pl.BlockSpec((1, d), lambda i: (0, 0))],
        out_specs=pl.BlockSpec((tm, d), lambda i: (i, 0)),
        out_shape=jax.ShapeDtypeStruct((m, d), F32),
        compiler_params=_cp(("parallel",)),
    )(x, g.reshape(1, d))


def _pad_heads(a, heads, dh, dp, axis):
    shp = a.shape
    a = a.reshape(shp[:axis] + (heads, dh) + shp[axis + 1:])
    pad = [(0, 0)] * a.ndim
    pad[axis + 1] = (0, dp - dh)
    a = jnp.pad(a, pad)
    return a.reshape(shp[:axis] + (heads * dp,) + shp[axis + 1:])


def _rope_cols(a):
    ev, od = a[..., 0::2], a[..., 1::2]
    return jnp.concatenate([ev, od, od, ev], axis=-1)


def _prep_w_in(w):
    qlat, kvlat, krope, mq, mk, mv, mo, gi, gf, nq, nk, nv = jnp.split(w, IN_OFFSETS, axis=-1)
    d = w.shape[0]
    ph = lambda a: _pad_heads(a, ML_HEADS, ML_DH, ML_DP, 1)
    gi = gi.reshape(d, 2, ML_HEADS)
    gf = gf.reshape(d, 2, ML_HEADS)
    g4 = jnp.stack([gi[:, 0], gi[:, 1], gf[:, 0], gf[:, 1]], axis=-1)
    gates = jnp.pad(g4, ((0, 0), (0, 0), (0, LANE - 4))).reshape(d, C_GATES)
    main = jnp.concatenate([ph(mq), ph(mk), ph(mv), ph(mo), qlat, nq, nk, nv, kvlat, _rope_cols(krope)], axis=-1)
    main = jnp.pad(main, ((0, 0), (0, C_MAIN - main.shape[1])))
    return jnp.concatenate([main, gates], axis=-1).astype(BF16)


def _prep_w_uq(w):
    w = w.reshape(w.shape[0], MLA_HEADS, MLA_NOPE + MLA_ROPE)
    out = jnp.concatenate([w[..., :MLA_NOPE], _rope_cols(w[..., MLA_NOPE:])], axis=-1)
    return out.reshape(w.shape[0], MLA_HEADS * MLA_DQ).astype(BF16)


def _gate_bias(i_bias, f_bias):
    g4 = jnp.stack([i_bias[0], i_bias[1], f_bias[0], f_bias[1]], axis=-1).astype(F32)
    col = jnp.pad(g4, ((0, 0), (0, LANE - 4))).reshape(ML_HEADS, 1, LANE)
    row = jnp.pad(g4, ((0, 0), (0, 4))).reshape(ML_HEADS, 8, 1)
    return col, row


def _rope_table(n_tokens):
    t = jnp.arange(n_tokens)
    row = (t // GRID_W).astype(F32)
    col = (t % GRID_W).astype(F32)
    n_freq = MLA_ROPE // 4
    inv = ROPE_BASE ** (-jnp.arange(n_freq, dtype=F32) / n_freq)
    ang = jnp.concatenate([row[:, None] * inv, col[:, None] * inv], axis=-1)
    cos, sin = jnp.cos(ang), jnp.sin(ang)
    return jnp.concatenate([cos, cos, -sin, sin], axis=-1)


def kernel(x, c, ctx, c_ctx, ada_w, ada_b, norm1_g, norm2_g, w_in, mla_q_norm, mla_kv_norm, mla_w_uq, mla_w_uk, mla_w_uv, ml_conv_w, ml_conv_b, ml_i_bias, ml_f_bias, na_rpb, w_out, ffn_w_up, ffn_conv_w, ffn_conv_b, ffn_w_down, final_norm_g):
    batch, seq, d = x.shape
    tctx = ctx.shape[1]
    depth = ada_w.shape[0]
    assert d == D_MODEL and batch <= 7 and tctx == ML_L and tctx % HALO == 0
    assert seq % TM == 0 and seq % NA_BLK == 0 and seq // GRID_W >= NA_KR

    h = x.reshape(batch * seq, d)
    hc = ctx.reshape(batch * tctx, d)
    tm_l = min(TM, seq)
    row_lat = lambda i: i // (seq // tm_l)
    row_ctx = lambda i: batch
    cond8 = jnp.zeros((8, d), F32).at[:batch].set(c).at[batch].set(c_ctx)

    tab_lat = _rope_table(seq)
    tab_ctx = jnp.concatenate([jnp.ones((tctx, 2 * (MLA_ROPE // 2)), F32), jnp.zeros((tctx, MLA_ROPE), F32)], axis=-1)
    ml_scale = jnp.concatenate([jnp.ones((1, ML_WP), F32), jnp.full((1, ML_WP), ML_DH ** -0.5, F32)], axis=-1)

    for l in range(depth):
        last = l == depth - 1
        mod4 = _ada(cond8, ada_w[l], ada_b[l]).reshape(8, N_MOD, 1, d)

        w_in_p = _prep_w_in(w_in[l])
        wuq = _prep_w_uq(mla_w_uq[l])
        wuk = mla_w_uk[l].astype(BF16)
        wuv = mla_w_uv[l].astype(BF16)
        qg = mla_q_norm[l].reshape(1, -1)
        kvg = mla_kv_norm[l].reshape(1, -1)
        cw = jnp.concatenate([_pad_heads(ml_conv_w[l][:, :ML_W], ML_HEADS, ML_DH, ML_DP, 1),
                              _pad_heads(ml_conv_w[l][:, ML_W:], ML_HEADS, ML_DH, ML_DP, 1)], axis=-1)
        cb = jnp.concatenate([_pad_heads(ml_conv_b[l][None, :ML_W], ML_HEADS, ML_DH, ML_DP, 1),
                              _pad_heads(ml_conv_b[l][None, ML_W:], ML_HEADS, ML_DH, ML_DP, 1)], axis=-1)
        bias_col, bias_row = _gate_bias(ml_i_bias[l], ml_f_bias[l])
        na_bias = _na_bias_table(na_rpb[l])
        w1 = w_out[l][:MLA_W].astype(BF16)
        w2 = _pad_heads(w_out[l][MLA_W:MLA_W + ML_W], ML_HEADS, ML_DH, ML_DP, 0).astype(BF16)
        w3 = w_out[l][MLA_W + ML_W:].astype(BF16)
        w_up = ffn_w_up[l].astype(BF16)
        w_down = ffn_w_down[l].astype(BF16)

        def mixers_in(hh, rowf):
            proj, gates = _nm_mm(hh, norm1_g[l], mod4, 1, 0, rowf, w_in_p, TN_IN, tail_cols=C_GATES)
            m = hh.shape[0]
            gates_row = gates.reshape(m, ML_HEADS, LANE)[:, :, :8].transpose(1, 2, 0)
            return proj, gates, gates_row

        proj_c, gates_c, grow_c = mixers_in(hc, row_ctx)
        proj_l, gates_l, grow_l = mixers_in(h, row_lat)

        q_c, k_c, v_c = _mla_up(proj_c, tab_ctx, qg, kvg, wuq, wuk, wuv)
        q_l, k_l, v_l = _mla_up(proj_l, tab_lat, qg, kvg, wuq, wuk, wuv)
        mla_lat = _flash(q_l, 0, k_c, 0, v_c, 0, MLA_HEADS, MLA_DQ, MLA_V, batch, lat=(k_l, 0, v_l, 0))

        qk_c = _conv_silu(proj_c, 2 * ML_WP, cw, cb, ml_scale, tctx)
        qk_l = _conv_silu(proj_l, 2 * ML_WP, cw, cb, ml_scale, seq)
        c0 = jnp.zeros((batch, ML_HEADS, 2, ML_DP, ML_DP), F32)
        m0 = jnp.zeros((batch, ML_HEADS, 2, 8, LANE), F32)
        hf_c, hb_c, c1, m1 = _mlstm(qk_c, proj_c, gates_c, grow_c, bias_col, bias_row, c0, m0, batch)
        hf_l, hb_l, _, _ = _mlstm(qk_l, proj_l, gates_l, grow_l, bias_col, bias_row, c1, m1, batch)

        na_lat = _na(proj_l, proj_c, na_bias, batch)

        h = _outproj(mla_lat, hf_l, hb_l, proj_l, na_lat, w1, w2, w3, h, mod4, 2, row_lat)
        u = _nm_mm(h, norm2_g[l], mod4, 4, 3, row_lat, w_up, 512)
        h = _down(_conv_glu(u, ffn_conv_w[l], ffn_conv_b[l].reshape(1, -1), seq), w_down, h, mod4, 5, row_lat)

        if not last:
            mla_ctx = _flash(q_c, 0, k_c, 0, v_c, 0, MLA_HEADS, MLA_DQ, MLA_V, batch)
            na_ctx = _flash(proj_c, C_NAQ // NA_DH, proj_c, C_NAK // NA_DH, proj_c, C_NAV // NA_DH,
                            NA_HEADS, NA_DH, NA_DH, batch, scale=NA_DH ** -0.5)
            hc = _outproj(mla_ctx, hf_c, hb_c, proj_c, na_ctx, w1, w2, w3, hc, mod4, 2, row_ctx)
            uc = _nm_mm(hc, norm2_g[l], mod4, 4, 3, row_ctx, w_up, 512)
            hc = _down(_conv_glu(uc, ffn_conv_w[l], ffn_conv_b[l].reshape(1, -1), tctx), w_down, hc, mod4, 5, row_ctx)

    return _final_norm(h, final_norm_g).reshape(batch, seq, d)
```

```python
import functools
import math

import numpy as np
import jax
import jax.numpy as jnp
from jax import lax
from jax.experimental import pallas as pl
from jax.experimental.pallas import tpu as pltpu

F32 = jnp.float32
BF16 = jnp.bfloat16

D_MODEL = 2048
GRID_W = 64
CTX_LEN = 256
MLA_HEADS = 6
MLA_NOPE = 128
MLA_ROPE = 64
MLA_V = 128
MLA_Q_LORA = 512
MLA_KV_LORA = 256
ML_HEADS = 4
ML_DH = 192
NA_HEADS = 4
NA_DH = 128
NA_KR = 8
NA_KC = 16
D_FF = 256 * ((8 * D_MODEL // 3 + 255) // 256)
ROPE_BASE = 10000.0
EPS = 1e-6
N_MOD = 6
ML_W = ML_HEADS * ML_DH
NA_W = NA_HEADS * NA_DH
MLA_W = MLA_HEADS * MLA_V
IN_SIZES = (MLA_Q_LORA, MLA_KV_LORA, MLA_ROPE, ML_W, ML_W, ML_W, ML_W, 2 * ML_HEADS, 2 * ML_HEADS, NA_W, NA_W, NA_W)
IN_OFFSETS = tuple(sum(IN_SIZES[:i + 1]) for i in range(len(IN_SIZES) - 1))

LANE = 128
ML_DP = 256
ML_WP = ML_HEADS * ML_DP
ML_L = 256
MLA_DQ = 256
TM = 512
VMEM_LIMIT = 52 * 1024 * 1024

C_MLQ = 0
C_MLK = C_MLQ + ML_WP
C_MLV = C_MLK + ML_WP
C_MLO = C_MLV + ML_WP
C_QLAT = C_MLO + ML_WP
C_NAQ = C_QLAT + MLA_Q_LORA
C_NAK = C_NAQ + NA_W
C_NAV = C_NAK + NA_W
C_KVLAT = C_NAV + NA_W
C_KROPE = C_KVLAT + MLA_KV_LORA
C_MAIN = 6656
C_GATES = ML_HEADS * LANE
TN_IN = 512


def _cp(sem):
    return pltpu.CompilerParams(dimension_semantics=sem, vmem_limit_bytes=VMEM_LIMIT)


def _sigmoid(x):
    return 1.0 / (1.0 + jnp.exp(-x))


def _rms(x, g):
    return x * lax.rsqrt(jnp.mean(x * x, axis=-1, keepdims=True) + EPS) * g


def _ada_kernel(c_ref, w_ref, b_ref, o_ref):
    c = c_ref[...]
    a = c * _sigmoid(c)
    o_ref[...] = jnp.dot(a, w_ref[...], preferred_element_type=F32,
                         precision=lax.Precision.HIGHEST) + b_ref[...]


def _ada(cond8, w, b):
    d, n = w.shape
    tn = 1024
    return pl.pallas_call(
        _ada_kernel,
        grid=(n // tn,),
        in_specs=[pl.BlockSpec((8, d), lambda j: (0, 0)),
                  pl.BlockSpec((d, tn), lambda j: (0, j)),
                  pl.BlockSpec((1, tn), lambda j: (0, j))],
        out_specs=pl.BlockSpec((8, tn), lambda j: (0, j)),
        out_shape=jax.ShapeDtypeStruct((8, n), F32),
        compiler_params=_cp(("arbitrary",)),
    )(cond8, w, b.reshape(1, n))


def _nm_mm_kernel(x_ref, g_ref, sc_ref, sh_ref, w_ref, o_ref, *rest, has_tail):
    if has_tail:
        t_ref, hn_ref = rest
    else:
        (hn_ref,) = rest
    j = pl.program_id(1)

    @pl.when(j == 0)
    def _():
        y = _rms(x_ref[...], g_ref[...])
        hn_ref[...] = (y * (1.0 + sc_ref[...]) + sh_ref[...]).astype(BF16)

    acc = jnp.dot(hn_ref[...], w_ref[...], preferred_element_type=F32)
    if has_tail:
        nj = pl.num_programs(1)

        @pl.when(j < nj - 1)
        def _():
            o_ref[...] = acc.astype(o_ref.dtype)

        @pl.when(j == nj - 1)
        def _():
            t_ref[...] = acc
    else:
        o_ref[...] = acc.astype(o_ref.dtype)


def _mod_spec(chunk, rowf):
    return pl.BlockSpec((None, None, 1, D_MODEL), lambda i, j: (rowf(i), chunk, 0, 0))


def _nm_mm(x, g, mod4, sc_chunk, sh_chunk, rowf, w, tn, tail_cols=0):
    m, d = x.shape
    n = w.shape[1]
    tm = min(TM, m)
    nj = n // tn
    has_tail = tail_cols > 0
    n_main = n - tail_cols
    if has_tail:
        o_spec = [pl.BlockSpec((tm, tn), lambda i, j: (i, jnp.minimum(j, nj - 2))),
                  pl.BlockSpec((tm, tn), lambda i, j: (i, 0))]
        o_shape = [jax.ShapeDtypeStruct((m, n_main), BF16), jax.ShapeDtypeStruct((m, tail_cols), F32)]
    else:
        o_spec = pl.BlockSpec((tm, tn), lambda i, j: (i, j))
        o_shape = jax.ShapeDtypeStruct((m, n), BF16)
    return pl.pallas_call(
        functools.partial(_nm_mm_kernel, has_tail=has_tail),
        grid=(m // tm, nj),
        in_specs=[pl.BlockSpec((tm, d), lambda i, j: (i, 0)),
                  pl.BlockSpec((1, d), lambda i, j: (0, 0)),
                  _mod_spec(sc_chunk, rowf), _mod_spec(sh_chunk, rowf),
                  pl.BlockSpec((d, tn), lambda i, j: (0, j))],
        out_specs=o_spec,
        out_shape=o_shape,
        scratch_shapes=[pltpu.VMEM((tm, d), BF16)],
        compiler_params=_cp(("parallel", "arbitrary")),
    )(x, g.reshape(1, d), mod4, mod4, w)


def _mla_up_kernel(ql_ref, kvl_ref, kr_ref, tab_ref, qg_ref, kvg_ref, wuq_ref, wuk_ref, wuv_ref,
                   q_ref, k_ref, v_ref, *, scale):
    tab = tab_ref[...]
    qn = _rms(ql_ref[...].astype(F32), qg_ref[...]).astype(BF16)
    qf = jnp.dot(qn, wuq_ref[...], preferred_element_type=F32)
    for h in range(MLA_HEADS):
        c0 = h * MLA_DQ
        y = qf[:, c0 + MLA_NOPE:c0 + MLA_DQ] * tab
        z = y + pltpu.roll(y, MLA_ROPE, axis=1)
        q_ref[:, c0:c0 + MLA_NOPE] = (qf[:, c0:c0 + MLA_NOPE] * scale).astype(BF16)
        q_ref[:, c0 + MLA_NOPE:c0 + MLA_DQ] = (z * scale).astype(BF16)
    cn = _rms(kvl_ref[...].astype(F32), kvg_ref[...]).astype(BF16)
    kn = jnp.dot(cn, wuk_ref[...], preferred_element_type=F32)
    v_ref[...] = jnp.dot(cn, wuv_ref[...], preferred_element_type=F32).astype(BF16)
    ky = kr_ref[...].astype(F32) * tab
    kz = ky + pltpu.roll(ky, MLA_ROPE, axis=1)
    lane = lax.broadcasted_iota(jnp.int32, kz.shape, 1)
    kz = jnp.where(lane < MLA_ROPE, kz, 0.0).astype(BF16)
    for h in range(MLA_HEADS):
        c0 = h * MLA_DQ
        k_ref[:, c0:c0 + MLA_NOPE] = kn[:, h * MLA_NOPE:(h + 1) * MLA_NOPE].astype(BF16)
        k_ref[:, c0 + MLA_NOPE:c0 + MLA_DQ] = kz


def _mla_up(proj, tab, qg, kvg, wuq, wuk, wuv):
    m = proj.shape[0]
    tm = min(TM, tab.shape[0])
    nt = tab.shape[0] // tm
    full = lambda a: pl.BlockSpec(a.shape, lambda i: (0, 0))
    return pl.pallas_call(
        functools.partial(_mla_up_kernel, scale=(MLA_NOPE + MLA_ROPE) ** -0.5 * LOG2E),
        grid=(m // tm,),
        in_specs=[pl.BlockSpec((tm, MLA_Q_LORA), lambda i: (i, C_QLAT // MLA_Q_LORA)),
                  pl.BlockSpec((tm, MLA_KV_LORA), lambda i: (i, C_KVLAT // MLA_KV_LORA)),
                  pl.BlockSpec((tm, LANE), lambda i: (i, C_KROPE // LANE)),
                  pl.BlockSpec((tm, LANE), lambda i: (i % nt, 0)),
                  full(qg), full(kvg), full(wuq), full(wuk), full(wuv)],
        out_specs=[pl.BlockSpec((tm, MLA_HEADS * MLA_DQ), lambda i: (i, 0)),
                   pl.BlockSpec((tm, MLA_HEADS * MLA_DQ), lambda i: (i, 0)),
                   pl.BlockSpec((tm, MLA_W), lambda i: (i, 0))],
        out_shape=[jax.ShapeDtypeStruct((m, MLA_HEADS * MLA_DQ), BF16),
                   jax.ShapeDtypeStruct((m, MLA_HEADS * MLA_DQ), BF16),
                   jax.ShapeDtypeStruct((m, MLA_W), BF16)],
        compiler_params=_cp(("parallel",)),
    )(proj, proj, proj, tab, qg, kvg, wuq, wuk, wuv)


LOG2E = math.log2(math.e)


def _flash_kernel(*refs, has_lat, scale, nsub, tk):
    if has_lat:
        q_ref, kc_ref, vc_ref, kl_ref, vl_ref, o_ref, m_scr, acc_scr = refs
    else:
        q_ref, kc_ref, vc_ref, o_ref, m_scr, acc_scr = refs
    ts = q_ref.shape[0] // nsub
    dv = vc_ref.shape[1]
    dn = (((1,), (1,)), ((), ()))

    def scores(q, k):
        s = lax.dot_general(q, k, dn, preferred_element_type=F32)
        return s if scale == 1.0 else s * scale

    def with_ones(v):
        lane = lax.broadcasted_iota(jnp.int32, v.shape, 1)
        return jnp.concatenate([v, jnp.where(lane == 0, 1.0, 0.0).astype(v.dtype)], axis=1)

    def lanes(x, n):
        return x if n == LANE else pltpu.repeat(x, n // LANE, axis=1)

    kc = kc_ref[...]
    vc = with_ones(vc_ref[...])
    for u in range(nsub):
        s = scores(q_ref[u * ts:(u + 1) * ts, :], kc)
        m = jnp.broadcast_to(jnp.max(s, axis=-1, keepdims=True), (ts, LANE))
        m_scr[u] = m
        acc_scr[u] = jnp.dot(jnp.exp2(s - lanes(m, s.shape[1])).astype(BF16), vc, preferred_element_type=F32)

    if has_lat:
        def body(kb, carry):
            off = pl.multiple_of(kb * tk, tk)
            k = kl_ref[pl.ds(off, tk), :]
            v = with_ones(vl_ref[pl.ds(off, tk), :])
            for u in range(nsub):
                s = scores(q_ref[u * ts:(u + 1) * ts, :], k)
                m_prev = m_scr[u]
                m_new = jnp.maximum(m_prev, jnp.max(s, axis=-1, keepdims=True))
                p = jnp.exp2(s - lanes(m_new, tk)).astype(BF16)
                m_scr[u] = m_new
                acc_scr[u] = (lanes(jnp.exp2(m_prev - m_new), 2 * dv) * acc_scr[u]
                              + jnp.dot(p, v, preferred_element_type=F32))
            return carry

        lax.fori_loop(0, kl_ref.shape[0] // tk, body, 0)

    for u in range(nsub):
        acc = acc_scr[u]
        o_ref[u * ts:(u + 1) * ts, :] = (acc[:, :dv] / acc[:, dv:dv + 1]).astype(o_ref.dtype)


def _flash(q, qcol, kc, kccol, vc, vccol, heads, dq, dv, batch, lat=None, scale=1.0, tq=2048, nsub=4, tk=1024):
    mq = q.shape[0]
    sq = mq // batch
    tq = min(tq, sq)
    nsub = min(nsub, tq // 256)
    nq = sq // tq
    tc = kc.shape[0] // batch
    has_lat = lat is not None
    in_specs = [pl.BlockSpec((tq, dq), lambda b, h, i: (b * nq + i, qcol + h)),
                pl.BlockSpec((tc, dq), lambda b, h, i: (b, kccol + h)),
                pl.BlockSpec((tc, dv), lambda b, h, i: (b, vccol + h))]
    args = [q, kc, vc]
    if has_lat:
        kl, klcol, vl, vlcol = lat
        sk = kl.shape[0] // batch
        tk = min(tk, sk)
        in_specs += [pl.BlockSpec((sk, dq), lambda b, h, i: (b, klcol + h)),
                     pl.BlockSpec((sk, dv), lambda b, h, i: (b, vlcol + h))]
        args += [kl, vl]
    return pl.pallas_call(
        functools.partial(_flash_kernel, has_lat=has_lat, scale=scale, nsub=nsub, tk=tk),
        grid=(batch, heads, nq),
        in_specs=in_specs,
        out_specs=pl.BlockSpec((tq, dv), lambda b, h, i: (b * nq + i, h)),
        out_shape=jax.ShapeDtypeStruct((mq, heads * dv), BF16),
        scratch_shapes=[pltpu.VMEM((nsub, tq // nsub, LANE), F32), pltpu.VMEM((nsub, tq // nsub, 2 * dv), F32)],
        compiler_params=_cp(("parallel", "parallel", "arbitrary")),
    )(*args)


NA_RB = 8
NA_BLK = NA_RB * GRID_W


def _na_kernel(q_ref, kp_ref, kc_ref, kn_ref, vp_ref, vc_ref, vn_ref, kx_ref, vx_ref, bias_ref, o_ref,
               kw_scr, vw_scr, *, rows, scale):
    j = pl.program_id(1)
    for t, (kr, vr) in enumerate(((kp_ref, vp_ref), (kc_ref, vc_ref), (kn_ref, vn_ref))):
        kw_scr[t * NA_BLK:(t + 1) * NA_BLK, :] = kr[...]
        vw_scr[t * NA_BLK:(t + 1) * NA_BLK, :] = vr[...]
    nwin = NA_KR * GRID_W
    dn = (((1,), (1,)), ((), ()))
    for i in range(NA_RB):
        r = NA_RB * j + i
        r0 = jnp.clip(r - NA_KR // 2, 0, rows - NA_KR)
        start = pl.multiple_of((r0 - NA_RB * j + NA_RB) * GRID_W, GRID_W)
        var = r - r0
        for h in range(NA_HEADS):
            cs = slice(h * NA_DH, (h + 1) * NA_DH)
            q = q_ref[i * GRID_W:(i + 1) * GRID_W, cs]
            kwin = kw_scr[pl.ds(start, nwin), cs]
            vwin = vw_scr[pl.ds(start, nwin), cs]
            s_loc = lax.dot_general(q, kwin, dn, preferred_element_type=F32) * scale + bias_ref[h, var]
            s_ctx = lax.dot_general(q, kx_ref[:, cs], dn, preferred_element_type=F32) * scale
            m = jnp.maximum(jnp.max(s_loc, axis=-1, keepdims=True), jnp.max(s_ctx, axis=-1, keepdims=True))
            p_loc = jnp.exp(s_loc - m)
            p_ctx = jnp.exp(s_ctx - m)
            l = jnp.sum(p_loc, axis=-1, keepdims=True) + jnp.sum(p_ctx, axis=-1, keepdims=True)
            o = (jnp.dot(p_loc.astype(BF16), vwin, preferred_element_type=F32)
                 + jnp.dot(p_ctx.astype(BF16), vx_ref[:, cs], preferred_element_type=F32))
            o_ref[i * GRID_W:(i + 1) * GRID_W, cs] = (o / l).astype(o_ref.dtype)


def _na_bias_table(rpb):
    c = np.arange(GRID_W)
    cs = np.clip(c - NA_KC // 2, 0, GRID_W - NA_KC)
    valid = (c[None, :] >= cs[:, None]) & (c[None, :] < cs[:, None] + NA_KC)
    dc = np.clip(c[None, :] - c[:, None] + (NA_KC - 1), 0, 2 * NA_KC - 2)
    dr = np.arange(NA_KR)[None, :] - np.arange(NA_KR)[:, None] + (NA_KR - 1)
    sel_r = (dr[:, :, None] == np.arange(2 * NA_KR - 1)).astype(np.float32)
    sel_c = ((dc[:, :, None] == np.arange(2 * NA_KC - 1)) & valid[:, :, None]).astype(np.float32)
    tab = jnp.einsum('hab,via,cdb->hvcid', rpb.astype(F32), sel_r, sel_c, precision=lax.Precision.HIGHEST)
    tab = tab + jnp.where(valid[None, None, :, None, :], 0.0, -1e30)
    return tab.reshape(NA_HEADS, NA_KR, GRID_W, NA_KR * GRID_W)


def _na(proj_l, proj_c, bias, batch):
    m = proj_l.shape[0]
    s = m // batch
    rows = s // GRID_W
    nb = rows // NA_RB
    cq, ck, cv = C_NAQ // NA_W, C_NAK // NA_W, C_NAV // NA_W
    tc = proj_c.shape[0] // batch
    prev = lambda b, j: b * nb + jnp.maximum(j - 1, 0)
    cur = lambda b, j: b * nb + j
    nxt = lambda b, j: b * nb + jnp.minimum(j + 1, nb - 1)
    blk = lambda rf, col: pl.BlockSpec((NA_BLK, NA_W), lambda b, j: (rf(b, j), col))
    return pl.pallas_call(
        functools.partial(_na_kernel, rows=rows, scale=NA_DH ** -0.5),
        grid=(batch, nb),
        in_specs=[blk(cur, cq), blk(prev, ck), blk(cur, ck), blk(nxt, ck),
                  blk(prev, cv), blk(cur, cv), blk(nxt, cv),
                  pl.BlockSpec((tc, NA_W), lambda b, j: (b, ck)),
                  pl.BlockSpec((tc, NA_W), lambda b, j: (b, cv)),
                  pl.BlockSpec(bias.shape, lambda b, j: (0, 0, 0, 0))],
        out_specs=pl.BlockSpec((NA_BLK, NA_W), lambda b, j: (b * nb + j, 0)),
        out_shape=jax.ShapeDtypeStruct((m, NA_W), BF16),
        scratch_shapes=[pltpu.VMEM((3 * NA_BLK, NA_W), BF16), pltpu.VMEM((3 * NA_BLK, NA_W), BF16)],
        compiler_params=_cp(("parallel", "arbitrary")),
    )(proj_l, proj_l, proj_l, proj_l, proj_l, proj_l, proj_l, proj_c, proj_c, bias)


def _conv3(x, prev_row, next_row, w):
    tm = x.shape[0]
    rid = lax.broadcasted_iota(jnp.int32, x.shape, 0)
    xp = jnp.where(rid == 0, prev_row, pltpu.roll(x, 1, axis=0))
    xn = jnp.where(rid == tm - 1, next_row, pltpu.roll(x, tm - 1, axis=0))
    return xp * w[0:1, :] + x * w[1:2, :] + xn * w[2:3, :]


HALO = 16


def _halo_rows(p_ref, n_ref, tiles_per_seg):
    i = pl.program_id(0)
    t = i % tiles_per_seg
    prev_row = jnp.where(t == 0, 0.0, p_ref[HALO - 1:HALO, :].astype(F32))
    next_row = jnp.where(t == tiles_per_seg - 1, 0.0, n_ref[0:1, :].astype(F32))
    return prev_row, next_row


def _conv_silu_kernel(x_ref, p_ref, n_ref, w_ref, b_ref, sc_ref, o_ref, *, tiles_per_seg):
    prev_row, next_row = _halo_rows(p_ref, n_ref, tiles_per_seg)
    y = _conv3(x_ref[...].astype(F32), prev_row, next_row, w_ref[...]) + b_ref[...]
    o_ref[...] = (y * _sigmoid(y) * sc_ref[...]).astype(o_ref.dtype)


def _conv_glu_kernel(g_ref, gp_ref, gn_ref, v_ref, vp_ref, vn_ref, wg_ref, wv_ref, bg_ref, bv_ref, o_ref,
                     *, tiles_per_seg):
    gp, gn = _halo_rows(gp_ref, gn_ref, tiles_per_seg)
    vp, vn = _halo_rows(vp_ref, vn_ref, tiles_per_seg)
    g = _conv3(g_ref[...].astype(F32), gp, gn, wg_ref[...]) + bg_ref[...]
    v = _conv3(v_ref[...].astype(F32), vp, vn, wv_ref[...]) + bv_ref[...]
    o_ref[...] = (g * _sigmoid(g) * v).astype(o_ref.dtype)


def _conv_specs(m, tm, tc, coff):
    hb = tm // HALO
    last = m // HALO - 1
    return [pl.BlockSpec((tm, tc), lambda i, j: (i, coff + j)),
            pl.BlockSpec((HALO, tc), lambda i, j: (jnp.maximum(i * hb - 1, 0), coff + j)),
            pl.BlockSpec((HALO, tc), lambda i, j: (jnp.minimum((i + 1) * hb, last), coff + j))]


def _conv_silu(x, ncols, w, b, colscale, seg, tc=512):
    m = x.shape[0]
    tm = min(TM, seg)
    vec = lambda r: pl.BlockSpec((r, tc), lambda i, j: (0, j))
    return pl.pallas_call(
        functools.partial(_conv_silu_kernel, tiles_per_seg=seg // tm),
        grid=(m // tm, ncols // tc),
        in_specs=_conv_specs(m, tm, tc, 0) + [vec(3), vec(1), vec(1)],
        out_specs=pl.BlockSpec((tm, tc), lambda i, j: (i, j)),
        out_shape=jax.ShapeDtypeStruct((m, ncols), BF16),
        compiler_params=_cp(("parallel", "parallel")),
    )(x, x, x, w, b, colscale)


def _conv_glu(u, w, b, seg, tc=512):
    m = u.shape[0]
    dff = u.shape[1] // 2
    tm = min(TM, seg)
    nj = dff // tc
    vec = lambda r, off: pl.BlockSpec((r, tc), lambda i, j: (0, off + j))
    return pl.pallas_call(
        functools.partial(_conv_glu_kernel, tiles_per_seg=seg // tm),
        grid=(m // tm, nj),
        in_specs=_conv_specs(m, tm, tc, 0) + _conv_specs(m, tm, tc, nj)
        + [vec(3, 0), vec(3, nj), vec(1, 0), vec(1, nj)],
        out_specs=pl.BlockSpec((tm, tc), lambda i, j: (i, j)),
        out_shape=jax.ShapeDtypeStruct((m, dff), BF16),
        compiler_params=_cp(("parallel", "parallel")),
    )(u, u, u, u, u, u, w, w, b, b)


def _split3(x):
    x1 = x.astype(BF16)
    r = x - x1.astype(F32)
    x2 = r.astype(BF16)
    x3 = (r - x2.astype(F32)).astype(BF16)
    return x1, x2, x3


def _log_sigmoid(x):
    return jnp.minimum(x, 0.0) - jnp.log(1.0 + jnp.exp(-jnp.abs(x)))


def _mlstm_dir(d, q_ref, k_ref, v_ref, gc_ref, gr_ref, bc_ref, br_ref, h_ref, c_scr, m_scr):
    L = ML_L
    ri = lax.broadcasted_iota(jnp.int32, (L, L), 0)
    ci = lax.broadcasted_iota(jnp.int32, (L, L), 1)
    mask = (ci <= ri) if d == 0 else (ci >= ri)
    tri = jnp.where(mask, 1.0, 0.0).astype(BF16)
    tri_t = jnp.where((ri <= ci) if d == 0 else (ri >= ci), 1.0, 0.0).astype(BF16)

    q = q_ref[...]
    k = k_ref[...]
    lane = lax.broadcasted_iota(jnp.int32, (L, ML_DP), 1)
    v = jnp.where(lane == ML_DH, 1.0, v_ref[...].astype(F32)).astype(BF16)

    gc = gc_ref[...] + bc_ref[...]
    gr = gr_ref[...] + br_ref[...]
    b_col = sum(jnp.dot(tri, p, preferred_element_type=F32) for p in _split3(_log_sigmoid(gc)))[:, 2 + d:3 + d]
    b_row = sum(jnp.dot(p, tri_t, preferred_element_type=F32) for p in _split3(_log_sigmoid(gr)))[2 + d:3 + d, :]
    li_col = gc[:, d:d + 1]
    li_row = gr[d:d + 1, :]

    m_prev = m_scr[d][0:1, 0:1]
    ct = c_scr[d]

    dmat = jnp.where(mask, b_col - b_row + li_row, -jnp.inf)
    g = b_col + m_prev
    m_t = jnp.maximum(g, jnp.max(dmat, axis=-1, keepdims=True))
    w_loc = jnp.exp(dmat - m_t)
    w_st = jnp.exp(g - m_t)
    a = lax.dot_general(q, k, (((1,), (1,)), ((), ())), preferred_element_type=F32) * w_loc
    num = (w_st * jnp.dot(q, ct.astype(BF16), preferred_element_type=F32)
           + jnp.dot(a.astype(BF16), v, preferred_element_type=F32))
    den = num[:, ML_DH:ML_DH + 1]
    hv = num / jnp.maximum(jnp.abs(den), jnp.exp(-m_t))
    h_ref[...] = jnp.where(lane < ML_DH, hv, 0.0).astype(h_ref.dtype)

    b_end = b_col[L - 1:L, :] if d == 0 else b_col[0:1, :]
    e_log = b_end - b_col + li_col
    m_new = jnp.maximum(b_end + m_prev, jnp.max(e_log, axis=0, keepdims=True))
    decay = jnp.exp(b_end + m_prev - m_new)
    ek = (jnp.exp(e_log - m_new) * k.astype(F32)).astype(BF16)
    c_scr[d] = decay * ct + lax.dot_general(ek, v, (((0,), (0,)), ((), ())), preferred_element_type=F32)
    m_scr[d] = jnp.broadcast_to(m_new, m_scr.shape[1:])


def _mlstm_kernel(qf_ref, kf_ref, vf_ref, gcf_ref, grf_ref, qb_ref, kb_ref, vb_ref, gcb_ref, grb_ref,
                  bc_ref, br_ref, c0_ref, m0_ref, hf_ref, hb_ref, c1_ref, m1_ref, c_scr, m_scr):
    c = pl.program_id(2)

    @pl.when(c == 0)
    def _():
        c_scr[...] = c0_ref[...]
        m_scr[...] = m0_ref[...]

    _mlstm_dir(0, qf_ref, kf_ref, vf_ref, gcf_ref, grf_ref, bc_ref, br_ref, hf_ref, c_scr, m_scr)
    _mlstm_dir(1, qb_ref, kb_ref, vb_ref, gcb_ref, grb_ref, bc_ref, br_ref, hb_ref, c_scr, m_scr)

    @pl.when(c == pl.num_programs(2) - 1)
    def _():
        c1_ref[...] = c_scr[...]
        m1_ref[...] = m_scr[...]


def _mlstm(qk, proj, gates, gates_row, bias_col, bias_row, c0, m0, batch):
    m = qk.shape[0]
    s = m // batch
    nc = s // ML_L
    fwd = lambda b, c: b * nc + c
    bwd = lambda b, c: b * nc + (nc - 1 - c)
    hq, hk, hv = C_MLQ // ML_DP, C_MLK // ML_DP, C_MLV // ML_DP

    def dir_specs(rf):
        return [pl.BlockSpec((ML_L, ML_DP), lambda b, h, c: (rf(b, c), hq + h)),
                pl.BlockSpec((ML_L, ML_DP), lambda b, h, c: (rf(b, c), hk + h)),
                pl.BlockSpec((ML_L, ML_DP), lambda b, h, c: (rf(b, c), hv + h)),
                pl.BlockSpec((ML_L, LANE), lambda b, h, c: (rf(b, c), h)),
                pl.BlockSpec((None, 8, ML_L), lambda b, h, c: (h, 0, rf(b, c)))]

    st_c = pl.BlockSpec((None, None, 2, ML_DP, ML_DP), lambda b, h, c: (b, h, 0, 0, 0))
    st_m = pl.BlockSpec((None, None, 2, 8, LANE), lambda b, h, c: (b, h, 0, 0, 0))
    return pl.pallas_call(
        _mlstm_kernel,
        grid=(batch, ML_HEADS, nc),
        in_specs=dir_specs(fwd) + dir_specs(bwd)
        + [pl.BlockSpec((None, 1, LANE), lambda b, h, c: (h, 0, 0)),
           pl.BlockSpec((None, 8, 1), lambda b, h, c: (h, 0, 0)), st_c, st_m],
        out_specs=[pl.BlockSpec((ML_L, ML_DP), lambda b, h, c: (fwd(b, c), h)),
                   pl.BlockSpec((ML_L, ML_DP), lambda b, h, c: (bwd(b, c), h)), st_c, st_m],
        out_shape=[jax.ShapeDtypeStruct((m, ML_WP), BF16), jax.ShapeDtypeStruct((m, ML_WP), BF16),
                   jax.ShapeDtypeStruct(c0.shape, F32), jax.ShapeDtypeStruct(m0.shape, F32)],
        scratch_shapes=[pltpu.VMEM((2, ML_DP, ML_DP), F32), pltpu.VMEM((2, 8, LANE), F32)],
        compiler_params=_cp(("parallel", "parallel", "arbitrary")),
    )(qk, qk, proj, gates, gates_row, qk, qk, proj, gates, gates_row, bias_col, bias_row, c0, m0)


def _outproj_kernel(mla_ref, hf_ref, hb_ref, og_ref, na_ref, w1_ref, w2_ref, w3_ref, res_ref, gate_ref,
                    out_ref, ml_scr):
    @pl.when(pl.program_id(1) == 0)
    def _():
        hsum = hf_ref[...].astype(F32) + hb_ref[...].astype(F32)
        ml_scr[...] = (hsum * _sigmoid(og_ref[...].astype(F32))).astype(BF16)

    acc = (jnp.dot(mla_ref[...], w1_ref[...], preferred_element_type=F32)
           + jnp.dot(ml_scr[...], w2_ref[...], preferred_element_type=F32)
           + jnp.dot(na_ref[...], w3_ref[...], preferred_element_type=F32))
    out_ref[...] = res_ref[...] + gate_ref[...] * acc


def _gate_spec(chunk, rowf, tn):
    return pl.BlockSpec((None, None, 1, tn), lambda i, j: (rowf(i), chunk, 0, j))


def _outproj(mla, hf, hb, proj, na, w1, w2, w3, res, mod4, gate_chunk, rowf, tn=1024):
    m, d = res.shape
    tm = min(TM, m)
    rows = lambda width, col: pl.BlockSpec((tm, width), lambda i, j: (i, col))
    wcol = lambda w: pl.BlockSpec((w.shape[0], tn), lambda i, j: (0, j))
    return pl.pallas_call(
        _outproj_kernel,
        grid=(m // tm, d // tn),
        in_specs=[rows(MLA_W, 0), rows(ML_WP, 0), rows(ML_WP, 0), rows(ML_WP, C_MLO // ML_WP), rows(NA_W, 0),
                  wcol(w1), wcol(w2), wcol(w3),
                  pl.BlockSpec((tm, tn), lambda i, j: (i, j)), _gate_spec(gate_chunk, rowf, tn)],
        out_specs=pl.BlockSpec((tm, tn), lambda i, j: (i, j)),
        out_shape=jax.ShapeDtypeStruct((m, d), F32),
        scratch_shapes=[pltpu.VMEM((tm, ML_WP), BF16)],
        compiler_params=_cp(("parallel", "arbitrary")),
    )(mla, hf, hb, proj, na, w1, w2, w3, res, mod4)


def _down_kernel(a_ref, w_ref, res_ref, gate_ref, out_ref):
    acc = jnp.dot(a_ref[...], w_ref[...], preferred_element_type=F32)
    out_ref[...] = res_ref[...] + gate_ref[...] * acc


def _down(act, w, res, mod4, gate_chunk, rowf, tn=512):
    m, d = res.shape
    kdim = act.shape[1]
    tm = min(TM, m)
    return pl.pallas_call(
        _down_kernel,
        grid=(m // tm, d // tn),
        in_specs=[pl.BlockSpec((tm, kdim), lambda i, j: (i, 0)),
                  pl.BlockSpec((kdim, tn), lambda i, j: (0, j)),
                  pl.BlockSpec((tm, tn), lambda i, j: (i, j)), _gate_spec(gate_chunk, rowf, tn)],
        out_specs=pl.BlockSpec((tm, tn), lambda i, j: (i, j)),
        out_shape=jax.ShapeDtypeStruct((m, d), F32),
        compiler_params=_cp(("parallel", "arbitrary")),
    )(act, w, res, mod4)


def _final_norm_kernel(x_ref, g_ref, o_ref):
    o_ref[...] = _rms(x_ref[...], g_ref[...])


def _final_norm(x, g):
    m, d = x.shape
    tm = min(TM, m)
    return pl.pallas_call(
        _final_norm_kernel,
        grid=(m // tm,),
        in_specs=[pl.BlockSpec((tm, d), lambda i: (i, 0)), pl.BlockSpec((1, d), lambda i: (0, 0))],
        out_specs=pl.BlockSpec((tm, d), lambda i: (i, 0)),
        out_shape=jax.ShapeDtypeStruct((m, d), F32),
        compiler_params=_cp(("parallel",)),
    )(x, g.reshape(1, d))


def _pad_heads(a, heads, dh, dp, axis):
    shp = a.shape
    a = a.reshape(shp[:axis] + (heads, dh) + shp[axis + 1:])
    pad = [(0, 0)] * a.ndim
    pad[axis + 1] = (0, dp - dh)
    a = jnp.pad(a, pad)
    return a.reshape(shp[:axis] + (heads * dp,) + shp[axis + 1:])


def _rope_cols(a):
    ev, od = a[..., 0::2], a[..., 1::2]
    return jnp.concatenate([ev, od, od, ev], axis=-1)


def _prep_w_in(w):
    qlat, kvlat, krope, mq, mk, mv, mo, gi, gf, nq, nk, nv = jnp.split(w, IN_OFFSETS, axis=-1)
    d = w.shape[0]
    ph = lambda a: _pad_heads(a, ML_HEADS, ML_DH, ML_DP, 1)
    gi = gi.reshape(d, 2, ML_HEADS)
    gf = gf.reshape(d, 2, ML_HEADS)
    g4 = jnp.stack([gi[:, 0], gi[:, 1], gf[:, 0], gf[:, 1]], axis=-1)
    gates = jnp.pad(g4, ((0, 0), (0, 0), (0, LANE - 4))).reshape(d, C_GATES)
    main = jnp.concatenate([ph(mq), ph(mk), ph(mv), ph(mo), qlat, nq, nk, nv, kvlat, _rope_cols(krope)], axis=-1)
    main = jnp.pad(main, ((0, 0), (0, C_MAIN - main.shape[1])))
    return jnp.concatenate([main, gates], axis=-1).astype(BF16)


def _prep_w_uq(w):
    w = w.reshape(w.shape[0], MLA_HEADS, MLA_NOPE + MLA_ROPE)
    out = jnp.concatenate([w[..., :MLA_NOPE], _rope_cols(w[..., MLA_NOPE:])], axis=-1)
    return out.reshape(w.shape[0], MLA_HEADS * MLA_DQ).astype(BF16)


def _gate_bias(i_bias, f_bias):
    g4 = jnp.stack([i_bias[0], i_bias[1], f_bias[0], f_bias[1]], axis=-1).astype(F32)
    col = jnp.pad(g4, ((0, 0), (0, LANE - 4))).reshape(ML_HEADS, 1, LANE)
    row = jnp.pad(g4, ((0, 0), (0, 4))).reshape(ML_HEADS, 8, 1)
    return col, row


def _rope_table(n_tokens):
    t = jnp.arange(n_tokens)
    row = (t // GRID_W).astype(F32)
    col = (t % GRID_W).astype(F32)
    n_freq = MLA_ROPE // 4
    inv = ROPE_BASE ** (-jnp.arange(n_freq, dtype=F32) / n_freq)
    ang = jnp.concatenate([row[:, None] * inv, col[:, None] * inv], axis=-1)
    cos, sin = jnp.cos(ang), jnp.sin(ang)
    return jnp.concatenate([cos, cos, -sin, sin], axis=-1)


def kernel(x, c, ctx, c_ctx, ada_w, ada_b, norm1_g, norm2_g, w_in, mla_q_norm, mla_kv_norm, mla_w_uq, mla_w_uk, mla_w_uv, ml_conv_w, ml_conv_b, ml_i_bias, ml_f_bias, na_rpb, w_out, ffn_w_up, ffn_conv_w, ffn_conv_b, ffn_w_down, final_norm_g):
    batch, seq, d = x.shape
    tctx = ctx.shape[1]
    depth = ada_w.shape[0]
    assert d == D_MODEL and batch <= 7 and tctx == ML_L and tctx % HALO == 0
    assert seq % TM == 0 and seq % NA_BLK == 0 and seq // GRID_W >= NA_KR

    h = x.reshape(batch * seq, d)
    hc = ctx.reshape(batch * tctx, d)
    tm_l = min(TM, seq)
    row_lat = lambda i: i // (seq // tm_l)
    row_ctx = lambda i: batch
    cond8 = jnp.zeros((8, d), F32).at[:batch].set(c).at[batch].set(c_ctx)

    tab_lat = _rope_table(seq)
    tab_ctx = jnp.concatenate([jnp.ones((tctx, 2 * (MLA_ROPE // 2)), F32), jnp.zeros((tctx, MLA_ROPE), F32)], axis=-1)
    ml_scale = jnp.concatenate([jnp.ones((1, ML_WP), F32), jnp.full((1, ML_WP), ML_DH ** -0.5, F32)], axis=-1)

    for l in range(depth):
        last = l == depth - 1
        mod4 = _ada(cond8, ada_w[l], ada_b[l]).reshape(8, N_MOD, 1, d)

        w_in_p = _prep_w_in(w_in[l])
        wuq = _prep_w_uq(mla_w_uq[l])
        wuk = mla_w_uk[l].astype(BF16)
        wuv = mla_w_uv[l].astype(BF16)
        qg = mla_q_norm[l].reshape(1, -1)
        kvg = mla_kv_norm[l].reshape(1, -1)
        cw = jnp.concatenate([_pad_heads(ml_conv_w[l][:, :ML_W], ML_HEADS, ML_DH, ML_DP, 1),
                              _pad_heads(ml_conv_w[l][:, ML_W:], ML_HEADS, ML_DH, ML_DP, 1)], axis=-1)
        cb = jnp.concatenate([_pad_heads(ml_conv_b[l][None, :ML_W], ML_HEADS, ML_DH, ML_DP, 1),
                              _pad_heads(ml_conv_b[l][None, ML_W:], ML_HEADS, ML_DH, ML_DP, 1)], axis=-1)
        bias_col, bias_row = _gate_bias(ml_i_bias[l], ml_f_bias[l])
        na_bias = _na_bias_table(na_rpb[l])
        w1 = w_out[l][:MLA_W].astype(BF16)
        w2 = _pad_heads(w_out[l][MLA_W:MLA_W + ML_W], ML_HEADS, ML_DH, ML_DP, 0).astype(BF16)
        w3 = w_out[l][MLA_W + ML_W:].astype(BF16)
        w_up = ffn_w_up[l].astype(BF16)
        w_down = ffn_w_down[l].astype(BF16)

        def mixers_in(hh, rowf):
            proj, gates = _nm_mm(hh, norm1_g[l], mod4, 1, 0, rowf, w_in_p, TN_IN, tail_cols=C_GATES)
            m = hh.shape[0]
            gates_row = gates.reshape(m, ML_HEADS, LANE)[:, :, :8].transpose(1, 2, 0)
            return proj, gates, gates_row

        proj_c, gates_c, grow_c = mixers_in(hc, row_ctx)
        proj_l, gates_l, grow_l = mixers_in(h, row_lat)

        q_c, k_c, v_c = _mla_up(proj_c, tab_ctx, qg, kvg, wuq, wuk, wuv)
        q_l, k_l, v_l = _mla_up(proj_l, tab_lat, qg, kvg, wuq, wuk, wuv)
        mla_lat = _flash(q_l, 0, k_c, 0, v_c, 0, MLA_HEADS, MLA_DQ, MLA_V, batch, lat=(k_l, 0, v_l, 0))

        qk_c = _conv_silu(proj_c, 2 * ML_WP, cw, cb, ml_scale, tctx)
        qk_l = _conv_silu(proj_l, 2 * ML_WP, cw, cb, ml_scale, seq)
        c0 = jnp.zeros((batch, ML_HEADS, 2, ML_DP, ML_DP), F32)
        m0 = jnp.zeros((batch, ML_HEADS, 2, 8, LANE), F32)
        hf_c, hb_c, c1, m1 = _mlstm(qk_c, proj_c, gates_c, grow_c, bias_col, bias_row, c0, m0, batch)
        hf_l, hb_l, _, _ = _mlstm(qk_l, proj_l, gates_l, grow_l, bias_col, bias_row, c1, m1, batch)

        na_lat = _na(proj_l, proj_c, na_bias, batch)

        h = _outproj(mla_lat, hf_l, hb_l, proj_l, na_lat, w1, w2, w3, h, mod4, 2, row_lat)
        u = _nm_mm(h, norm2_g[l], mod4, 4, 3, row_lat, w_up, 512)
        h = _down(_conv_glu(u, ffn_conv_w[l], ffn_conv_b[l].reshape(1, -1), seq), w_down, h, mod4, 5, row_lat)

        if not last:
            mla_ctx = _flash(q_c, 0, k_c, 0, v_c, 0, MLA_HEADS, MLA_DQ, MLA_V, batch)
            na_ctx = _flash(proj_c, C_NAQ // NA_DH, proj_c, C_NAK // NA_DH, proj_c, C_NAV // NA_DH,
                            NA_HEADS, NA_DH, NA_DH, batch, scale=NA_DH ** -0.5 * LOG2E)
            hc = _outproj(mla_ctx, hf_c, hb_c, proj_c, na_ctx, w1, w2, w3, hc, mod4, 2, row_ctx)
            uc = _nm_mm(hc, norm2_g[l], mod4, 4, 3, row_ctx, w_up, 512)
            hc = _down(_conv_glu(uc, ffn_conv_w[l], ffn_conv_b[l].reshape(1, -1), tctx), w_down, hc, mod4, 5, row_ctx)

    return _final_norm(h, final_norm_g).reshape(batch, seq, d)
```

```python
import functools
import math

import numpy as np
import jax
import jax.numpy as jnp
from jax import lax
from jax.experimental import pallas as pl
from jax.experimental.pallas import tpu as pltpu

F32 = jnp.float32
BF16 = jnp.bfloat16

D_MODEL = 2048
GRID_W = 64
MLA_HEADS = 6
MLA_NOPE = 128
MLA_ROPE = 64
MLA_V = 128
MLA_Q_LORA = 512
MLA_KV_LORA = 256
ML_HEADS = 4
ML_DH = 192
NA_HEADS = 4
NA_DH = 128
NA_KR = 8
NA_KC = 16
ROPE_BASE = 10000.0
EPS = 1e-6
N_MOD = 6
ML_W = ML_HEADS * ML_DH
NA_W = NA_HEADS * NA_DH
MLA_W = MLA_HEADS * MLA_V
IN_SIZES = (MLA_Q_LORA, MLA_KV_LORA, MLA_ROPE, ML_W, ML_W, ML_W, ML_W, 2 * ML_HEADS, 2 * ML_HEADS, NA_W, NA_W, NA_W)
IN_OFFSETS = tuple(sum(IN_SIZES[:i + 1]) for i in range(len(IN_SIZES) - 1))
LOG2E = math.log2(math.e)

LANE = 128
ML_DP = 256
ML_WP = ML_HEADS * ML_DP
ML_L = 256
MLA_DQ = 256
TM = 512
TM_BIG = 1024
HALO = 16
VMEM_LIMIT = 52 * 1024 * 1024

C_MLQ = 0
C_MLK = C_MLQ + ML_WP
C_MLV = C_MLK + ML_WP
C_MLO = C_MLV + ML_WP
C_QLAT = C_MLO + ML_WP
C_NAQ = C_QLAT + MLA_Q_LORA
C_NAK = C_NAQ + NA_W
C_NAV = C_NAK + NA_W
C_KVLAT = C_NAV + NA_W
C_KROPE = C_KVLAT + MLA_KV_LORA
C_MAIN = 6656
C_GATES = ML_HEADS * LANE
TN_IN = 512


def _cp(sem):
    return pltpu.CompilerParams(dimension_semantics=sem, vmem_limit_bytes=VMEM_LIMIT)


def _sigmoid(x):
    return 1.0 / (1.0 + jnp.exp(-x))


def _rms(x, g):
    return x * lax.rsqrt(jnp.mean(x * x, axis=-1, keepdims=True) + EPS) * g


def _mod_spec(chunk, rowf, tm, width=D_MODEL):
    return pl.BlockSpec((None, None, 1, width), lambda i, j: (rowf(i, tm), chunk, 0, j if width != D_MODEL else 0))


def _ada_kernel(c_ref, w_ref, b_ref, o_ref):
    c = c_ref[...]
    a = c * _sigmoid(c)
    o_ref[...] = jnp.dot(a, w_ref[...], preferred_element_type=F32,
                         precision=lax.Precision.HIGHEST) + b_ref[...]


def _ada(cond8, w, b, l):
    _, d, n = w.shape
    tn = 1024
    return pl.pallas_call(
        _ada_kernel,
        grid=(n // tn,),
        in_specs=[pl.BlockSpec((8, d), lambda j: (0, 0)),
                  pl.BlockSpec((None, d, tn), lambda j: (l, 0, j)),
                  pl.BlockSpec((None, 1, tn), lambda j: (l, 0, j))],
        out_specs=pl.BlockSpec((8, tn), lambda j: (0, j)),
        out_shape=jax.ShapeDtypeStruct((8, n), F32),
        compiler_params=_cp(("arbitrary",)),
    )(cond8, w, b.reshape(b.shape[0], 1, n))


def _nm_mm_kernel(x_ref, g_ref, sc_ref, sh_ref, w_ref, o_ref, t_ref, hn_ref):
    j = pl.program_id(1)
    nj = pl.num_programs(1)

    @pl.when(j == 0)
    def _():
        y = _rms(x_ref[...], g_ref[...])
        hn_ref[...] = (y * (1.0 + sc_ref[...]) + sh_ref[...]).astype(BF16)

    acc = jnp.dot(hn_ref[...], w_ref[...], preferred_element_type=F32)

    @pl.when(j < nj - 1)
    def _():
        o_ref[...] = acc.astype(o_ref.dtype)

    @pl.when(j == nj - 1)
    def _():
        t_ref[...] = acc


def _nm_mm(x, g, mod4, sc_chunk, sh_chunk, rowf, w, tn, tail_cols):
    m, d = x.shape
    n = w.shape[1]
    tm = min(TM_BIG, m)
    nj = n // tn
    assert tail_cols == tn
    return pl.pallas_call(
        _nm_mm_kernel,
        grid=(m // tm, nj),
        in_specs=[pl.BlockSpec((tm, d), lambda i, j: (i, 0)),
                  pl.BlockSpec((1, d), lambda i, j: (0, 0)),
                  _mod_spec(sc_chunk, rowf, tm), _mod_spec(sh_chunk, rowf, tm),
                  pl.BlockSpec((d, tn), lambda i, j: (0, j))],
        out_specs=[pl.BlockSpec((tm, tn), lambda i, j: (i, jnp.minimum(j, nj - 2))),
                   pl.BlockSpec((tm, tn), lambda i, j: (i, 0))],
        out_shape=[jax.ShapeDtypeStruct((m, n - tail_cols), BF16), jax.ShapeDtypeStruct((m, tail_cols), F32)],
        scratch_shapes=[pltpu.VMEM((tm, d), BF16)],
        compiler_params=_cp(("parallel", "arbitrary")),
    )(x, g.reshape(1, d), mod4, mod4, w)


def _mla_up_kernel(ql_ref, kvl_ref, kr_ref, tab_ref, qg_ref, kvg_ref, wuq_ref, wuk_ref, wuv_ref,
                   q_ref, k_ref, v_ref, *, scale):
    tab = tab_ref[...]
    qn = _rms(ql_ref[...].astype(F32), qg_ref[...]).astype(BF16)
    qf = jnp.dot(qn, wuq_ref[...], preferred_element_type=F32)
    for h in range(MLA_HEADS):
        c0 = h * MLA_DQ
        y = qf[:, c0 + MLA_NOPE:c0 + MLA_DQ] * tab
        z = y + pltpu.roll(y, MLA_ROPE, axis=1)
        q_ref[:, c0:c0 + MLA_NOPE] = (qf[:, c0:c0 + MLA_NOPE] * scale).astype(BF16)
        q_ref[:, c0 + MLA_NOPE:c0 + MLA_DQ] = (z * scale).astype(BF16)
    cn = _rms(kvl_ref[...].astype(F32), kvg_ref[...]).astype(BF16)
    kn = jnp.dot(cn, wuk_ref[...], preferred_element_type=F32)
    v_ref[...] = jnp.dot(cn, wuv_ref[...], preferred_element_type=F32).astype(BF16)
    ky = kr_ref[...].astype(F32) * tab
    kz = ky + pltpu.roll(ky, MLA_ROPE, axis=1)
    lane = lax.broadcasted_iota(jnp.int32, kz.shape, 1)
    kz = jnp.where(lane < MLA_ROPE, kz, 0.0).astype(BF16)
    for h in range(MLA_HEADS):
        c0 = h * MLA_DQ
        k_ref[:, c0:c0 + MLA_NOPE] = kn[:, h * MLA_NOPE:(h + 1) * MLA_NOPE].astype(BF16)
        k_ref[:, c0 + MLA_NOPE:c0 + MLA_DQ] = kz


def _mla_up(proj, tab, qg, kvg, wuq, wuk, wuv):
    m = proj.shape[0]
    tm = min(TM, tab.shape[0])
    nt = tab.shape[0] // tm
    full = lambda a: pl.BlockSpec(a.shape, lambda i: (0, 0))
    return pl.pallas_call(
        functools.partial(_mla_up_kernel, scale=(MLA_NOPE + MLA_ROPE) ** -0.5 * LOG2E),
        grid=(m // tm,),
        in_specs=[pl.BlockSpec((tm, MLA_Q_LORA), lambda i: (i, C_QLAT // MLA_Q_LORA)),
                  pl.BlockSpec((tm, MLA_KV_LORA), lambda i: (i, C_KVLAT // MLA_KV_LORA)),
                  pl.BlockSpec((tm, LANE), lambda i: (i, C_KROPE // LANE)),
                  pl.BlockSpec((tm, LANE), lambda i: (i % nt, 0)),
                  full(qg), full(kvg), full(wuq), full(wuk), full(wuv)],
        out_specs=[pl.BlockSpec((tm, MLA_HEADS * MLA_DQ), lambda i: (i, 0)),
                   pl.BlockSpec((tm, MLA_HEADS * MLA_DQ), lambda i: (i, 0)),
                   pl.BlockSpec((tm, MLA_W), lambda i: (i, 0))],
        out_shape=[jax.ShapeDtypeStruct((m, MLA_HEADS * MLA_DQ), BF16),
                   jax.ShapeDtypeStruct((m, MLA_HEADS * MLA_DQ), BF16),
                   jax.ShapeDtypeStruct((m, MLA_W), BF16)],
        compiler_params=_cp(("parallel",)),
    )(proj, proj, proj, tab, qg, kvg, wuq, wuk, wuv)


def _flash_kernel(*refs, has_lat, scale, nsub, tk):
    if has_lat:
        q_ref, kc_ref, vc_ref, kl_ref, vl_ref, o_ref, m_scr, acc_scr = refs
    else:
        q_ref, kc_ref, vc_ref, o_ref, m_scr, acc_scr = refs
    ts = q_ref.shape[0] // nsub
    dv = vc_ref.shape[1]
    dn = (((1,), (1,)), ((), ()))

    def scores(q, k):
        s = lax.dot_general(q, k, dn, preferred_element_type=F32)
        return s if scale == 1.0 else s * scale

    def with_ones(v):
        lane = lax.broadcasted_iota(jnp.int32, v.shape, 1)
        return jnp.concatenate([v, jnp.where(lane == 0, 1.0, 0.0).astype(v.dtype)], axis=1)

    def lanes(x, n):
        return x if n == LANE else pltpu.repeat(x, n // LANE, axis=1)

    def probs(s, m):
        return jnp.exp2(s - lanes(m, s.shape[1])).astype(BF16)

    kc = kc_ref[...]
    vc = with_ones(vc_ref[...])
    for u in range(nsub):
        s = scores(q_ref[u * ts:(u + 1) * ts, :], kc)
        m = jnp.broadcast_to(jnp.max(s, axis=-1, keepdims=True), (ts, LANE))
        m_scr[u] = m
        acc_scr[u] = jnp.dot(probs(s, m), vc, preferred_element_type=F32)

    if has_lat:
        def body(kb, carry):
            off = pl.multiple_of(kb * tk, tk)
            k = kl_ref[pl.ds(off, tk), :]
            v = with_ones(vl_ref[pl.ds(off, tk), :])
            s_next = scores(q_ref[0:ts, :], k)
            for u in range(nsub):
                s = s_next
                if u + 1 < nsub:
                    s_next = scores(q_ref[(u + 1) * ts:(u + 2) * ts, :], k)
                m_prev = m_scr[u]
                m_new = jnp.maximum(m_prev, jnp.max(s, axis=-1, keepdims=True))
                m_scr[u] = m_new
                acc_scr[u] = (lanes(jnp.exp2(m_prev - m_new), 2 * dv) * acc_scr[u]
                              + jnp.dot(probs(s, m_new), v, preferred_element_type=F32))
            return carry

        lax.fori_loop(0, kl_ref.shape[0] // tk, body, 0)

    for u in range(nsub):
        acc = acc_scr[u]
        o_ref[u * ts:(u + 1) * ts, :] = (acc[:, :dv] / acc[:, dv:dv + 1]).astype(o_ref.dtype)


def _flash(q, qcol, kc, kccol, vc, vccol, heads, dq, dv, batch, lat=None, scale=1.0, tq=2048, nsub=4, tk=1024):
    mq = q.shape[0]
    sq = mq // batch
    tq = min(tq, sq)
    nsub = min(nsub, tq // 256)
    nq = sq // tq
    tc = kc.shape[0] // batch
    has_lat = lat is not None
    in_specs = [pl.BlockSpec((tq, dq), lambda b, h, i: (b * nq + i, qcol + h)),
                pl.BlockSpec((tc, dq), lambda b, h, i: (b, kccol + h)),
                pl.BlockSpec((tc, dv), lambda b, h, i: (b, vccol + h))]
    args = [q, kc, vc]
    if has_lat:
        kl, klcol, vl, vlcol = lat
        sk = kl.shape[0] // batch
        tk = min(tk, sk)
        in_specs += [pl.BlockSpec((sk, dq), lambda b, h, i: (b, klcol + h)),
                     pl.BlockSpec((sk, dv), lambda b, h, i: (b, vlcol + h))]
        args += [kl, vl]
    return pl.pallas_call(
        functools.partial(_flash_kernel, has_lat=has_lat, scale=scale, nsub=nsub, tk=tk),
        grid=(batch, heads, nq),
        in_specs=in_specs,
        out_specs=pl.BlockSpec((tq, dv), lambda b, h, i: (b * nq + i, h)),
        out_shape=jax.ShapeDtypeStruct((mq, heads * dv), BF16),
        scratch_shapes=[pltpu.VMEM((nsub, tq // nsub, LANE), F32), pltpu.VMEM((nsub, tq // nsub, 2 * dv), F32)],
        compiler_params=_cp(("parallel", "parallel", "arbitrary")),
    )(*args)


NA_RB = 8
NA_BLK = NA_RB * GRID_W
NA_GROUP = 2


def _na_kernel(q_ref, kp_ref, kc_ref, kn_ref, vp_ref, vc_ref, vn_ref, kx_ref, vx_ref, bias_ref, o_ref,
               kw_scr, vw_scr, *, rows, scale):
    j = pl.program_id(1)
    for t, (kr, vr) in enumerate(((kp_ref, vp_ref), (kc_ref, vc_ref), (kn_ref, vn_ref))):
        kw_scr[t * NA_BLK:(t + 1) * NA_BLK, :] = kr[...]
        vw_scr[t * NA_BLK:(t + 1) * NA_BLK, :] = vr[...]
    nwin = NA_KR * GRID_W
    dn = (((1,), (1,)), ((), ()))
    heads = [slice(h * NA_DH, (h + 1) * NA_DH) for h in range(NA_HEADS)]
    for i0 in range(0, NA_RB, NA_GROUP):
        units = []
        for i in range(i0, i0 + NA_GROUP):
            r = NA_RB * j + i
            r0 = jnp.clip(r - NA_KR // 2, 0, rows - NA_KR)
            start = pl.multiple_of((r0 - NA_RB * j + NA_RB) * GRID_W, GRID_W)
            for h, cs in enumerate(heads):
                q = q_ref[i * GRID_W:(i + 1) * GRID_W, cs]
                s_loc = (lax.dot_general(q, kw_scr[pl.ds(start, nwin), cs], dn, preferred_element_type=F32) * scale
                         + bias_ref[h, r - r0])
                s_ctx = lax.dot_general(q, kx_ref[:, cs], dn, preferred_element_type=F32) * scale
                units.append((i, cs, start, s_loc, s_ctx))
        outs = {}
        for i, cs, start, s_loc, s_ctx in units:
            m = jnp.maximum(jnp.max(s_loc, axis=-1, keepdims=True), jnp.max(s_ctx, axis=-1, keepdims=True))
            p_loc = jnp.exp(s_loc - m)
            p_ctx = jnp.exp(s_ctx - m)
            l = jnp.sum(p_loc, axis=-1, keepdims=True) + jnp.sum(p_ctx, axis=-1, keepdims=True)
            o = (jnp.dot(p_loc.astype(BF16), vw_scr[pl.ds(start, nwin), cs], preferred_element_type=F32)
                 + jnp.dot(p_ctx.astype(BF16), vx_ref[:, cs], preferred_element_type=F32))
            outs.setdefault(i, []).append((o / l).astype(o_ref.dtype))
        for i, parts in outs.items():
            o_ref[i * GRID_W:(i + 1) * GRID_W, :] = jnp.concatenate(parts, axis=1)


def _na_bias_table(rpb):
    c = np.arange(GRID_W)
    cs = np.clip(c - NA_KC // 2, 0, GRID_W - NA_KC)
    valid = (c[None, :] >= cs[:, None]) & (c[None, :] < cs[:, None] + NA_KC)
    dc = np.clip(c[None, :] - c[:, None] + (NA_KC - 1), 0, 2 * NA_KC - 2)
    dr = np.arange(NA_KR)[None, :] - np.arange(NA_KR)[:, None] + (NA_KR - 1)
    sel_r = (dr[:, :, None] == np.arange(2 * NA_KR - 1)).astype(np.float32)
    sel_c = ((dc[:, :, None] == np.arange(2 * NA_KC - 1)) & valid[:, :, None]).astype(np.float32)
    tab = jnp.einsum('hab,via,cdb->hvcid', rpb.astype(F32), sel_r, sel_c, precision=lax.Precision.HIGHEST)
    tab = tab + jnp.where(valid[None, None, :, None, :], 0.0, -1e30)
    return tab.reshape(NA_HEADS, NA_KR, GRID_W, NA_KR * GRID_W)


def _na(proj_l, proj_c, bias, batch):
    m = proj_l.shape[0]
    s = m // batch
    rows = s // GRID_W
    nb = rows // NA_RB
    cq, ck, cv = C_NAQ // NA_W, C_NAK // NA_W, C_NAV // NA_W
    tc = proj_c.shape[0] // batch
    prev = lambda b, j: b * nb + jnp.maximum(j - 1, 0)
    cur = lambda b, j: b * nb + j
    nxt = lambda b, j: b * nb + jnp.minimum(j + 1, nb - 1)
    blk = lambda rf, col: pl.BlockSpec((NA_BLK, NA_W), lambda b, j: (rf(b, j), col))
    return pl.pallas_call(
        functools.partial(_na_kernel, rows=rows, scale=NA_DH ** -0.5),
        grid=(batch, nb),
        in_specs=[blk(cur, cq), blk(prev, ck), blk(cur, ck), blk(nxt, ck),
                  blk(prev, cv), blk(cur, cv), blk(nxt, cv),
                  pl.BlockSpec((tc, NA_W), lambda b, j: (b, ck)),
                  pl.BlockSpec((tc, NA_W), lambda b, j: (b, cv)),
                  pl.BlockSpec(bias.shape, lambda b, j: (0, 0, 0, 0))],
        out_specs=pl.BlockSpec((NA_BLK, NA_W), lambda b, j: (b * nb + j, 0)),
        out_shape=jax.ShapeDtypeStruct((m, NA_W), BF16),
        scratch_shapes=[pltpu.VMEM((3 * NA_BLK, NA_W), BF16), pltpu.VMEM((3 * NA_BLK, NA_W), BF16)],
        compiler_params=_cp(("parallel", "arbitrary")),
    )(proj_l, proj_l, proj_l, proj_l, proj_l, proj_l, proj_l, proj_c, proj_c, bias)


def _conv_silu_kernel(x_ref, p_ref, n_ref, w_ref, b_ref, sc_ref, o_ref, *, tiles_per_seg):
    t = pl.program_id(0) % tiles_per_seg
    prev_row = jnp.where(t == 0, 0.0, p_ref[HALO - 1:HALO, :].astype(F32))
    next_row = jnp.where(t == tiles_per_seg - 1, 0.0, n_ref[0:1, :].astype(F32))
    x = x_ref[...].astype(F32)
    tm = x.shape[0]
    rid = lax.broadcasted_iota(jnp.int32, x.shape, 0)
    xp = jnp.where(rid == 0, prev_row, pltpu.roll(x, 1, axis=0))
    xn = jnp.where(rid == tm - 1, next_row, pltpu.roll(x, tm - 1, axis=0))
    w = w_ref[...]
    y = xp * w[0:1, :] + x * w[1:2, :] + xn * w[2:3, :] + b_ref[...]
    o_ref[...] = (y * _sigmoid(y) * sc_ref[...]).astype(o_ref.dtype)


def _conv_silu(x, ncols, w, b, colscale, seg, tc=512):
    m = x.shape[0]
    tm = min(TM, seg)
    hb = tm // HALO
    last = m // HALO - 1
    vec = lambda r: pl.BlockSpec((r, tc), lambda i, j: (0, j))
    return pl.pallas_call(
        functools.partial(_conv_silu_kernel, tiles_per_seg=seg // tm),
        grid=(m // tm, ncols // tc),
        in_specs=[pl.BlockSpec((tm, tc), lambda i, j: (i, j)),
                  pl.BlockSpec((HALO, tc), lambda i, j: (jnp.maximum(i * hb - 1, 0), j)),
                  pl.BlockSpec((HALO, tc), lambda i, j: (jnp.minimum((i + 1) * hb, last), j)),
                  vec(3), vec(1), vec(1)],
        out_specs=pl.BlockSpec((tm, tc), lambda i, j: (i, j)),
        out_shape=jax.ShapeDtypeStruct((m, ncols), BF16),
        compiler_params=_cp(("parallel", "parallel")),
    )(x, x, x, w, b, colscale)


def _split3(x):
    x1 = x.astype(BF16)
    r = x - x1.astype(F32)
    x2 = r.astype(BF16)
    x3 = (r - x2.astype(F32)).astype(BF16)
    return x1, x2, x3


def _log_sigmoid(x):
    return jnp.minimum(x, 0.0) - jnp.log(1.0 + jnp.exp(-jnp.abs(x)))


def _mlstm_dir(d, q_ref, k_ref, v_ref, gc_ref, gr_ref, bc_ref, br_ref, h_ref, c_scr, m_scr):
    L = ML_L
    ri = lax.broadcasted_iota(jnp.int32, (L, L), 0)
    ci = lax.broadcasted_iota(jnp.int32, (L, L), 1)
    mask = (ci <= ri) if d == 0 else (ci >= ri)
    tri = jnp.where(mask, 1.0, 0.0).astype(BF16)
    tri_t = jnp.where((ri <= ci) if d == 0 else (ri >= ci), 1.0, 0.0).astype(BF16)
    lane = lax.broadcasted_iota(jnp.int32, (L, ML_DP), 1)

    gc = gc_ref[...] + bc_ref[...]
    gr = gr_ref[...] + br_ref[...]
    bc_all = sum(jnp.dot(tri, p, preferred_element_type=F32) for p in _split3(_log_sigmoid(gc)))
    br_all = sum(jnp.dot(p, tri_t, preferred_element_type=F32) for p in _split3(_log_sigmoid(gr)))

    hs = range(ML_HEADS)
    cols = [slice(h * ML_DP, (h + 1) * ML_DP) for h in hs]
    q = [q_ref[:, cs] for cs in cols]
    k = [k_ref[:, cs] for cs in cols]
    v = [jnp.where(lane == ML_DH, 1.0, v_ref[:, cs].astype(F32)).astype(BF16) for cs in cols]
    m_prev = [m_scr[h, d][0:1, 0:1] for h in hs]
    ct = [c_scr[h, d] for h in hs]
    qk = [lax.dot_general(q[h], k[h], (((1,), (1,)), ((), ())), preferred_element_type=F32) for h in hs]
    inter = [jnp.dot(q[h], ct[h].astype(BF16), preferred_element_type=F32) for h in hs]

    b_col = [bc_all[:, h * LANE + 2 + d:h * LANE + 3 + d] for h in hs]
    li_col = [gc[:, h * LANE + d:h * LANE + d + 1] for h in hs]
    a, w_st, m_t = [], [], []
    for h in hs:
        b_row = br_all[h * 8 + 2 + d:h * 8 + 3 + d, :]
        li_row = gr[h * 8 + d:h * 8 + d + 1, :]
        dmat = jnp.where(mask, b_col[h] - b_row + li_row, -jnp.inf)
        g = b_col[h] + m_prev[h]
        m_t.append(jnp.maximum(g, jnp.max(dmat, axis=-1, keepdims=True)))
        w_st.append(jnp.exp(g - m_t[h]))
        a.append((qk[h] * jnp.exp(dmat - m_t[h])).astype(BF16))

    for h in hs:
        num = w_st[h] * inter[h] + jnp.dot(a[h], v[h], preferred_element_type=F32)
        den = num[:, ML_DH:ML_DH + 1]
        hv = num / jnp.maximum(jnp.abs(den), jnp.exp(-m_t[h]))
        h_ref[:, cols[h]] = jnp.where(lane < ML_DH, hv, 0.0).astype(h_ref.dtype)

    for h in hs:
        b_end = b_col[h][L - 1:L, :] if d == 0 else b_col[h][0:1, :]
        e_log = b_end - b_col[h] + li_col[h]
        m_new = jnp.maximum(b_end + m_prev[h], jnp.max(e_log, axis=0, keepdims=True))
        decay = jnp.exp(b_end + m_prev[h] - m_new)
        ek = (jnp.exp(e_log - m_new) * k[h].astype(F32)).astype(BF16)
        c_scr[h, d] = decay * ct[h] + lax.dot_general(ek, v[h], (((0,), (0,)), ((), ())),
                                                      preferred_element_type=F32)
        m_scr[h, d] = jnp.broadcast_to(m_new, m_scr.shape[2:])


def _mlstm_kernel(qf_ref, kf_ref, vf_ref, gcf_ref, grf_ref, qb_ref, kb_ref, vb_ref, gcb_ref, grb_ref,
                  bc_ref, br_ref, c0_ref, m0_ref, hf_ref, hb_ref, c1_ref, m1_ref, c_scr, m_scr):
    c = pl.program_id(1)

    @pl.when(c == 0)
    def _():
        c_scr[...] = c0_ref[...]
        m_scr[...] = m0_ref[...]

    _mlstm_dir(0, qf_ref, kf_ref, vf_ref, gcf_ref, grf_ref, bc_ref, br_ref, hf_ref, c_scr, m_scr)
    _mlstm_dir(1, qb_ref, kb_ref, vb_ref, gcb_ref, grb_ref, bc_ref, br_ref, hb_ref, c_scr, m_scr)

    @pl.when(c == pl.num_programs(1) - 1)
    def _():
        c1_ref[...] = c_scr[...]
        m1_ref[...] = m_scr[...]


def _mlstm(qk, proj, gates, gates_row, bias_col, bias_row, c0, m0, batch):
    m = qk.shape[0]
    s = m // batch
    nc = s // ML_L
    fwd = lambda b, c: b * nc + c
    bwd = lambda b, c: b * nc + (nc - 1 - c)

    def dir_specs(rf):
        return [pl.BlockSpec((ML_L, ML_WP), lambda b, c: (rf(b, c), C_MLQ // ML_WP)),
                pl.BlockSpec((ML_L, ML_WP), lambda b, c: (rf(b, c), C_MLK // ML_WP)),
                pl.BlockSpec((ML_L, ML_WP), lambda b, c: (rf(b, c), C_MLV // ML_WP)),
                pl.BlockSpec((ML_L, C_GATES), lambda b, c: (rf(b, c), 0)),
                pl.BlockSpec((ML_HEADS * 8, ML_L), lambda b, c: (0, rf(b, c)))]

    st_c = pl.BlockSpec((None, ML_HEADS, 2, ML_DP, ML_DP), lambda b, c: (b, 0, 0, 0, 0))
    st_m = pl.BlockSpec((None, ML_HEADS, 2, 8, LANE), lambda b, c: (b, 0, 0, 0, 0))
    return pl.pallas_call(
        _mlstm_kernel,
        grid=(batch, nc),
        in_specs=dir_specs(fwd) + dir_specs(bwd)
        + [pl.BlockSpec((1, C_GATES), lambda b, c: (0, 0)),
           pl.BlockSpec((ML_HEADS * 8, 1), lambda b, c: (0, 0)), st_c, st_m],
        out_specs=[pl.BlockSpec((ML_L, ML_WP), lambda b, c: (fwd(b, c), 0)),
                   pl.BlockSpec((ML_L, ML_WP), lambda b, c: (bwd(b, c), 0)), st_c, st_m],
        out_shape=[jax.ShapeDtypeStruct((m, ML_WP), BF16), jax.ShapeDtypeStruct((m, ML_WP), BF16),
                   jax.ShapeDtypeStruct(c0.shape, F32), jax.ShapeDtypeStruct(m0.shape, F32)],
        scratch_shapes=[pltpu.VMEM((ML_HEADS, 2, ML_DP, ML_DP), F32), pltpu.VMEM((ML_HEADS, 2, 8, LANE), F32)],
        compiler_params=_cp(("parallel", "arbitrary")),
    )(qk, qk, proj, gates, gates_row, qk, qk, proj, gates, gates_row, bias_col, bias_row, c0, m0)


def _outproj_kernel(mla_ref, hf_ref, hb_ref, og_ref, na_ref, w1_ref, w2_ref, w3_ref, res_ref, gate_ref,
                    out_ref, ml_scr):
    @pl.when(pl.program_id(1) == 0)
    def _():
        hsum = hf_ref[...].astype(F32) + hb_ref[...].astype(F32)
        ml_scr[...] = (hsum * _sigmoid(og_ref[...].astype(F32))).astype(BF16)

    acc = (jnp.dot(mla_ref[...], w1_ref[...], preferred_element_type=F32)
           + jnp.dot(ml_scr[...], w2_ref[...], preferred_element_type=F32)
           + jnp.dot(na_ref[...], w3_ref[...], preferred_element_type=F32))
    out_ref[...] = res_ref[...] + gate_ref[...] * acc


def _outproj(mla, hf, hb, proj, na, w1, w2, w3, res, mod4, gate_chunk, rowf, tn=1024):
    m, d = res.shape
    tm = min(TM, m)
    rows = lambda width, col: pl.BlockSpec((tm, width), lambda i, j: (i, col))
    wcol = lambda w: pl.BlockSpec((w.shape[0], tn), lambda i, j: (0, j))
    return pl.pallas_call(
        _outproj_kernel,
        grid=(m // tm, d // tn),
        in_specs=[rows(MLA_W, 0), rows(ML_WP, 0), rows(ML_WP, 0), rows(ML_WP, C_MLO // ML_WP), rows(NA_W, 0),
                  wcol(w1), wcol(w2), wcol(w3),
                  pl.BlockSpec((tm, tn), lambda i, j: (i, j)), _mod_spec(gate_chunk, rowf, tm, tn)],
        out_specs=pl.BlockSpec((tm, tn), lambda i, j: (i, j)),
        out_shape=jax.ShapeDtypeStruct((m, d), F32),
        scratch_shapes=[pltpu.VMEM((tm, ML_WP), BF16)],
        compiler_params=_cp(("parallel", "arbitrary")),
    )(mla, hf, hb, proj, na, w1, w2, w3, res, mod4)


def _ffn_kernel(x_ref, xp_ref, xn_ref, g_ref, sc_ref, sh_ref, wg_ref, wv_ref, cwg_ref, cwv_ref, cbg_ref, cbv_ref,
                wd_ref, gate_ref, *rest, tiles_per_seg, final):
    if final:
        fg_ref, o_ref, hn_scr = rest
    else:
        o_ref, hn_scr = rest
    i = pl.program_id(0)
    j = pl.program_id(1)
    tm = x_ref.shape[0]
    te = hn_scr.shape[0]

    @pl.when(j == 0)
    def _():
        nm = lambda x: (_rms(x, g_ref[...]) * (1.0 + sc_ref[...]) + sh_ref[...]).astype(BF16)
        hn_scr[0:HALO, :] = nm(xp_ref[...])
        hn_scr[HALO:HALO + tm, :] = nm(x_ref[...])
        hn_scr[HALO + tm:te, :] = nm(xn_ref[...])
        o_ref[...] = jnp.zeros_like(o_ref)

    t = i % tiles_per_seg
    keep_prev = jnp.where(t == 0, 0.0, 1.0)
    keep_next = jnp.where(t == tiles_per_seg - 1, 0.0, 1.0)
    hn = hn_scr[...]

    def conv(w_ref, cw_ref, cb_ref):
        u = jnp.dot(hn, w_ref[...], preferred_element_type=F32)
        rid = lax.broadcasted_iota(jnp.int32, (tm, u.shape[1]), 0)
        up = pltpu.roll(u, 1, axis=0)[HALO:HALO + tm]
        un = pltpu.roll(u, te - 1, axis=0)[HALO:HALO + tm]
        up = jnp.where(rid == 0, up * keep_prev, up)
        un = jnp.where(rid == tm - 1, un * keep_next, un)
        cw = cw_ref[...]
        return up * cw[0:1, :] + u[HALO:HALO + tm] * cw[1:2, :] + un * cw[2:3, :] + cb_ref[...]

    gg = conv(wg_ref, cwg_ref, cbg_ref)
    vv = conv(wv_ref, cwv_ref, cbv_ref)
    act = (gg * _sigmoid(gg) * vv).astype(BF16)
    o_ref[...] += jnp.dot(act, wd_ref[...], preferred_element_type=F32)

    @pl.when(j == pl.num_programs(1) - 1)
    def _():
        y = x_ref[...] + gate_ref[...] * o_ref[...]
        o_ref[...] = _rms(y, fg_ref[...]) if final else y


def _ffn(x, g, mod4, rowf, w_up, cw, cb, w_down, seg, final_g=None, tf=512):
    m, d = x.shape
    dff = w_down.shape[0]
    tm = min(TM_BIG, seg)
    nj = dff // tf
    hb = tm // HALO
    last = m // HALO - 1
    final = final_g is not None
    vec = lambda r, off: pl.BlockSpec((r, tf), lambda i, j: (0, off + j))
    once = pl.Buffered(1)
    in_specs = [pl.BlockSpec((tm, d), lambda i, j: (i, 0), pipeline_mode=once),
                pl.BlockSpec((HALO, d), lambda i, j: (jnp.maximum(i * hb - 1, 0), 0)),
                pl.BlockSpec((HALO, d), lambda i, j: (jnp.minimum((i + 1) * hb, last), 0)),
                pl.BlockSpec((1, d), lambda i, j: (0, 0)),
                _mod_spec(4, rowf, tm), _mod_spec(3, rowf, tm),
                pl.BlockSpec((d, tf), lambda i, j: (0, j)), pl.BlockSpec((d, tf), lambda i, j: (0, nj + j)),
                vec(3, 0), vec(3, nj), vec(1, 0), vec(1, nj),
                pl.BlockSpec((tf, d), lambda i, j: (j, 0)), _mod_spec(5, rowf, tm)]
    args = [x, x, x, g.reshape(1, d), mod4, mod4, w_up, w_up, cw, cw, cb, cb, w_down, mod4]
    if final:
        in_specs.append(pl.BlockSpec((1, d), lambda i, j: (0, 0)))
        args.append(final_g.reshape(1, d))
    return pl.pallas_call(
        functools.partial(_ffn_kernel, tiles_per_seg=seg // tm, final=final),
        grid=(m // tm, nj),
        in_specs=in_specs,
        out_specs=pl.BlockSpec((tm, d), lambda i, j: (i, 0), pipeline_mode=once),
        out_shape=jax.ShapeDtypeStruct((m, d), F32),
        scratch_shapes=[pltpu.VMEM((tm + 2 * HALO, d), BF16)],
        compiler_params=_cp(("parallel", "arbitrary")),
    )(*args)


def _pad_heads(a, heads, dh, dp, axis):
    shp = a.shape
    a = a.reshape(shp[:axis] + (heads, dh) + shp[axis + 1:])
    pad = [(0, 0)] * a.ndim
    pad[axis + 1] = (0, dp - dh)
    a = jnp.pad(a, pad)
    return a.reshape(shp[:axis] + (heads * dp,) + shp[axis + 1:])


def _rope_cols(a):
    ev, od = a[..., 0::2], a[..., 1::2]
    return jnp.concatenate([ev, od, od, ev], axis=-1)


def _prep_w_in(w):
    w = w.astype(BF16)
    qlat, kvlat, krope, mq, mk, mv, mo, gi, gf, nq, nk, nv = jnp.split(w, IN_OFFSETS, axis=-1)
    d = w.shape[0]
    ph = lambda a: _pad_heads(a, ML_HEADS, ML_DH, ML_DP, 1)
    gi = gi.reshape(d, 2, ML_HEADS)
    gf = gf.reshape(d, 2, ML_HEADS)
    g4 = jnp.stack([gi[:, 0], gi[:, 1], gf[:, 0], gf[:, 1]], axis=-1)
    gates = jnp.pad(g4, ((0, 0), (0, 0), (0, LANE - 4))).reshape(d, C_GATES)
    main = jnp.concatenate([ph(mq), ph(mk), ph(mv), ph(mo), qlat, nq, nk, nv, kvlat, _rope_cols(krope)], axis=-1)
    main = jnp.pad(main, ((0, 0), (0, C_MAIN - main.shape[1])))
    return jnp.concatenate([main, gates], axis=-1)


def _prep_w_uq(w):
    w = w.reshape(w.shape[0], MLA_HEADS, MLA_NOPE + MLA_ROPE)
    out = jnp.concatenate([w[..., :MLA_NOPE], _rope_cols(w[..., MLA_NOPE:])], axis=-1)
    return out.reshape(w.shape[0], MLA_HEADS * MLA_DQ).astype(BF16)


def _gate_bias(i_bias, f_bias):
    g4 = jnp.stack([i_bias[0], i_bias[1], f_bias[0], f_bias[1]], axis=-1).astype(F32)
    col = jnp.pad(g4, ((0, 0), (0, LANE - 4))).reshape(1, C_GATES)
    row = jnp.pad(g4, ((0, 0), (0, 4))).reshape(ML_HEADS * 8, 1)
    return col, row


def _rope_table(n_tokens):
    t = jnp.arange(n_tokens)
    row = (t // GRID_W).astype(F32)
    col = (t % GRID_W).astype(F32)
    n_freq = MLA_ROPE // 4
    inv = ROPE_BASE ** (-jnp.arange(n_freq, dtype=F32) / n_freq)
    ang = jnp.concatenate([row[:, None] * inv, col[:, None] * inv], axis=-1)
    cos, sin = jnp.cos(ang), jnp.sin(ang)
    return jnp.concatenate([cos, cos, -sin, sin], axis=-1)


def kernel(x, c, ctx, c_ctx, ada_w, ada_b, norm1_g, norm2_g, w_in, mla_q_norm, mla_kv_norm, mla_w_uq, mla_w_uk, mla_w_uv, ml_conv_w, ml_conv_b, ml_i_bias, ml_f_bias, na_rpb, w_out, ffn_w_up, ffn_conv_w, ffn_conv_b, ffn_w_down, final_norm_g):
    batch, seq, d = x.shape
    tctx = ctx.shape[1]
    depth = ada_w.shape[0]
    assert d == D_MODEL and batch <= 7 and tctx == ML_L and tctx % HALO == 0
    assert seq % TM_BIG == 0 and seq % NA_BLK == 0 and seq // GRID_W >= NA_KR

    h = x.reshape(batch * seq, d)
    hc = ctx.reshape(batch * tctx, d)
    row_lat = lambda i, tm: i // (seq // tm)
    row_ctx = lambda i, tm: batch
    cond8 = jnp.zeros((8, d), F32).at[:batch].set(c).at[batch].set(c_ctx)

    tab_lat = _rope_table(seq)
    tab_ctx = jnp.concatenate([jnp.ones((tctx, MLA_ROPE), F32), jnp.zeros((tctx, MLA_ROPE), F32)], axis=-1)
    ml_scale = jnp.concatenate([jnp.ones((1, ML_WP), F32), jnp.full((1, ML_WP), ML_DH ** -0.5, F32)], axis=-1)

    for l in range(depth):
        last = l == depth - 1
        mod4 = _ada(cond8, ada_w, ada_b, l).reshape(8, N_MOD, 1, d)

        w_in_p = _prep_w_in(w_in[l])
        wuq = _prep_w_uq(mla_w_uq[l])
        wuk = mla_w_uk[l].astype(BF16)
        wuv = mla_w_uv[l].astype(BF16)
        qg = mla_q_norm[l].reshape(1, -1)
        kvg = mla_kv_norm[l].reshape(1, -1)
        cw = jnp.concatenate([_pad_heads(ml_conv_w[l][:, :ML_W], ML_HEADS, ML_DH, ML_DP, 1),
                              _pad_heads(ml_conv_w[l][:, ML_W:], ML_HEADS, ML_DH, ML_DP, 1)], axis=-1)
        cb = jnp.concatenate([_pad_heads(ml_conv_b[l][None, :ML_W], ML_HEADS, ML_DH, ML_DP, 1),
                              _pad_heads(ml_conv_b[l][None, ML_W:], ML_HEADS, ML_DH, ML_DP, 1)], axis=-1)
        bias_col, bias_row = _gate_bias(ml_i_bias[l], ml_f_bias[l])
        na_bias = _na_bias_table(na_rpb[l])
        w1 = w_out[l][:MLA_W].astype(BF16)
        w2 = _pad_heads(w_out[l][MLA_W:MLA_W + ML_W].astype(BF16), ML_HEADS, ML_DH, ML_DP, 0)
        w3 = w_out[l][MLA_W + ML_W:].astype(BF16)
        w_up = ffn_w_up[l].astype(BF16)
        w_down = ffn_w_down[l].astype(BF16)
        fcw = ffn_conv_w[l]
        fcb = ffn_conv_b[l].reshape(1, -1)

        def mixers_in(hh, rowf):
            proj, gates = _nm_mm(hh, norm1_g[l], mod4, 1, 0, rowf, w_in_p, TN_IN, C_GATES)
            m = hh.shape[0]
            gates_row = gates.reshape(m, ML_HEADS, LANE)[:, :, :8].transpose(1, 2, 0).reshape(ML_HEADS * 8, m)
            return proj, gates, gates_row

        proj_c, gates_c, grow_c = mixers_in(hc, row_ctx)
        proj_l, gates_l, grow_l = mixers_in(h, row_lat)

        q_c, k_c, v_c = _mla_up(proj_c, tab_ctx, qg, kvg, wuq, wuk, wuv)
        q_l, k_l, v_l = _mla_up(proj_l, tab_lat, qg, kvg, wuq, wuk, wuv)
        mla_lat = _flash(q_l, 0, k_c, 0, v_c, 0, MLA_HEADS, MLA_DQ, MLA_V, batch, lat=(k_l, 0, v_l, 0))

        qk_c = _conv_silu(proj_c, 2 * ML_WP, cw, cb, ml_scale, tctx)
        qk_l = _conv_silu(proj_l, 2 * ML_WP, cw, cb, ml_scale, seq)
        c0 = jnp.zeros((batch, ML_HEADS, 2, ML_DP, ML_DP), F32)
        m0 = jnp.zeros((batch, ML_HEADS, 2, 8, LANE), F32)
        hf_c, hb_c, c1, m1 = _mlstm(qk_c, proj_c, gates_c, grow_c, bias_col, bias_row, c0, m0, batch)
        hf_l, hb_l, _, _ = _mlstm(qk_l, proj_l, gates_l, grow_l, bias_col, bias_row, c1, m1, batch)

        na_lat = _na(proj_l, proj_c, na_bias, batch)

        h = _outproj(mla_lat, hf_l, hb_l, proj_l, na_lat, w1, w2, w3, h, mod4, 2, row_lat)
        h = _ffn(h, norm2_g[l], mod4, row_lat, w_up, fcw, fcb, w_down, seq,
                 final_g=final_norm_g if last else None)

        if not last:
            mla_ctx = _flash(q_c, 0, k_c, 0, v_c, 0, MLA_HEADS, MLA_DQ, MLA_V, batch)
            na_ctx = _flash(proj_c, C_NAQ // NA_DH, proj_c, C_NAK // NA_DH, proj_c, C_NAV // NA_DH,
                            NA_HEADS, NA_DH, NA_DH, batch, scale=NA_DH ** -0.5 * LOG2E)
            hc = _outproj(mla_ctx, hf_c, hb_c, proj_c, na_ctx, w1, w2, w3, hc, mod4, 2, row_ctx)
            hc = _ffn(hc, norm2_g[l], mod4, row_ctx, w_up, fcw, fcb, w_down, tctx)

    return h.reshape(batch, seq, d)
```

```python
import functools
import math

import numpy as np
import jax
import jax.numpy as jnp
from jax import lax
from jax.experimental import pallas as pl
from jax.experimental.pallas import tpu as pltpu

F32 = jnp.float32
BF16 = jnp.bfloat16

D_MODEL = 2048
GRID_W = 64
MLA_HEADS = 6
MLA_NOPE = 128
MLA_ROPE = 64
MLA_V = 128
MLA_Q_LORA = 512
MLA_KV_LORA = 256
ML_HEADS = 4
ML_DH = 192
NA_HEADS = 4
NA_DH = 128
NA_KR = 8
NA_KC = 16
ROPE_BASE = 10000.0
EPS = 1e-6
N_MOD = 6
ML_W = ML_HEADS * ML_DH
NA_W = NA_HEADS * NA_DH
MLA_W = MLA_HEADS * MLA_V
IN_SIZES = (MLA_Q_LORA, MLA_KV_LORA, MLA_ROPE, ML_W, ML_W, ML_W, ML_W, 2 * ML_HEADS, 2 * ML_HEADS, NA_W, NA_W, NA_W)
IN_OFFSETS = tuple(sum(IN_SIZES[:i + 1]) for i in range(len(IN_SIZES) - 1))
LOG2E = math.log2(math.e)

LANE = 128
ML_DP = 256
ML_WP = ML_HEADS * ML_DP
ML_L = 256
MLA_DQ = 256
TM = 512
TM_BIG = 1024
HALO = 16
VMEM_LIMIT = 52 * 1024 * 1024

C_MLQ = 0
C_MLK = C_MLQ + ML_WP
C_MLV = C_MLK + ML_WP
C_MLO = C_MLV + ML_WP
C_QLAT = C_MLO + ML_WP
C_NAQ = C_QLAT + MLA_Q_LORA
C_NAK = C_NAQ + NA_W
C_NAV = C_NAK + NA_W
C_KVLAT = C_NAV + NA_W
C_KROPE = C_KVLAT + MLA_KV_LORA
C_GATE0 = C_KROPE + LANE
C_GATES = LANE
C_IN = C_GATE0 + C_GATES
TN_IN = 512


def _cp(sem):
    return pltpu.CompilerParams(dimension_semantics=sem, vmem_limit_bytes=VMEM_LIMIT)


def _sigmoid(x):
    return 1.0 / (1.0 + jnp.exp(-x))


def _rms(x, g):
    return x * lax.rsqrt(jnp.mean(x * x, axis=-1, keepdims=True) + EPS) * g


def _mod_spec(chunk, rowf, tm, width=D_MODEL):
    return pl.BlockSpec((None, None, 1, width), lambda i, j: (rowf(i, tm), chunk, 0, j if width != D_MODEL else 0))


def _ada_kernel(c_ref, w_ref, b_ref, o_ref):
    c = c_ref[...]
    a = c * _sigmoid(c)
    o_ref[...] = jnp.dot(a, w_ref[...], preferred_element_type=F32,
                         precision=lax.Precision.HIGHEST) + b_ref[...]


def _ada(cond8, w, b, l):
    _, d, n = w.shape
    tn = 1024
    return pl.pallas_call(
        _ada_kernel,
        grid=(n // tn,),
        in_specs=[pl.BlockSpec((8, d), lambda j: (0, 0)),
                  pl.BlockSpec((None, d, tn), lambda j: (l, 0, j)),
                  pl.BlockSpec((None, 1, tn), lambda j: (l, 0, j))],
        out_specs=pl.BlockSpec((8, tn), lambda j: (0, j)),
        out_shape=jax.ShapeDtypeStruct((8, n), F32),
        compiler_params=_cp(("arbitrary",)),
    )(cond8, w, b.reshape(b.shape[0], 1, n))


def _nm_mm_kernel(x_ref, g_ref, sc_ref, sh_ref, w_ref, o_ref, t_ref, hn_ref):
    j = pl.program_id(1)
    nj = pl.num_programs(1)

    @pl.when(j == 0)
    def _():
        y = _rms(x_ref[...], g_ref[...])
        hn_ref[...] = (y * (1.0 + sc_ref[...]) + sh_ref[...]).astype(BF16)

    acc = jnp.dot(hn_ref[...], w_ref[...], preferred_element_type=F32)
    o_ref[...] = acc.astype(o_ref.dtype)

    @pl.when(j == nj - 1)
    def _():
        t_ref[...] = acc[:, acc.shape[1] - t_ref.shape[1]:]


def _nm_mm(x, g, mod4, sc_chunk, sh_chunk, rowf, w, l, tn, tail_cols):
    m, d = x.shape
    n = w.shape[2]
    tm = min(TM_BIG, m)
    return pl.pallas_call(
        _nm_mm_kernel,
        grid=(m // tm, n // tn),
        in_specs=[pl.BlockSpec((tm, d), lambda i, j: (i, 0)),
                  pl.BlockSpec((1, d), lambda i, j: (0, 0)),
                  _mod_spec(sc_chunk, rowf, tm), _mod_spec(sh_chunk, rowf, tm),
                  pl.BlockSpec((None, d, tn), lambda i, j: (l, 0, j))],
        out_specs=[pl.BlockSpec((tm, tn), lambda i, j: (i, j)),
                   pl.BlockSpec((tm, tail_cols), lambda i, j: (i, 0))],
        out_shape=[jax.ShapeDtypeStruct((m, n), BF16), jax.ShapeDtypeStruct((m, tail_cols), F32)],
        scratch_shapes=[pltpu.VMEM((tm, d), BF16)],
        compiler_params=_cp(("parallel", "arbitrary")),
    )(x, g.reshape(1, d), mod4, mod4, w)


def _mla_up_kernel(ql_ref, kvl_ref, kr_ref, tab_ref, qg_ref, kvg_ref, wuq_ref, wuk_ref, wuv_ref,
                   q_ref, k_ref, v_ref, *, scale):
    tab = tab_ref[...]
    qn = _rms(ql_ref[...].astype(F32), qg_ref[...]).astype(BF16)
    qf = jnp.dot(qn, wuq_ref[...], preferred_element_type=F32)
    for h in range(MLA_HEADS):
        c0 = h * MLA_DQ
        y = qf[:, c0 + MLA_NOPE:c0 + MLA_DQ] * tab
        z = y + pltpu.roll(y, MLA_ROPE, axis=1)
        q_ref[:, c0:c0 + MLA_NOPE] = (qf[:, c0:c0 + MLA_NOPE] * scale).astype(BF16)
        q_ref[:, c0 + MLA_NOPE:c0 + MLA_DQ] = (z * scale).astype(BF16)
    cn = _rms(kvl_ref[...].astype(F32), kvg_ref[...]).astype(BF16)
    kn = jnp.dot(cn, wuk_ref[...], preferred_element_type=F32)
    v_ref[...] = jnp.dot(cn, wuv_ref[...], preferred_element_type=F32).astype(BF16)
    ky = kr_ref[...].astype(F32) * tab
    kz = ky + pltpu.roll(ky, MLA_ROPE, axis=1)
    lane = lax.broadcasted_iota(jnp.int32, kz.shape, 1)
    kz = jnp.where(lane < MLA_ROPE, kz, 0.0).astype(BF16)
    for h in range(MLA_HEADS):
        c0 = h * MLA_DQ
        k_ref[:, c0:c0 + MLA_NOPE] = kn[:, h * MLA_NOPE:(h + 1) * MLA_NOPE].astype(BF16)
        k_ref[:, c0 + MLA_NOPE:c0 + MLA_DQ] = kz


def _mla_up(proj, tab, qg, kvg, wuq, wuk, wuv):
    m = proj.shape[0]
    tm = min(TM, tab.shape[0])
    nt = tab.shape[0] // tm
    full = lambda a: pl.BlockSpec(a.shape, lambda i: (0, 0))
    return pl.pallas_call(
        functools.partial(_mla_up_kernel, scale=(MLA_NOPE + MLA_ROPE) ** -0.5 * LOG2E),
        grid=(m // tm,),
        in_specs=[pl.BlockSpec((tm, MLA_Q_LORA), lambda i: (i, C_QLAT // MLA_Q_LORA)),
                  pl.BlockSpec((tm, MLA_KV_LORA), lambda i: (i, C_KVLAT // MLA_KV_LORA)),
                  pl.BlockSpec((tm, LANE), lambda i: (i, C_KROPE // LANE)),
                  pl.BlockSpec((tm, LANE), lambda i: (i % nt, 0)),
                  full(qg), full(kvg), full(wuq), full(wuk), full(wuv)],
        out_specs=[pl.BlockSpec((tm, MLA_HEADS * MLA_DQ), lambda i: (i, 0)),
                   pl.BlockSpec((tm, MLA_HEADS * MLA_DQ), lambda i: (i, 0)),
                   pl.BlockSpec((tm, MLA_W), lambda i: (i, 0))],
        out_shape=[jax.ShapeDtypeStruct((m, MLA_HEADS * MLA_DQ), BF16),
                   jax.ShapeDtypeStruct((m, MLA_HEADS * MLA_DQ), BF16),
                   jax.ShapeDtypeStruct((m, MLA_W), BF16)],
        compiler_params=_cp(("parallel",)),
    )(proj, proj, proj, tab, qg, kvg, wuq, wuk, wuv)


def _flash_kernel(*refs, has_lat, scale, nsub, tk):
    if has_lat:
        q_ref, kc_ref, vc_ref, kl_ref, vl_ref, o_ref, m_scr, acc_scr = refs
    else:
        q_ref, kc_ref, vc_ref, o_ref, m_scr, acc_scr = refs
    ts = q_ref.shape[0] // nsub
    dv = vc_ref.shape[1]
    dn = (((1,), (1,)), ((), ()))

    def scores(q, k):
        s = lax.dot_general(q, k, dn, preferred_element_type=F32)
        return s if scale == 1.0 else s * scale

    def with_ones(v):
        lane = lax.broadcasted_iota(jnp.int32, v.shape, 1)
        return jnp.concatenate([v, jnp.where(lane == 0, 1.0, 0.0).astype(v.dtype)], axis=1)

    def lanes(x, n):
        return x if n == LANE else pltpu.repeat(x, n // LANE, axis=1)

    def probs(s, m):
        return jnp.exp2(s - lanes(m, s.shape[1])).astype(BF16)

    kc = kc_ref[...]
    vc = with_ones(vc_ref[...])
    for u in range(nsub):
        s = scores(q_ref[u * ts:(u + 1) * ts, :], kc)
        m = jnp.broadcast_to(jnp.max(s, axis=-1, keepdims=True), (ts, LANE))
        m_scr[u] = m
        acc_scr[u] = jnp.dot(probs(s, m), vc, preferred_element_type=F32)

    if has_lat:
        def body(kb, carry):
            off = pl.multiple_of(kb * tk, tk)
            k = kl_ref[pl.ds(off, tk), :]
            v = with_ones(vl_ref[pl.ds(off, tk), :])
            s_next = scores(q_ref[0:ts, :], k)
            for u in range(nsub):
                s = s_next
                if u + 1 < nsub:
                    s_next = scores(q_ref[(u + 1) * ts:(u + 2) * ts, :], k)
                m_prev = m_scr[u]
                m_new = jnp.maximum(m_prev, jnp.max(s, axis=-1, keepdims=True))
                m_scr[u] = m_new
                acc_scr[u] = (lanes(jnp.exp2(m_prev - m_new), 2 * dv) * acc_scr[u]
                              + jnp.dot(probs(s, m_new), v, preferred_element_type=F32))
            return carry

        lax.fori_loop(0, kl_ref.shape[0] // tk, body, 0)

    for u in range(nsub):
        acc = acc_scr[u]
        o_ref[u * ts:(u + 1) * ts, :] = (acc[:, :dv] / acc[:, dv:dv + 1]).astype(o_ref.dtype)


def _flash(q, qcol, kc, kccol, vc, vccol, heads, dq, dv, batch, lat=None, scale=1.0, tq=2048, nsub=4, tk=1024):
    mq = q.shape[0]
    sq = mq // batch
    tq = min(tq, sq)
    nsub = min(nsub, tq // 256)
    nq = sq // tq
    tc = kc.shape[0] // batch
    has_lat = lat is not None
    in_specs = [pl.BlockSpec((tq, dq), lambda b, h, i: (b * nq + i, qcol + h)),
                pl.BlockSpec((tc, dq), lambda b, h, i: (b, kccol + h)),
                pl.BlockSpec((tc, dv), lambda b, h, i: (b, vccol + h))]
    args = [q, kc, vc]
    if has_lat:
        kl, klcol, vl, vlcol = lat
        sk = kl.shape[0] // batch
        tk = min(tk, sk)
        in_specs += [pl.BlockSpec((sk, dq), lambda b, h, i: (b, klcol + h)),
                     pl.BlockSpec((sk, dv), lambda b, h, i: (b, vlcol + h))]
        args += [kl, vl]
    return pl.pallas_call(
        functools.partial(_flash_kernel, has_lat=has_lat, scale=scale, nsub=nsub, tk=tk),
        grid=(batch, heads, nq),
        in_specs=in_specs,
        out_specs=pl.BlockSpec((tq, dv), lambda b, h, i: (b * nq + i, h)),
        out_shape=jax.ShapeDtypeStruct((mq, heads * dv), BF16),
        scratch_shapes=[pltpu.VMEM((nsub, tq // nsub, LANE), F32), pltpu.VMEM((nsub, tq // nsub, 2 * dv), F32)],
        compiler_params=_cp(("parallel", "parallel", "arbitrary")),
    )(*args)


NA_RB = 8
NA_BLK = NA_RB * GRID_W
NA_GROUP = 2


def _na_kernel(q_ref, kp_ref, kc_ref, kn_ref, vp_ref, vc_ref, vn_ref, kx_ref, vx_ref, bias_ref, o_ref,
               kw_scr, vw_scr, *, rows, scale):
    j = pl.program_id(1)
    for t, (kr, vr) in enumerate(((kp_ref, vp_ref), (kc_ref, vc_ref), (kn_ref, vn_ref))):
        kw_scr[t * NA_BLK:(t + 1) * NA_BLK, :] = kr[...]
        vw_scr[t * NA_BLK:(t + 1) * NA_BLK, :] = vr[...]
    nwin = NA_KR * GRID_W
    dn = (((1,), (1,)), ((), ()))
    heads = [slice(h * NA_DH, (h + 1) * NA_DH) for h in range(NA_HEADS)]
    for i0 in range(0, NA_RB, NA_GROUP):
        units = []
        for i in range(i0, i0 + NA_GROUP):
            r = NA_RB * j + i
            r0 = jnp.clip(r - NA_KR // 2, 0, rows - NA_KR)
            start = pl.multiple_of((r0 - NA_RB * j + NA_RB) * GRID_W, GRID_W)
            for h, cs in enumerate(heads):
                q = q_ref[i * GRID_W:(i + 1) * GRID_W, cs]
                s_loc = (lax.dot_general(q, kw_scr[pl.ds(start, nwin), cs], dn, preferred_element_type=F32) * scale
                         + bias_ref[h, r - r0])
                s_ctx = lax.dot_general(q, kx_ref[:, cs], dn, preferred_element_type=F32) * scale
                units.append((i, cs, start, s_loc, s_ctx))
        outs = {}
        for i, cs, start, s_loc, s_ctx in units:
            m = jnp.maximum(jnp.max(s_loc, axis=-1, keepdims=True), jnp.max(s_ctx, axis=-1, keepdims=True))
            p_loc = jnp.exp(s_loc - m)
            p_ctx = jnp.exp(s_ctx - m)
            l = jnp.sum(p_loc, axis=-1, keepdims=True) + jnp.sum(p_ctx, axis=-1, keepdims=True)
            o = (jnp.dot(p_loc.astype(BF16), vw_scr[pl.ds(start, nwin), cs], preferred_element_type=F32)
                 + jnp.dot(p_ctx.astype(BF16), vx_ref[:, cs], preferred_element_type=F32))
            outs.setdefault(i, []).append((o / l).astype(o_ref.dtype))
        for i, parts in outs.items():
            o_ref[i * GRID_W:(i + 1) * GRID_W, :] = jnp.concatenate(parts, axis=1)


def _na_bias_table(rpb):
    c = np.arange(GRID_W)
    cs = np.clip(c - NA_KC // 2, 0, GRID_W - NA_KC)
    valid = (c[None, :] >= cs[:, None]) & (c[None, :] < cs[:, None] + NA_KC)
    dc = np.clip(c[None, :] - c[:, None] + (NA_KC - 1), 0, 2 * NA_KC - 2)
    dr = np.arange(NA_KR)[None, :] - np.arange(NA_KR)[:, None] + (NA_KR - 1)
    sel_r = (dr[:, :, None] == np.arange(2 * NA_KR - 1)).astype(np.float32)
    sel_c = ((dc[:, :, None] == np.arange(2 * NA_KC - 1)) & valid[:, :, None]).astype(np.float32)
    tab = jnp.einsum('hab,via,cdb->hvcid', rpb.astype(F32), sel_r, sel_c, precision=lax.Precision.HIGHEST)
    tab = tab + jnp.where(valid[None, None, :, None, :], 0.0, -1e30)
    return tab.reshape(NA_HEADS, NA_KR, GRID_W, NA_KR * GRID_W)


def _na(proj_l, proj_c, bias, batch):
    m = proj_l.shape[0]
    s = m // batch
    rows = s // GRID_W
    nb = rows // NA_RB
    cq, ck, cv = C_NAQ // NA_W, C_NAK // NA_W, C_NAV // NA_W
    tc = proj_c.shape[0] // batch
    prev = lambda b, j: b * nb + jnp.maximum(j - 1, 0)
    cur = lambda b, j: b * nb + j
    nxt = lambda b, j: b * nb + jnp.minimum(j + 1, nb - 1)
    blk = lambda rf, col: pl.BlockSpec((NA_BLK, NA_W), lambda b, j: (rf(b, j), col))
    return pl.pallas_call(
        functools.partial(_na_kernel, rows=rows, scale=NA_DH ** -0.5),
        grid=(batch, nb),
        in_specs=[blk(cur, cq), blk(prev, ck), blk(cur, ck), blk(nxt, ck),
                  blk(prev, cv), blk(cur, cv), blk(nxt, cv),
                  pl.BlockSpec((tc, NA_W), lambda b, j: (b, ck)),
                  pl.BlockSpec((tc, NA_W), lambda b, j: (b, cv)),
                  pl.BlockSpec(bias.shape, lambda b, j: (0, 0, 0, 0))],
        out_specs=pl.BlockSpec((NA_BLK, NA_W), lambda b, j: (b * nb + j, 0)),
        out_shape=jax.ShapeDtypeStruct((m, NA_W), BF16),
        scratch_shapes=[pltpu.VMEM((3 * NA_BLK, NA_W), BF16), pltpu.VMEM((3 * NA_BLK, NA_W), BF16)],
        compiler_params=_cp(("parallel", "arbitrary")),
    )(proj_l, proj_l, proj_l, proj_l, proj_l, proj_l, proj_l, proj_c, proj_c, bias)


def _conv_silu_kernel(x_ref, p_ref, n_ref, w_ref, b_ref, sc_ref, o_ref, *, tiles_per_seg):
    t = pl.program_id(0) % tiles_per_seg
    prev_row = jnp.where(t == 0, 0.0, p_ref[HALO - 1:HALO, :].astype(F32))
    next_row = jnp.where(t == tiles_per_seg - 1, 0.0, n_ref[0:1, :].astype(F32))
    x = x_ref[...].astype(F32)
    tm = x.shape[0]
    rid = lax.broadcasted_iota(jnp.int32, x.shape, 0)
    xp = jnp.where(rid == 0, prev_row, pltpu.roll(x, 1, axis=0))
    xn = jnp.where(rid == tm - 1, next_row, pltpu.roll(x, tm - 1, axis=0))
    w = w_ref[...]
    y = xp * w[0:1, :] + x * w[1:2, :] + xn * w[2:3, :] + b_ref[...]
    o_ref[...] = (y * _sigmoid(y) * sc_ref[...]).astype(o_ref.dtype)


def _conv_silu(x, ncols, w, b, colscale, seg, tc=512):
    m = x.shape[0]
    tm = min(TM, seg)
    hb = tm // HALO
    last = m // HALO - 1
    vec = lambda r: pl.BlockSpec((r, tc), lambda i, j: (0, j))
    return pl.pallas_call(
        functools.partial(_conv_silu_kernel, tiles_per_seg=seg // tm),
        grid=(m // tm, ncols // tc),
        in_specs=[pl.BlockSpec((tm, tc), lambda i, j: (i, j)),
                  pl.BlockSpec((HALO, tc), lambda i, j: (jnp.maximum(i * hb - 1, 0), j)),
                  pl.BlockSpec((HALO, tc), lambda i, j: (jnp.minimum((i + 1) * hb, last), j)),
                  vec(3), vec(1), vec(1)],
        out_specs=pl.BlockSpec((tm, tc), lambda i, j: (i, j)),
        out_shape=jax.ShapeDtypeStruct((m, ncols), BF16),
        compiler_params=_cp(("parallel", "parallel")),
    )(x, x, x, w, b, colscale)


def _split3(x):
    x1 = x.astype(BF16)
    r = x - x1.astype(F32)
    x2 = r.astype(BF16)
    x3 = (r - x2.astype(F32)).astype(BF16)
    return x1, x2, x3


def _log_sigmoid(x):
    return jnp.minimum(x, 0.0) - jnp.log(1.0 + jnp.exp(-jnp.abs(x)))


def _mlstm_dir(d, q_ref, k_ref, v_ref, gc_ref, gr_ref, bc_ref, br_ref, h_ref, c_scr, m_scr):
    L = ML_L
    ri = lax.broadcasted_iota(jnp.int32, (L, L), 0)
    ci = lax.broadcasted_iota(jnp.int32, (L, L), 1)
    mask = (ci <= ri) if d == 0 else (ci >= ri)
    tri = jnp.where(mask, 1.0, 0.0).astype(BF16)
    tri_t = jnp.where((ri <= ci) if d == 0 else (ri >= ci), 1.0, 0.0).astype(BF16)
    lane = lax.broadcasted_iota(jnp.int32, (L, ML_DP), 1)

    gc = gc_ref[...] + bc_ref[...]
    gr = gr_ref[...] + br_ref[...]
    bc_all = sum(jnp.dot(tri, p, preferred_element_type=F32) for p in _split3(_log_sigmoid(gc)))
    br_all = sum(jnp.dot(p, tri_t, preferred_element_type=F32) for p in _split3(_log_sigmoid(gr)))

    hs = range(ML_HEADS)
    cols = [slice(h * ML_DP, (h + 1) * ML_DP) for h in hs]
    q = [q_ref[:, cs] for cs in cols]
    k = [k_ref[:, cs] for cs in cols]
    v = [jnp.where(lane == ML_DH, 1.0, v_ref[:, cs].astype(F32)).astype(BF16) for cs in cols]
    m_prev = [m_scr[h, d][0:1, 0:1] for h in hs]
    ct = [c_scr[h, d] for h in hs]
    qk = [lax.dot_general(q[h], k[h], (((1,), (1,)), ((), ())), preferred_element_type=F32) for h in hs]
    inter = [jnp.dot(q[h], ct[h].astype(BF16), preferred_element_type=F32) for h in hs]

    rep = lambda col: jnp.broadcast_to(col, (L, LANE))
    wide = lambda x: pltpu.repeat(x, ML_DP // LANE, axis=1)
    b_rep = [rep(bc_all[:, 4 * h + 2 + d:4 * h + 3 + d]) for h in hs]
    li_rep = [rep(gc[:, 4 * h + d:4 * h + d + 1]) for h in hs]
    a, w_st, m_t = [], [], []
    for h in hs:
        b_row = br_all[4 * h + 2 + d:4 * h + 3 + d, :]
        li_row = gr[4 * h + d:4 * h + d + 1, :]
        dmat = jnp.where(mask, wide(b_rep[h]) - b_row + li_row, -jnp.inf)
        g = b_rep[h] + m_prev[h]
        m_t.append(jnp.maximum(g, jnp.max(dmat, axis=-1, keepdims=True)))
        w_st.append(jnp.exp(g - m_t[h]))
        a.append((qk[h] * jnp.exp(dmat - wide(m_t[h]))).astype(BF16))

    for h in hs:
        num = wide(w_st[h]) * inter[h] + jnp.dot(a[h], v[h], preferred_element_type=F32)
        den = rep(num[:, ML_DH:ML_DH + 1])
        inv = 1.0 / jnp.maximum(jnp.abs(den), jnp.exp(-m_t[h]))
        h_ref[:, cols[h]] = jnp.where(lane < ML_DH, num * wide(inv), 0.0).astype(h_ref.dtype)

    for h in hs:
        b_end = b_rep[h][L - 1:L, :] if d == 0 else b_rep[h][0:1, :]
        e_log = b_end - b_rep[h] + li_rep[h]
        m_new = jnp.maximum(b_end + m_prev[h], jnp.max(e_log, axis=0, keepdims=True))
        decay = jnp.exp(b_end + m_prev[h] - m_new)[:, 0:1]
        ek = (wide(jnp.exp(e_log - m_new)) * k[h].astype(F32)).astype(BF16)
        c_scr[h, d] = decay * ct[h] + lax.dot_general(ek, v[h], (((0,), (0,)), ((), ())),
                                                      preferred_element_type=F32)
        m_scr[h, d] = jnp.broadcast_to(m_new, m_scr.shape[2:])


def _mlstm_kernel(qf_ref, kf_ref, vf_ref, gcf_ref, grf_ref, qb_ref, kb_ref, vb_ref, gcb_ref, grb_ref,
                  bc_ref, br_ref, c0_ref, m0_ref, hf_ref, hb_ref, c1_ref, m1_ref, c_scr, m_scr):
    c = pl.program_id(1)

    @pl.when(c == 0)
    def _():
        c_scr[...] = c0_ref[...]
        m_scr[...] = m0_ref[...]

    _mlstm_dir(0, qf_ref, kf_ref, vf_ref, gcf_ref, grf_ref, bc_ref, br_ref, hf_ref, c_scr, m_scr)
    _mlstm_dir(1, qb_ref, kb_ref, vb_ref, gcb_ref, grb_ref, bc_ref, br_ref, hb_ref, c_scr, m_scr)

    @pl.when(c == pl.num_programs(1) - 1)
    def _():
        c1_ref[...] = c_scr[...]
        m1_ref[...] = m_scr[...]


def _mlstm(qk, proj, gates, gates_row, bias_col, bias_row, c0, m0, batch):
    m = qk.shape[0]
    s = m // batch
    nc = s // ML_L
    fwd = lambda b, c: b * nc + c
    bwd = lambda b, c: b * nc + (nc - 1 - c)

    def dir_specs(rf):
        return [pl.BlockSpec((ML_L, ML_WP), lambda b, c: (rf(b, c), C_MLQ // ML_WP)),
                pl.BlockSpec((ML_L, ML_WP), lambda b, c: (rf(b, c), C_MLK // ML_WP)),
                pl.BlockSpec((ML_L, ML_WP), lambda b, c: (rf(b, c), C_MLV // ML_WP)),
                pl.BlockSpec((ML_L, C_GATES), lambda b, c: (rf(b, c), 0)),
                pl.BlockSpec((4 * ML_HEADS, ML_L), lambda b, c: (0, rf(b, c)))]

    st_c = pl.BlockSpec((None, ML_HEADS, 2, ML_DP, ML_DP), lambda b, c: (b, 0, 0, 0, 0))
    st_m = pl.BlockSpec((None, ML_HEADS, 2, 8, LANE), lambda b, c: (b, 0, 0, 0, 0))
    return pl.pallas_call(
        _mlstm_kernel,
        grid=(batch, nc),
        in_specs=dir_specs(fwd) + dir_specs(bwd)
        + [pl.BlockSpec((1, C_GATES), lambda b, c: (0, 0)),
           pl.BlockSpec((4 * ML_HEADS, 1), lambda b, c: (0, 0)), st_c, st_m],
        out_specs=[pl.BlockSpec((ML_L, ML_WP), lambda b, c: (fwd(b, c), 0)),
                   pl.BlockSpec((ML_L, ML_WP), lambda b, c: (bwd(b, c), 0)), st_c, st_m],
        out_shape=[jax.ShapeDtypeStruct((m, ML_WP), BF16), jax.ShapeDtypeStruct((m, ML_WP), BF16),
                   jax.ShapeDtypeStruct(c0.shape, F32), jax.ShapeDtypeStruct(m0.shape, F32)],
        scratch_shapes=[pltpu.VMEM((ML_HEADS, 2, ML_DP, ML_DP), F32), pltpu.VMEM((ML_HEADS, 2, 8, LANE), F32)],
        compiler_params=_cp(("parallel", "arbitrary")),
    )(qk, qk, proj, gates, gates_row, qk, qk, proj, gates, gates_row, bias_col, bias_row, c0, m0)


def _outproj_kernel(mla_ref, hf_ref, hb_ref, og_ref, na_ref, w1_ref, w2_ref, w3_ref, res_ref, gate_ref,
                    out_ref, ml_scr):
    @pl.when(pl.program_id(1) == 0)
    def _():
        hsum = hf_ref[...].astype(F32) + hb_ref[...].astype(F32)
        ml_scr[...] = (hsum * _sigmoid(og_ref[...].astype(F32))).astype(BF16)

    acc = (jnp.dot(mla_ref[...], w1_ref[...], preferred_element_type=F32)
           + jnp.dot(ml_scr[...], w2_ref[...], preferred_element_type=F32)
           + jnp.dot(na_ref[...], w3_ref[...], preferred_element_type=F32))
    out_ref[...] = res_ref[...] + gate_ref[...] * acc


def _outproj(mla, hf, hb, proj, na, w_out, w2, l, res, mod4, gate_chunk, rowf, tn=512):
    m, d = res.shape
    tm = min(TM_BIG, m)
    rows = lambda width, col: pl.BlockSpec((tm, width), lambda i, j: (i, col))
    wrows = lambda nrows, blk: pl.BlockSpec((None, nrows, tn), lambda i, j: (l, blk, j))
    return pl.pallas_call(
        _outproj_kernel,
        grid=(m // tm, d // tn),
        in_specs=[rows(MLA_W, 0), rows(ML_WP, 0), rows(ML_WP, 0), rows(ML_WP, C_MLO // ML_WP), rows(NA_W, 0),
                  wrows(MLA_W, 0), wrows(ML_WP, 0), wrows(NA_W, (MLA_W + ML_W) // NA_W),
                  pl.BlockSpec((tm, tn), lambda i, j: (i, j)), _mod_spec(gate_chunk, rowf, tm, tn)],
        out_specs=pl.BlockSpec((tm, tn), lambda i, j: (i, j)),
        out_shape=jax.ShapeDtypeStruct((m, d), F32),
        scratch_shapes=[pltpu.VMEM((tm, ML_WP), BF16)],
        compiler_params=_cp(("parallel", "arbitrary")),
    )(mla, hf, hb, proj, na, w_out, w2, w_out, res, mod4)


def _ffn_kernel(x_ref, xp_ref, xn_ref, g_ref, sc_ref, sh_ref, wg_ref, wv_ref, cwg_ref, cwv_ref, cbg_ref, cbv_ref,
                wd_ref, gate_ref, *rest, tiles_per_seg, final):
    if final:
        fg_ref, o_ref, hn_scr = rest
    else:
        o_ref, hn_scr = rest
    i = pl.program_id(0)
    j = pl.program_id(1)
    tm = x_ref.shape[0]
    te = hn_scr.shape[0]

    @pl.when(j == 0)
    def _():
        nm = lambda x: (_rms(x, g_ref[...]) * (1.0 + sc_ref[...]) + sh_ref[...]).astype(BF16)
        hn_scr[0:HALO, :] = nm(xp_ref[...])
        hn_scr[HALO:HALO + tm, :] = nm(x_ref[...])
        hn_scr[HALO + tm:te, :] = nm(xn_ref[...])
        o_ref[...] = jnp.zeros_like(o_ref)

    t = i % tiles_per_seg
    keep_prev = jnp.where(t == 0, 0.0, 1.0)
    keep_next = jnp.where(t == tiles_per_seg - 1, 0.0, 1.0)
    hn = hn_scr[...]

    def conv(w_ref, cw_ref, cb_ref):
        u = jnp.dot(hn, w_ref[...], preferred_element_type=F32)
        rid = lax.broadcasted_iota(jnp.int32, (tm, u.shape[1]), 0)
        up = pltpu.roll(u, 1, axis=0)[HALO:HALO + tm]
        un = pltpu.roll(u, te - 1, axis=0)[HALO:HALO + tm]
        up = jnp.where(rid == 0, up * keep_prev, up)
        un = jnp.where(rid == tm - 1, un * keep_next, un)
        cw = cw_ref[...]
        return up * cw[0:1, :] + u[HALO:HALO + tm] * cw[1:2, :] + un * cw[2:3, :] + cb_ref[...]

    gg = conv(wg_ref, cwg_ref, cbg_ref)
    vv = conv(wv_ref, cwv_ref, cbv_ref)
    act = (gg * _sigmoid(gg) * vv).astype(BF16)
    o_ref[...] += jnp.dot(act, wd_ref[...], preferred_element_type=F32)

    @pl.when(j == pl.num_programs(1) - 1)
    def _():
        y = x_ref[...] + gate_ref[...] * o_ref[...]
        o_ref[...] = _rms(y, fg_ref[...]) if final else y


def _ffn(x, g, mod4, rowf, w_up, cw, cb, w_down, l, seg, final_g=None, tf=512):
    m, d = x.shape
    dff = w_down.shape[1]
    tm = min(TM_BIG, seg)
    nj = dff // tf
    hb = tm // HALO
    last = m // HALO - 1
    final = final_g is not None
    vec = lambda r, off: pl.BlockSpec((None, r, tf), lambda i, j: (l, 0, off + j))
    once = pl.Buffered(1)
    in_specs = [pl.BlockSpec((tm, d), lambda i, j: (i, 0), pipeline_mode=once),
                pl.BlockSpec((HALO, d), lambda i, j: (jnp.maximum(i * hb - 1, 0), 0)),
                pl.BlockSpec((HALO, d), lambda i, j: (jnp.minimum((i + 1) * hb, last), 0)),
                pl.BlockSpec((1, d), lambda i, j: (0, 0)),
                _mod_spec(4, rowf, tm), _mod_spec(3, rowf, tm),
                pl.BlockSpec((None, d, tf), lambda i, j: (l, 0, j)),
                pl.BlockSpec((None, d, tf), lambda i, j: (l, 0, nj + j)),
                vec(3, 0), vec(3, nj), vec(1, 0), vec(1, nj),
                pl.BlockSpec((None, tf, d), lambda i, j: (l, j, 0)), _mod_spec(5, rowf, tm)]
    args = [x, x, x, g.reshape(1, d), mod4, mod4, w_up, w_up, cw, cw, cb, cb, w_down, mod4]
    if final:
        in_specs.append(pl.BlockSpec((1, d), lambda i, j: (0, 0)))
        args.append(final_g.reshape(1, d))
    return pl.pallas_call(
        functools.partial(_ffn_kernel, tiles_per_seg=seg // tm, final=final),
        grid=(m // tm, nj),
        in_specs=in_specs,
        out_specs=pl.BlockSpec((tm, d), lambda i, j: (i, 0), pipeline_mode=once),
        out_shape=jax.ShapeDtypeStruct((m, d), F32),
        scratch_shapes=[pltpu.VMEM((tm + 2 * HALO, d), BF16)],
        compiler_params=_cp(("parallel", "arbitrary")),
    )(*args)


def _pad_heads(a, heads, dh, dp, axis):
    shp = a.shape
    a = a.reshape(shp[:axis] + (heads, dh) + shp[axis + 1:])
    pad = [(0, 0)] * a.ndim
    pad[axis + 1] = (0, dp - dh)
    a = jnp.pad(a, pad)
    return a.reshape(shp[:axis] + (heads * dp,) + shp[axis + 1:])


def _rope_cols(a):
    ev, od = a[..., 0::2], a[..., 1::2]
    return jnp.concatenate([ev, od, od, ev], axis=-1)


def _prep_w_in(w):
    w = w.astype(BF16)
    qlat, kvlat, krope, mq, mk, mv, mo, gi, gf, nq, nk, nv = jnp.split(w, IN_OFFSETS, axis=-1)
    lead = w.shape[:-1]
    ph = lambda a: _pad_heads(a, ML_HEADS, ML_DH, ML_DP, a.ndim - 1)
    gi = gi.reshape(lead + (2, ML_HEADS))
    gf = gf.reshape(lead + (2, ML_HEADS))
    g4 = jnp.stack([gi[..., 0, :], gi[..., 1, :], gf[..., 0, :], gf[..., 1, :]], axis=-1)
    gates = g4.reshape(lead + (4 * ML_HEADS,))
    out = jnp.concatenate([ph(mq), ph(mk), ph(mv), ph(mo), qlat, nq, nk, nv, kvlat, _rope_cols(krope), gates],
                          axis=-1)
    return jnp.pad(out, [(0, 0)] * len(lead) + [(0, C_IN - out.shape[-1])])


def _prep_w_uq(w):
    w = w.reshape(w.shape[0], MLA_HEADS, MLA_NOPE + MLA_ROPE)
    out = jnp.concatenate([w[..., :MLA_NOPE], _rope_cols(w[..., MLA_NOPE:])], axis=-1)
    return out.reshape(w.shape[0], MLA_HEADS * MLA_DQ).astype(BF16)


def _gate_bias(i_bias, f_bias):
    g4 = jnp.stack([i_bias[0], i_bias[1], f_bias[0], f_bias[1]], axis=-1).astype(F32).reshape(4 * ML_HEADS)
    return jnp.pad(g4, (0, LANE - 4 * ML_HEADS)).reshape(1, LANE), g4.reshape(4 * ML_HEADS, 1)


def _rope_table(n_tokens):
    t = jnp.arange(n_tokens)
    row = (t // GRID_W).astype(F32)
    col = (t % GRID_W).astype(F32)
    n_freq = MLA_ROPE // 4
    inv = ROPE_BASE ** (-jnp.arange(n_freq, dtype=F32) / n_freq)
    ang = jnp.concatenate([row[:, None] * inv, col[:, None] * inv], axis=-1)
    cos, sin = jnp.cos(ang), jnp.sin(ang)
    return jnp.concatenate([cos, cos, -sin, sin], axis=-1)


def kernel(x, c, ctx, c_ctx, ada_w, ada_b, norm1_g, norm2_g, w_in, mla_q_norm, mla_kv_norm, mla_w_uq, mla_w_uk, mla_w_uv, ml_conv_w, ml_conv_b, ml_i_bias, ml_f_bias, na_rpb, w_out, ffn_w_up, ffn_conv_w, ffn_conv_b, ffn_w_down, final_norm_g):
    batch, seq, d = x.shape
    tctx = ctx.shape[1]
    depth = ada_w.shape[0]
    assert d == D_MODEL and batch <= 7 and tctx == ML_L and tctx % HALO == 0
    assert seq % TM_BIG == 0 and seq % NA_BLK == 0 and seq // GRID_W >= NA_KR

    h = x.reshape(batch * seq, d)
    hc = ctx.reshape(batch * tctx, d)
    row_lat = lambda i, tm: i // (seq // tm)
    row_ctx = lambda i, tm: batch
    cond8 = jnp.zeros((8, d), F32).at[:batch].set(c).at[batch].set(c_ctx)

    tab_lat = _rope_table(seq)
    tab_ctx = jnp.concatenate([jnp.ones((tctx, MLA_ROPE), F32), jnp.zeros((tctx, MLA_ROPE), F32)], axis=-1)
    ml_scale = jnp.concatenate([jnp.ones((1, ML_WP), F32), jnp.full((1, ML_WP), ML_DH ** -0.5, F32)], axis=-1)

    w_in_p = _prep_w_in(w_in)
    w_out_p = w_out.astype(BF16)
    w2_p = _pad_heads(w_out_p[:, MLA_W:MLA_W + ML_W], ML_HEADS, ML_DH, ML_DP, 1)
    w_up_p = ffn_w_up.astype(BF16)
    w_down_p = ffn_w_down.astype(BF16)
    fcb = ffn_conv_b.reshape(depth, 1, -1)

    for l in range(depth):
        last = l == depth - 1
        mod4 = _ada(cond8, ada_w, ada_b, l).reshape(8, N_MOD, 1, d)

        wuq = _prep_w_uq(mla_w_uq[l])
        wuk = mla_w_uk[l].astype(BF16)
        wuv = mla_w_uv[l].astype(BF16)
        qg = mla_q_norm[l].reshape(1, -1)
        kvg = mla_kv_norm[l].reshape(1, -1)
        cw = jnp.concatenate([_pad_heads(ml_conv_w[l][:, :ML_W], ML_HEADS, ML_DH, ML_DP, 1),
                              _pad_heads(ml_conv_w[l][:, ML_W:], ML_HEADS, ML_DH, ML_DP, 1)], axis=-1)
        cb = jnp.concatenate([_pad_heads(ml_conv_b[l][None, :ML_W], ML_HEADS, ML_DH, ML_DP, 1),
                              _pad_heads(ml_conv_b[l][None, ML_W:], ML_HEADS, ML_DH, ML_DP, 1)], axis=-1)
        bias_col, bias_row = _gate_bias(ml_i_bias[l], ml_f_bias[l])
        na_bias = _na_bias_table(na_rpb[l])

        def mixers_in(hh, rowf):
            proj, gates = _nm_mm(hh, norm1_g[l], mod4, 1, 0, rowf, w_in_p, l, TN_IN, C_GATES)
            return proj, gates, gates[:, :4 * ML_HEADS].T

        proj_c, gates_c, grow_c = mixers_in(hc, row_ctx)
        proj_l, gates_l, grow_l = mixers_in(h, row_lat)

        q_c, k_c, v_c = _mla_up(proj_c, tab_ctx, qg, kvg, wuq, wuk, wuv)
        q_l, k_l, v_l = _mla_up(proj_l, tab_lat, qg, kvg, wuq, wuk, wuv)
        mla_lat = _flash(q_l, 0, k_c, 0, v_c, 0, MLA_HEADS, MLA_DQ, MLA_V, batch, lat=(k_l, 0, v_l, 0))

        qk_c = _conv_silu(proj_c, 2 * ML_WP, cw, cb, ml_scale, tctx)
        qk_l = _conv_silu(proj_l, 2 * ML_WP, cw, cb, ml_scale, seq)
        c0 = jnp.zeros((batch, ML_HEADS, 2, ML_DP, ML_DP), F32)
        m0 = jnp.zeros((batch, ML_HEADS, 2, 8, LANE), F32)
        hf_c, hb_c, c1, m1 = _mlstm(qk_c, proj_c, gates_c, grow_c, bias_col, bias_row, c0, m0, batch)
        hf_l, hb_l, _, _ = _mlstm(qk_l, proj_l, gates_l, grow_l, bias_col, bias_row, c1, m1, batch)

        na_lat = _na(proj_l, proj_c, na_bias, batch)

        h = _outproj(mla_lat, hf_l, hb_l, proj_l, na_lat, w_out_p, w2_p, l, h, mod4, 2, row_lat)
        h = _ffn(h, norm2_g[l], mod4, row_lat, w_up_p, ffn_conv_w, fcb, w_down_p, l, seq,
                 final_g=final_norm_g if last else None)

        if not last:
            mla_ctx = _flash(q_c, 0, k_c, 0, v_c, 0, MLA_HEADS, MLA_DQ, MLA_V, batch)
            na_ctx = _flash(proj_c, C_NAQ // NA_DH, proj_c, C_NAK // NA_DH, proj_c, C_NAV // NA_DH,
                            NA_HEADS, NA_DH, NA_DH, batch, scale=NA_DH ** -0.5 * LOG2E)
            hc = _outproj(mla_ctx, hf_c, hb_c, proj_c, na_ctx, w_out_p, w2_p, l, hc, mod4, 2, row_ctx)
            hc = _ffn(hc, norm2_g[l], mod4, row_ctx, w_up_p, ffn_conv_w, fcb, w_down_p, l, tctx)

    return h.reshape(batch, seq, d)
```

```python
import functools
import math

import numpy as np
import jax
import jax.numpy as jnp
from jax import lax
from jax.experimental import pallas as pl
from jax.experimental.pallas import tpu as pltpu

F32 = jnp.float32
BF16 = jnp.bfloat16

D_MODEL = 2048
GRID_W = 64
MLA_HEADS = 6
MLA_NOPE = 128
MLA_ROPE = 64
MLA_V = 128
MLA_Q_LORA = 512
MLA_KV_LORA = 256
ML_HEADS = 4
ML_DH = 192
NA_HEADS = 4
NA_DH = 128
NA_KR = 8
NA_KC = 16
ROPE_BASE = 10000.0
EPS = 1e-6
N_MOD = 6
ML_W = ML_HEADS * ML_DH
NA_W = NA_HEADS * NA_DH
MLA_W = MLA_HEADS * MLA_V
IN_SIZES = (MLA_Q_LORA, MLA_KV_LORA, MLA_ROPE, ML_W, ML_W, ML_W, ML_W, 2 * ML_HEADS, 2 * ML_HEADS, NA_W, NA_W, NA_W)
IN_OFFSETS = tuple(sum(IN_SIZES[:i + 1]) for i in range(len(IN_SIZES) - 1))
LOG2E = math.log2(math.e)

LANE = 128
ML_DP = 256
ML_WP = ML_HEADS * ML_DP
ML_L = 256
MLA_DQ = 256
TM = 512
TM_BIG = 1024
HALO = 16
VMEM_LIMIT = 58 * 1024 * 1024

C_MLQ = 0
C_MLK = C_MLQ + ML_W
C_MLV = C_MLK + ML_W
C_MLO = C_MLV + ML_W
C_QLAT = C_MLO + ML_W
C_KVLAT = C_QLAT + MLA_Q_LORA
C_KROPE = C_KVLAT + MLA_KV_LORA
C_GATE0 = C_KROPE + LANE
C_GATES = LANE
C_NAQ = C_GATE0 + C_GATES
C_NAK = C_NAQ + NA_W
C_NAV = C_NAK + NA_W
C_IN = C_NAV + NA_W
TN_IN = 512
ML_PAIR = 2 * ML_DH


def _cp(sem):
    return pltpu.CompilerParams(dimension_semantics=sem, vmem_limit_bytes=VMEM_LIMIT)


def _sigmoid(x):
    return 1.0 / (1.0 + jnp.exp(-x))


def _rms(x, g):
    return x * lax.rsqrt(jnp.mean(x * x, axis=-1, keepdims=True) + EPS) * g


def _mod_spec(chunk, rowf, tm, width=D_MODEL):
    return pl.BlockSpec((None, None, 1, width), lambda i, j: (rowf(i, tm), chunk, 0, j if width != D_MODEL else 0))


def _ada_kernel(c_ref, w_ref, b_ref, o_ref):
    c = c_ref[...]
    a = c * _sigmoid(c)
    o_ref[...] = jnp.dot(a, w_ref[...], preferred_element_type=F32,
                         precision=lax.Precision.HIGHEST) + b_ref[...]


def _ada(cond8, w, b, l):
    _, d, n = w.shape
    tn = 1024
    return pl.pallas_call(
        _ada_kernel,
        grid=(n // tn,),
        in_specs=[pl.BlockSpec((8, d), lambda j: (0, 0)),
                  pl.BlockSpec((None, d, tn), lambda j: (l, 0, j)),
                  pl.BlockSpec((None, 1, tn), lambda j: (l, 0, j))],
        out_specs=pl.BlockSpec((8, tn), lambda j: (0, j)),
        out_shape=jax.ShapeDtypeStruct((8, n), F32),
        compiler_params=_cp(("arbitrary",)),
    )(cond8, w, b.reshape(b.shape[0], 1, n))


def _nm_mm_kernel(x_ref, g_ref, sc_ref, sh_ref, w_ref, o_ref, t_ref, hn_ref, *, f32_col):
    j = pl.program_id(1)
    tn = w_ref.shape[1]

    @pl.when(j == 0)
    def _():
        y = _rms(x_ref[...], g_ref[...])
        hn_ref[...] = (y * (1.0 + sc_ref[...]) + sh_ref[...]).astype(BF16)

    acc = jnp.dot(hn_ref[...], w_ref[...], preferred_element_type=F32)
    o_ref[...] = acc.astype(o_ref.dtype)

    @pl.when(j == f32_col // tn)
    def _():
        t_ref[...] = acc[:, f32_col % tn:f32_col % tn + t_ref.shape[1]]


def _nm_mm(x, g, mod4, sc_chunk, sh_chunk, rowf, w, l, tn, f32_col, f32_cols):
    m, d = x.shape
    n = w.shape[2]
    tm = min(TM_BIG, m)
    return pl.pallas_call(
        functools.partial(_nm_mm_kernel, f32_col=f32_col),
        grid=(m // tm, n // tn),
        in_specs=[pl.BlockSpec((tm, d), lambda i, j: (i, 0)),
                  pl.BlockSpec((1, d), lambda i, j: (0, 0)),
                  _mod_spec(sc_chunk, rowf, tm), _mod_spec(sh_chunk, rowf, tm),
                  pl.BlockSpec((None, d, tn), lambda i, j: (l, 0, j))],
        out_specs=[pl.BlockSpec((tm, tn), lambda i, j: (i, j)),
                   pl.BlockSpec((tm, f32_cols), lambda i, j: (i, 0))],
        out_shape=[jax.ShapeDtypeStruct((m, n), BF16), jax.ShapeDtypeStruct((m, f32_cols), F32)],
        scratch_shapes=[pltpu.VMEM((tm, d), BF16)],
        compiler_params=_cp(("parallel", "arbitrary")),
    )(x, g.reshape(1, d), mod4, mod4, w)


def _mla_up_kernel(ql_ref, kvl_ref, kr_ref, tab_ref, qg_ref, kvg_ref, wuq_ref, wuk_ref, wuv_ref,
                   q_ref, k_ref, v_ref, *, scale):
    tab = tab_ref[...]
    qn = _rms(ql_ref[...].astype(F32), qg_ref[...]).astype(BF16)
    qf = jnp.dot(qn, wuq_ref[...], preferred_element_type=F32)
    for h in range(MLA_HEADS):
        c0 = h * MLA_DQ
        y = qf[:, c0 + MLA_NOPE:c0 + MLA_DQ] * tab
        z = y + pltpu.roll(y, MLA_ROPE, axis=1)
        q_ref[:, c0:c0 + MLA_NOPE] = (qf[:, c0:c0 + MLA_NOPE] * scale).astype(BF16)
        q_ref[:, c0 + MLA_NOPE:c0 + MLA_DQ] = (z * scale).astype(BF16)
    cn = _rms(kvl_ref[...].astype(F32), kvg_ref[...]).astype(BF16)
    kn = jnp.dot(cn, wuk_ref[...], preferred_element_type=F32)
    v_ref[...] = jnp.dot(cn, wuv_ref[...], preferred_element_type=F32).astype(BF16)
    ky = kr_ref[...].astype(F32) * tab
    kz = ky + pltpu.roll(ky, MLA_ROPE, axis=1)
    lane = lax.broadcasted_iota(jnp.int32, kz.shape, 1)
    kz = jnp.where(lane < MLA_ROPE, kz, 0.0).astype(BF16)
    for h in range(MLA_HEADS):
        c0 = h * MLA_DQ
        k_ref[:, c0:c0 + MLA_NOPE] = kn[:, h * MLA_NOPE:(h + 1) * MLA_NOPE].astype(BF16)
        k_ref[:, c0 + MLA_NOPE:c0 + MLA_DQ] = kz


def _mla_up(proj, tab, qg, kvg, wuq, wuk, wuv):
    m = proj.shape[0]
    tm = min(TM, tab.shape[0])
    nt = tab.shape[0] // tm
    full = lambda a: pl.BlockSpec(a.shape, lambda i: (0, 0))
    return pl.pallas_call(
        functools.partial(_mla_up_kernel, scale=(MLA_NOPE + MLA_ROPE) ** -0.5 * LOG2E),
        grid=(m // tm,),
        in_specs=[pl.BlockSpec((tm, MLA_Q_LORA), lambda i: (i, C_QLAT // MLA_Q_LORA)),
                  pl.BlockSpec((tm, MLA_KV_LORA), lambda i: (i, C_KVLAT // MLA_KV_LORA)),
                  pl.BlockSpec((tm, LANE), lambda i: (i, C_KROPE // LANE)),
                  pl.BlockSpec((tm, LANE), lambda i: (i % nt, 0)),
                  full(qg), full(kvg), full(wuq), full(wuk), full(wuv)],
        out_specs=[pl.BlockSpec((tm, MLA_HEADS * MLA_DQ), lambda i: (i, 0)),
                   pl.BlockSpec((tm, MLA_HEADS * MLA_DQ), lambda i: (i, 0)),
                   pl.BlockSpec((tm, MLA_W), lambda i: (i, 0))],
        out_shape=[jax.ShapeDtypeStruct((m, MLA_HEADS * MLA_DQ), BF16),
                   jax.ShapeDtypeStruct((m, MLA_HEADS * MLA_DQ), BF16),
                   jax.ShapeDtypeStruct((m, MLA_W), BF16)],
        compiler_params=_cp(("parallel",)),
    )(proj, proj, proj, tab, qg, kvg, wuq, wuk, wuv)


def _flash_kernel(*refs, has_lat, scale, nsub, tk):
    if has_lat:
        q_ref, kc_ref, vc_ref, kl_ref, vl_ref, o_ref, m_scr, acc_scr = refs
    else:
        q_ref, kc_ref, vc_ref, o_ref, m_scr, acc_scr = refs
    ts = q_ref.shape[0] // nsub
    dv = vc_ref.shape[1]
    dn = (((1,), (1,)), ((), ()))

    def scores(q, k):
        s = lax.dot_general(q, k, dn, preferred_element_type=F32)
        return s if scale == 1.0 else s * scale

    def with_ones(v):
        lane = lax.broadcasted_iota(jnp.int32, v.shape, 1)
        return jnp.concatenate([v, jnp.where(lane == 0, 1.0, 0.0).astype(v.dtype)], axis=1)

    def lanes(x, n):
        return x if n == LANE else pltpu.repeat(x, n // LANE, axis=1)

    def probs(s, m):
        return jnp.exp2(s - lanes(m, s.shape[1])).astype(BF16)

    kc = kc_ref[...]
    vc = with_ones(vc_ref[...])
    for u in range(nsub):
        s = scores(q_ref[u * ts:(u + 1) * ts, :], kc)
        m = jnp.broadcast_to(jnp.max(s, axis=-1, keepdims=True), (ts, LANE))
        m_scr[u] = m
        acc_scr[u] = jnp.dot(probs(s, m), vc, preferred_element_type=F32)

    if has_lat:
        def body(kb, carry):
            off = pl.multiple_of(kb * tk, tk)
            k = kl_ref[pl.ds(off, tk), :]
            v = with_ones(vl_ref[pl.ds(off, tk), :])
            s_next = scores(q_ref[0:ts, :], k)
            for u in range(nsub):
                s = s_next
                if u + 1 < nsub:
                    s_next = scores(q_ref[(u + 1) * ts:(u + 2) * ts, :], k)
                m_prev = m_scr[u]
                m_new = jnp.maximum(m_prev, jnp.max(s, axis=-1, keepdims=True))
                m_scr[u] = m_new
                acc_scr[u] = (lanes(jnp.exp2(m_prev - m_new), 2 * dv) * acc_scr[u]
                              + jnp.dot(probs(s, m_new), v, preferred_element_type=F32))
            return carry

        lax.fori_loop(0, kl_ref.shape[0] // tk, body, 0)

    for u in range(nsub):
        acc = acc_scr[u]
        o_ref[u * ts:(u + 1) * ts, :] = (acc[:, :dv] / acc[:, dv:dv + 1]).astype(o_ref.dtype)


def _flash(q, qcol, kc, kccol, vc, vccol, heads, dq, dv, batch, lat=None, scale=1.0, tq=2048, nsub=4, tk=2048):
    mq = q.shape[0]
    sq = mq // batch
    tq = min(tq, sq)
    nsub = min(nsub, tq // 256)
    nq = sq // tq
    tc = kc.shape[0] // batch
    has_lat = lat is not None
    in_specs = [pl.BlockSpec((tq, dq), lambda b, h, i: (b * nq + i, qcol + h)),
                pl.BlockSpec((tc, dq), lambda b, h, i: (b, kccol + h)),
                pl.BlockSpec((tc, dv), lambda b, h, i: (b, vccol + h))]
    args = [q, kc, vc]
    if has_lat:
        kl, klcol, vl, vlcol = lat
        sk = kl.shape[0] // batch
        tk = min(tk, sk)
        in_specs += [pl.BlockSpec((sk, dq), lambda b, h, i: (b, klcol + h)),
                     pl.BlockSpec((sk, dv), lambda b, h, i: (b, vlcol + h))]
        args += [kl, vl]
    return pl.pallas_call(
        functools.partial(_flash_kernel, has_lat=has_lat, scale=scale, nsub=nsub, tk=tk),
        grid=(batch, heads, nq),
        in_specs=in_specs,
        out_specs=pl.BlockSpec((tq, dv), lambda b, h, i: (b * nq + i, h)),
        out_shape=jax.ShapeDtypeStruct((mq, heads * dv), BF16),
        scratch_shapes=[pltpu.VMEM((nsub, tq // nsub, LANE), F32), pltpu.VMEM((nsub, tq // nsub, 2 * dv), F32)],
        compiler_params=_cp(("parallel", "parallel", "arbitrary")),
    )(*args)


NA_RB = 8
NA_BLK = NA_RB * GRID_W
NA_GROUP = 2


def _na_kernel(q_ref, kp_ref, kc_ref, kn_ref, vp_ref, vc_ref, vn_ref, kx_ref, vx_ref, bias_ref, o_ref,
               kw_scr, vw_scr, *, rows, scale):
    j = pl.program_id(1)
    for t, (kr, vr) in enumerate(((kp_ref, vp_ref), (kc_ref, vc_ref), (kn_ref, vn_ref))):
        kw_scr[t * NA_BLK:(t + 1) * NA_BLK, :] = kr[...]
        vw_scr[t * NA_BLK:(t + 1) * NA_BLK, :] = vr[...]
    nwin = NA_KR * GRID_W
    dn = (((1,), (1,)), ((), ()))
    heads = [slice(h * NA_DH, (h + 1) * NA_DH) for h in range(NA_HEADS)]
    for i0 in range(0, NA_RB, NA_GROUP):
        units = []
        for i in range(i0, i0 + NA_GROUP):
            r = NA_RB * j + i
            r0 = jnp.clip(r - NA_KR // 2, 0, rows - NA_KR)
            start = pl.multiple_of((r0 - NA_RB * j + NA_RB) * GRID_W, GRID_W)
            for h, cs in enumerate(heads):
                q = q_ref[i * GRID_W:(i + 1) * GRID_W, cs]
                s_loc = (lax.dot_general(q, kw_scr[pl.ds(start, nwin), cs], dn, preferred_element_type=F32) * scale
                         + bias_ref[h, r - r0])
                s_ctx = lax.dot_general(q, kx_ref[:, cs], dn, preferred_element_type=F32) * scale
                units.append((i, cs, start, s_loc, s_ctx))
        outs = {}
        for i, cs, start, s_loc, s_ctx in units:
            m = jnp.maximum(jnp.max(s_loc, axis=-1, keepdims=True), jnp.max(s_ctx, axis=-1, keepdims=True))
            p_loc = jnp.exp(s_loc - m)
            p_ctx = jnp.exp(s_ctx - m)
            l = jnp.sum(p_loc, axis=-1, keepdims=True) + jnp.sum(p_ctx, axis=-1, keepdims=True)
            o = (jnp.dot(p_loc.astype(BF16), vw_scr[pl.ds(start, nwin), cs], preferred_element_type=F32)
                 + jnp.dot(p_ctx.astype(BF16), vx_ref[:, cs], preferred_element_type=F32))
            outs.setdefault(i, []).append((o / l).astype(o_ref.dtype))
        for i, parts in outs.items():
            o_ref[i * GRID_W:(i + 1) * GRID_W, :] = jnp.concatenate(parts, axis=1)


def _na_bias_table(rpb):
    c = np.arange(GRID_W)
    cs = np.clip(c - NA_KC // 2, 0, GRID_W - NA_KC)
    valid = (c[None, :] >= cs[:, None]) & (c[None, :] < cs[:, None] + NA_KC)
    dc = np.clip(c[None, :] - c[:, None] + (NA_KC - 1), 0, 2 * NA_KC - 2)
    dr = np.arange(NA_KR)[None, :] - np.arange(NA_KR)[:, None] + (NA_KR - 1)
    sel_r = (dr[:, :, None] == np.arange(2 * NA_KR - 1)).astype(np.float32)
    sel_c = ((dc[:, :, None] == np.arange(2 * NA_KC - 1)) & valid[:, :, None]).astype(np.float32)
    tab = jnp.einsum('hab,via,cdb->hvcid', rpb.astype(F32), sel_r, sel_c, precision=lax.Precision.HIGHEST)
    tab = tab + jnp.where(valid[None, None, :, None, :], 0.0, -1e30)
    return tab.reshape(NA_HEADS, NA_KR, GRID_W, NA_KR * GRID_W)


def _na(proj_l, proj_c, bias, batch):
    m = proj_l.shape[0]
    s = m // batch
    rows = s // GRID_W
    nb = rows // NA_RB
    cq, ck, cv = C_NAQ // NA_W, C_NAK // NA_W, C_NAV // NA_W
    tc = proj_c.shape[0] // batch
    prev = lambda b, j: b * nb + jnp.maximum(j - 1, 0)
    cur = lambda b, j: b * nb + j
    nxt = lambda b, j: b * nb + jnp.minimum(j + 1, nb - 1)
    blk = lambda rf, col: pl.BlockSpec((NA_BLK, NA_W), lambda b, j: (rf(b, j), col))
    return pl.pallas_call(
        functools.partial(_na_kernel, rows=rows, scale=NA_DH ** -0.5),
        grid=(batch, nb),
        in_specs=[blk(cur, cq), blk(prev, ck), blk(cur, ck), blk(nxt, ck),
                  blk(prev, cv), blk(cur, cv), blk(nxt, cv),
                  pl.BlockSpec((tc, NA_W), lambda b, j: (b, ck)),
                  pl.BlockSpec((tc, NA_W), lambda b, j: (b, cv)),
                  pl.BlockSpec(bias.shape, lambda b, j: (0, 0, 0, 0))],
        out_specs=pl.BlockSpec((NA_BLK, NA_W), lambda b, j: (b * nb + j, 0)),
        out_shape=jax.ShapeDtypeStruct((m, NA_W), BF16),
        scratch_shapes=[pltpu.VMEM((3 * NA_BLK, NA_W), BF16), pltpu.VMEM((3 * NA_BLK, NA_W), BF16)],
        compiler_params=_cp(("parallel", "arbitrary")),
    )(proj_l, proj_l, proj_l, proj_l, proj_l, proj_l, proj_l, proj_c, proj_c, bias)


def _ml_prep_kernel(x_ref, p_ref, n_ref, w_ref, b_ref, sc_ref, o_ref, *, tiles_per_seg, n_conv):
    j = pl.program_id(1)
    x = x_ref[...].astype(F32)
    tm = x.shape[0]

    def widen(y, fill):
        lane = lax.broadcasted_iota(jnp.int32, (tm, ML_DP - ML_DH), 1)
        spare = jnp.where(lane == 0, fill, 0.0)
        return jnp.concatenate([y[:, :ML_DH], spare, y[:, ML_DH:], spare], axis=1).astype(o_ref.dtype)

    @pl.when(j < n_conv)
    def _():
        t = pl.program_id(0) % tiles_per_seg
        prev_row = jnp.where(t == 0, 0.0, p_ref[HALO - 1:HALO, :].astype(F32))
        next_row = jnp.where(t == tiles_per_seg - 1, 0.0, n_ref[0:1, :].astype(F32))
        rid = lax.broadcasted_iota(jnp.int32, x.shape, 0)
        xp = jnp.where(rid == 0, prev_row, pltpu.roll(x, 1, axis=0))
        xn = jnp.where(rid == tm - 1, next_row, pltpu.roll(x, tm - 1, axis=0))
        w = w_ref[...]
        y = xp * w[0:1, :] + x * w[1:2, :] + xn * w[2:3, :] + b_ref[...]
        o_ref[...] = widen(y * _sigmoid(y) * sc_ref[...], 0.0)

    @pl.when(j >= n_conv)
    def _():
        o_ref[...] = widen(x, 1.0)


def _ml_prep(proj, w, b, colscale, seg):
    m = proj.shape[0]
    tm = min(TM, seg)
    hb = tm // HALO
    last = m // HALO - 1
    n_conv = w.shape[1] // ML_PAIR
    n_all = 3 * ML_W // ML_PAIR
    vec = lambda r: pl.BlockSpec((r, ML_PAIR), lambda i, j: (0, jnp.minimum(j, n_conv - 1)))
    return pl.pallas_call(
        functools.partial(_ml_prep_kernel, tiles_per_seg=seg // tm, n_conv=n_conv),
        grid=(m // tm, n_all),
        in_specs=[pl.BlockSpec((tm, ML_PAIR), lambda i, j: (i, j)),
                  pl.BlockSpec((HALO, ML_PAIR), lambda i, j: (jnp.maximum(i * hb - 1, 0), j)),
                  pl.BlockSpec((HALO, ML_PAIR), lambda i, j: (jnp.minimum((i + 1) * hb, last), j)),
                  vec(3), vec(1), vec(1)],
        out_specs=pl.BlockSpec((tm, 2 * ML_DP), lambda i, j: (i, j)),
        out_shape=jax.ShapeDtypeStruct((m, 3 * ML_WP), BF16),
        compiler_params=_cp(("parallel", "parallel")),
    )(proj, proj, proj, w, b, colscale)


def _split3(x):
    x1 = x.astype(BF16)
    r = x - x1.astype(F32)
    x2 = r.astype(BF16)
    x3 = (r - x2.astype(F32)).astype(BF16)
    return x1, x2, x3


def _log_sigmoid(x):
    return jnp.minimum(x, 0.0) - jnp.log(1.0 + jnp.exp(-jnp.abs(x)))


def _mlstm_dir(d, q_ref, k_ref, v_ref, gc_ref, gr_ref, bc_ref, br_ref, h_ref, c_scr, m_scr):
    L = ML_L
    ri = lax.broadcasted_iota(jnp.int32, (L, L), 0)
    ci = lax.broadcasted_iota(jnp.int32, (L, L), 1)
    mask = (ci <= ri) if d == 0 else (ci >= ri)
    tri = jnp.where(mask, 1.0, 0.0).astype(BF16)
    tri_t = jnp.where((ri <= ci) if d == 0 else (ri >= ci), 1.0, 0.0).astype(BF16)
    lane = lax.broadcasted_iota(jnp.int32, (L, ML_DP), 1)

    gc = gc_ref[...] + bc_ref[...]
    gr = gr_ref[...] + br_ref[...]
    bc_all = sum(jnp.dot(tri, p, preferred_element_type=F32) for p in _split3(_log_sigmoid(gc)))
    br_all = sum(jnp.dot(p, tri_t, preferred_element_type=F32) for p in _split3(_log_sigmoid(gr)))

    hs = range(ML_HEADS)
    cols = [slice(h * ML_DP, (h + 1) * ML_DP) for h in hs]
    q = [q_ref[:, cs] for cs in cols]
    k = [k_ref[:, cs] for cs in cols]
    v = [v_ref[:, cs] for cs in cols]
    m_prev = [m_scr[h, d][0:1, 0:1] for h in hs]
    ct = [c_scr[h, d] for h in hs]
    qk = [lax.dot_general(q[h], k[h], (((1,), (1,)), ((), ())), preferred_element_type=F32) for h in hs]
    inter = [jnp.dot(q[h], ct[h].astype(BF16), preferred_element_type=F32) for h in hs]

    rep = lambda col: jnp.broadcast_to(col, (L, LANE))
    wide = lambda x: pltpu.repeat(x, ML_DP // LANE, axis=1)
    b_rep = [rep(bc_all[:, 4 * h + 2 + d:4 * h + 3 + d]) for h in hs]
    li_rep = [rep(gc[:, 4 * h + d:4 * h + d + 1]) for h in hs]
    a, w_st, m_t = [], [], []
    for h in hs:
        b_row = br_all[4 * h + 2 + d:4 * h + 3 + d, :]
        li_row = gr[4 * h + d:4 * h + d + 1, :]
        dmat = jnp.where(mask, wide(b_rep[h]) - b_row + li_row, -jnp.inf)
        g = b_rep[h] + m_prev[h]
        m_t.append(jnp.maximum(g, jnp.max(dmat, axis=-1, keepdims=True)))
        w_st.append(jnp.exp(g - m_t[h]))
        a.append((qk[h] * jnp.exp(dmat - wide(m_t[h]))).astype(BF16))

    for h in hs:
        num = wide(w_st[h]) * inter[h] + jnp.dot(a[h], v[h], preferred_element_type=F32)
        den = rep(num[:, ML_DH:ML_DH + 1])
        inv = 1.0 / jnp.maximum(jnp.abs(den), jnp.exp(-m_t[h]))
        h_ref[:, cols[h]] = jnp.where(lane < ML_DH, num * wide(inv), 0.0).astype(h_ref.dtype)

    for h in hs:
        b_end = b_rep[h][L - 1:L, :] if d == 0 else b_rep[h][0:1, :]
        e_log = b_end - b_rep[h] + li_rep[h]
        m_new = jnp.maximum(b_end + m_prev[h], jnp.max(e_log, axis=0, keepdims=True))
        decay = jnp.exp(b_end + m_prev[h] - m_new)[:, 0:1]
        ek = (wide(jnp.exp(e_log - m_new)) * k[h].astype(F32)).astype(BF16)
        c_scr[h, d] = decay * ct[h] + lax.dot_general(ek, v[h], (((0,), (0,)), ((), ())),
                                                      preferred_element_type=F32)
        m_scr[h, d] = jnp.broadcast_to(m_new, m_scr.shape[2:])


def _mlstm_kernel(qf_ref, kf_ref, vf_ref, gcf_ref, grf_ref, qb_ref, kb_ref, vb_ref, gcb_ref, grb_ref,
                  bc_ref, br_ref, c0_ref, m0_ref, hf_ref, hb_ref, c1_ref, m1_ref, c_scr, m_scr):
    c = pl.program_id(1)

    @pl.when(c == 0)
    def _():
        c_scr[...] = c0_ref[...]
        m_scr[...] = m0_ref[...]

    _mlstm_dir(0, qf_ref, kf_ref, vf_ref, gcf_ref, grf_ref, bc_ref, br_ref, hf_ref, c_scr, m_scr)
    _mlstm_dir(1, qb_ref, kb_ref, vb_ref, gcb_ref, grb_ref, bc_ref, br_ref, hb_ref, c_scr, m_scr)

    @pl.when(c == pl.num_programs(1) - 1)
    def _():
        c1_ref[...] = c_scr[...]
        m1_ref[...] = m_scr[...]


def _mlstm(qkv, gates, gates_row, bias_col, bias_row, c0, m0, batch):
    m = qkv.shape[0]
    s = m // batch
    nc = s // ML_L
    fwd = lambda b, c: b * nc + c
    bwd = lambda b, c: b * nc + (nc - 1 - c)

    def dir_specs(rf):
        return [pl.BlockSpec((ML_L, ML_WP), lambda b, c: (rf(b, c), 0)),
                pl.BlockSpec((ML_L, ML_WP), lambda b, c: (rf(b, c), 1)),
                pl.BlockSpec((ML_L, ML_WP), lambda b, c: (rf(b, c), 2)),
                pl.BlockSpec((ML_L, C_GATES), lambda b, c: (rf(b, c), 0)),
                pl.BlockSpec((4 * ML_HEADS, ML_L), lambda b, c: (0, rf(b, c)))]

    st_c = pl.BlockSpec((None, ML_HEADS, 2, ML_DP, ML_DP), lambda b, c: (b, 0, 0, 0, 0))
    st_m = pl.BlockSpec((None, ML_HEADS, 2, 8, LANE), lambda b, c: (b, 0, 0, 0, 0))
    return pl.pallas_call(
        _mlstm_kernel,
        grid=(batch, nc),
        in_specs=dir_specs(fwd) + dir_specs(bwd)
        + [pl.BlockSpec((1, C_GATES), lambda b, c: (0, 0)),
           pl.BlockSpec((4 * ML_HEADS, 1), lambda b, c: (0, 0)), st_c, st_m],
        out_specs=[pl.BlockSpec((ML_L, ML_WP), lambda b, c: (fwd(b, c), 0)),
                   pl.BlockSpec((ML_L, ML_WP), lambda b, c: (bwd(b, c), 0)), st_c, st_m],
        out_shape=[jax.ShapeDtypeStruct((m, ML_WP), BF16), jax.ShapeDtypeStruct((m, ML_WP), BF16),
                   jax.ShapeDtypeStruct(c0.shape, F32), jax.ShapeDtypeStruct(m0.shape, F32)],
        scratch_shapes=[pltpu.VMEM((ML_HEADS, 2, ML_DP, ML_DP), F32), pltpu.VMEM((ML_HEADS, 2, 8, LANE), F32)],
        compiler_params=_cp(("parallel", "arbitrary")),
    )(qkv, qkv, qkv, gates, gates_row, qkv, qkv, qkv, gates, gates_row, bias_col, bias_row, c0, m0)


def _outproj_kernel(mla_ref, hf_ref, hb_ref, og_ref, na_ref, w1_ref, w2_ref, w3_ref, res_ref, gate_ref,
                    out_ref, ml_scr):
    @pl.when(pl.program_id(1) == 0)
    def _():
        og = _sigmoid(og_ref[...].astype(F32))
        for h in range(ML_HEADS):
            cs = slice(h * ML_DP, (h + 1) * ML_DP)
            o_h = jnp.concatenate([og[:, h * ML_DH:(h + 1) * ML_DH],
                                   jnp.zeros((og.shape[0], ML_DP - ML_DH), F32)], axis=1)
            ml_scr[:, cs] = ((hf_ref[:, cs].astype(F32) + hb_ref[:, cs].astype(F32)) * o_h).astype(BF16)

    acc = (jnp.dot(mla_ref[...], w1_ref[...], preferred_element_type=F32)
           + jnp.dot(ml_scr[...], w2_ref[...], preferred_element_type=F32)
           + jnp.dot(na_ref[...], w3_ref[...], preferred_element_type=F32))
    out_ref[...] = res_ref[...] + gate_ref[...] * acc


def _outproj(mla, hf, hb, proj, na, w_out, w2, l, res, mod4, gate_chunk, rowf):
    m, d = res.shape
    tm = min(TM, m)
    tn = d
    rows = lambda width, col: pl.BlockSpec((tm, width), lambda i, j: (i, col))
    wrows = lambda nrows, blk: pl.BlockSpec((None, nrows, tn), lambda i, j: (l, blk, j),
                                            pipeline_mode=pl.Buffered(1))
    return pl.pallas_call(
        _outproj_kernel,
        grid=(m // tm, d // tn),
        in_specs=[rows(MLA_W, 0), rows(ML_WP, 0), rows(ML_WP, 0), rows(ML_W, C_MLO // ML_W), rows(NA_W, 0),
                  wrows(MLA_W, 0), wrows(ML_WP, 0), wrows(NA_W, (MLA_W + ML_W) // NA_W),
                  pl.BlockSpec((tm, tn), lambda i, j: (i, j)), _mod_spec(gate_chunk, rowf, tm, tn)],
        out_specs=pl.BlockSpec((tm, tn), lambda i, j: (i, j)),
        out_shape=jax.ShapeDtypeStruct((m, d), F32),
        scratch_shapes=[pltpu.VMEM((tm, ML_WP), BF16)],
        compiler_params=_cp(("parallel", "arbitrary")),
    )(mla, hf, hb, proj, na, w_out, w2, w_out, res, mod4)


def _ffn_kernel(x_ref, xp_ref, xn_ref, g_ref, sc_ref, sh_ref, wg_ref, wv_ref, cwg_ref, cwv_ref, cbg_ref, cbv_ref,
                wd_ref, gate_ref, *rest, tiles_per_seg, final):
    if final:
        fg_ref, o_ref, hn_scr = rest
    else:
        o_ref, hn_scr = rest
    i = pl.program_id(0)
    j = pl.program_id(1)
    tm = x_ref.shape[0]
    te = hn_scr.shape[0]

    @pl.when(j == 0)
    def _():
        nm = lambda x: (_rms(x, g_ref[...]) * (1.0 + sc_ref[...]) + sh_ref[...]).astype(BF16)
        hn_scr[0:HALO, :] = nm(xp_ref[...])
        hn_scr[HALO:HALO + tm, :] = nm(x_ref[...])
        hn_scr[HALO + tm:te, :] = nm(xn_ref[...])
        o_ref[...] = jnp.zeros_like(o_ref)

    t = i % tiles_per_seg
    keep_prev = jnp.where(t == 0, 0.0, 1.0)
    keep_next = jnp.where(t == tiles_per_seg - 1, 0.0, 1.0)
    hn = hn_scr[...]

    def conv(w_ref, cw_ref, cb_ref):
        u = jnp.dot(hn, w_ref[...], preferred_element_type=F32)
        rid = lax.broadcasted_iota(jnp.int32, (tm, u.shape[1]), 0)
        up = pltpu.roll(u, 1, axis=0)[HALO:HALO + tm]
        un = pltpu.roll(u, te - 1, axis=0)[HALO:HALO + tm]
        up = jnp.where(rid == 0, up * keep_prev, up)
        un = jnp.where(rid == tm - 1, un * keep_next, un)
        cw = cw_ref[...]
        return up * cw[0:1, :] + u[HALO:HALO + tm] * cw[1:2, :] + un * cw[2:3, :] + cb_ref[...]

    gg = conv(wg_ref, cwg_ref, cbg_ref)
    vv = conv(wv_ref, cwv_ref, cbv_ref)
    act = (gg * _sigmoid(gg) * vv).astype(BF16)
    o_ref[...] += jnp.dot(act, wd_ref[...], preferred_element_type=F32)

    @pl.when(j == pl.num_programs(1) - 1)
    def _():
        y = x_ref[...] + gate_ref[...] * o_ref[...]
        o_ref[...] = _rms(y, fg_ref[...]) if final else y


def _ffn(x, g, mod4, rowf, w_up, cw, cb, w_down, l, seg, final_g=None, tf=512):
    m, d = x.shape
    dff = w_down.shape[1]
    tm = min(TM_BIG, seg)
    nj = dff // tf
    hb = tm // HALO
    last = m // HALO - 1
    final = final_g is not None
    vec = lambda r, off: pl.BlockSpec((None, r, tf), lambda i, j: (l, 0, off + j))
    once = pl.Buffered(1)
    in_specs = [pl.BlockSpec((tm, d), lambda i, j: (i, 0)),
                pl.BlockSpec((HALO, d), lambda i, j: (jnp.maximum(i * hb - 1, 0), 0)),
                pl.BlockSpec((HALO, d), lambda i, j: (jnp.minimum((i + 1) * hb, last), 0)),
                pl.BlockSpec((1, d), lambda i, j: (0, 0)),
                _mod_spec(4, rowf, tm), _mod_spec(3, rowf, tm),
                pl.BlockSpec((None, d, tf), lambda i, j: (l, 0, j)),
                pl.BlockSpec((None, d, tf), lambda i, j: (l, 0, nj + j)),
                vec(3, 0), vec(3, nj), vec(1, 0), vec(1, nj),
                pl.BlockSpec((None, tf, d), lambda i, j: (l, j, 0)), _mod_spec(5, rowf, tm)]
    args = [x, x, x, g.reshape(1, d), mod4, mod4, w_up, w_up, cw, cw, cb, cb, w_down, mod4]
    if final:
        in_specs.append(pl.BlockSpec((1, d), lambda i, j: (0, 0)))
        args.append(final_g.reshape(1, d))
    return pl.pallas_call(
        functools.partial(_ffn_kernel, tiles_per_seg=seg // tm, final=final),
        grid=(m // tm, nj),
        in_specs=in_specs,
        out_specs=pl.BlockSpec((tm, d), lambda i, j: (i, 0), pipeline_mode=once),
        out_shape=jax.ShapeDtypeStruct((m, d), F32),
        scratch_shapes=[pltpu.VMEM((tm + 2 * HALO, d), BF16)],
        compiler_params=_cp(("parallel", "arbitrary")),
    )(*args)


def _pad_heads(a, heads, dh, dp, axis):
    shp = a.shape
    a = a.reshape(shp[:axis] + (heads, dh) + shp[axis + 1:])
    pad = [(0, 0)] * a.ndim
    pad[axis + 1] = (0, dp - dh)
    a = jnp.pad(a, pad)
    return a.reshape(shp[:axis] + (heads * dp,) + shp[axis + 1:])


def _rope_cols(a):
    ev, od = a[..., 0::2], a[..., 1::2]
    return jnp.concatenate([ev, od, od, ev], axis=-1)


def _prep_w_in(w):
    w = w.astype(BF16)
    lead = w.shape[:-1]
    o_rope, o_ml, o_gi, o_na = IN_OFFSETS[1], IN_OFFSETS[2], IN_OFFSETS[6], IN_OFFSETS[8]
    gi = w[..., o_gi:o_gi + 2 * ML_HEADS].reshape(lead + (2, ML_HEADS))
    gf = w[..., o_gi + 2 * ML_HEADS:o_na].reshape(lead + (2, ML_HEADS))
    g4 = jnp.stack([gi[..., 0, :], gi[..., 1, :], gf[..., 0, :], gf[..., 1, :]], axis=-1)
    gates = jnp.pad(g4.reshape(lead + (4 * ML_HEADS,)), [(0, 0)] * len(lead) + [(0, C_GATES - 4 * ML_HEADS)])
    out = jnp.concatenate([w[..., o_ml:o_gi], w[..., :o_rope], _rope_cols(w[..., o_rope:o_ml]), gates, w[..., o_na:]],
                          axis=-1)
    assert out.shape[-1] == C_IN
    return out


def _prep_w_uq(w):
    w = w.reshape(w.shape[0], MLA_HEADS, MLA_NOPE + MLA_ROPE)
    out = jnp.concatenate([w[..., :MLA_NOPE], _rope_cols(w[..., MLA_NOPE:])], axis=-1)
    return out.reshape(w.shape[0], MLA_HEADS * MLA_DQ).astype(BF16)


def _gate_bias(i_bias, f_bias):
    g4 = jnp.stack([i_bias[0], i_bias[1], f_bias[0], f_bias[1]], axis=-1).astype(F32).reshape(4 * ML_HEADS)
    return jnp.pad(g4, (0, LANE - 4 * ML_HEADS)).reshape(1, LANE), g4.reshape(4 * ML_HEADS, 1)


def _rope_table(n_tokens):
    t = jnp.arange(n_tokens)
    row = (t // GRID_W).astype(F32)
    col = (t % GRID_W).astype(F32)
    n_freq = MLA_ROPE // 4
    inv = ROPE_BASE ** (-jnp.arange(n_freq, dtype=F32) / n_freq)
    ang = jnp.concatenate([row[:, None] * inv, col[:, None] * inv], axis=-1)
    cos, sin = jnp.cos(ang), jnp.sin(ang)
    return jnp.concatenate([cos, cos, -sin, sin], axis=-1)


def kernel(x, c, ctx, c_ctx, ada_w, ada_b, norm1_g, norm2_g, w_in, mla_q_norm, mla_kv_norm, mla_w_uq, mla_w_uk, mla_w_uv, ml_conv_w, ml_conv_b, ml_i_bias, ml_f_bias, na_rpb, w_out, ffn_w_up, ffn_conv_w, ffn_conv_b, ffn_w_down, final_norm_g):
    batch, seq, d = x.shape
    tctx = ctx.shape[1]
    depth = ada_w.shape[0]
    assert d == D_MODEL and batch <= 7 and tctx == ML_L and tctx % HALO == 0
    assert seq % TM_BIG == 0 and seq % NA_BLK == 0 and seq // GRID_W >= NA_KR

    h = x.reshape(batch * seq, d)
    hc = ctx.reshape(batch * tctx, d)
    row_lat = lambda i, tm: i // (seq // tm)
    row_ctx = lambda i, tm: batch
    cond8 = jnp.zeros((8, d), F32).at[:batch].set(c).at[batch].set(c_ctx)

    tab_lat = _rope_table(seq)
    tab_ctx = jnp.concatenate([jnp.ones((tctx, MLA_ROPE), F32), jnp.zeros((tctx, MLA_ROPE), F32)], axis=-1)
    ml_scale = jnp.concatenate([jnp.ones((1, ML_W), F32), jnp.full((1, ML_W), ML_DH ** -0.5, F32)], axis=-1)

    w_in_p = _prep_w_in(w_in)
    w_out_p = w_out.astype(BF16)
    w2_p = _pad_heads(w_out_p[:, MLA_W:MLA_W + ML_W], ML_HEADS, ML_DH, ML_DP, 1)
    w_up_p = ffn_w_up.astype(BF16)
    w_down_p = ffn_w_down.astype(BF16)
    fcb = ffn_conv_b.reshape(depth, 1, -1)

    for l in range(depth):
        last = l == depth - 1
        mod4 = _ada(cond8, ada_w, ada_b, l).reshape(8, N_MOD, 1, d)

        wuq = _prep_w_uq(mla_w_uq[l])
        wuk = mla_w_uk[l].astype(BF16)
        wuv = mla_w_uv[l].astype(BF16)
        qg = mla_q_norm[l].reshape(1, -1)
        kvg = mla_kv_norm[l].reshape(1, -1)
        cw = ml_conv_w[l]
        cb = ml_conv_b[l].reshape(1, -1)
        bias_col, bias_row = _gate_bias(ml_i_bias[l], ml_f_bias[l])
        na_bias = _na_bias_table(na_rpb[l])

        def mixers_in(hh, rowf):
            proj, gates = _nm_mm(hh, norm1_g[l], mod4, 1, 0, rowf, w_in_p, l, TN_IN, C_GATE0, C_GATES)
            return proj, gates, gates[:, :4 * ML_HEADS].T

        proj_c, gates_c, grow_c = mixers_in(hc, row_ctx)
        proj_l, gates_l, grow_l = mixers_in(h, row_lat)

        q_c, k_c, v_c = _mla_up(proj_c, tab_ctx, qg, kvg, wuq, wuk, wuv)
        q_l, k_l, v_l = _mla_up(proj_l, tab_lat, qg, kvg, wuq, wuk, wuv)
        mla_lat = _flash(q_l, 0, k_c, 0, v_c, 0, MLA_HEADS, MLA_DQ, MLA_V, batch, lat=(k_l, 0, v_l, 0))

        qkv_c = _ml_prep(proj_c, cw, cb, ml_scale, tctx)
        qkv_l = _ml_prep(proj_l, cw, cb, ml_scale, seq)
        c0 = jnp.zeros((batch, ML_HEADS, 2, ML_DP, ML_DP), F32)
        m0 = jnp.zeros((batch, ML_HEADS, 2, 8, LANE), F32)
        hf_c, hb_c, c1, m1 = _mlstm(qkv_c, gates_c, grow_c, bias_col, bias_row, c0, m0, batch)
        hf_l, hb_l, _, _ = _mlstm(qkv_l, gates_l, grow_l, bias_col, bias_row, c1, m1, batch)

        na_lat = _na(proj_l, proj_c, na_bias, batch)

        h = _outproj(mla_lat, hf_l, hb_l, proj_l, na_lat, w_out_p, w2_p, l, h, mod4, 2, row_lat)
        h = _ffn(h, norm2_g[l], mod4, row_lat, w_up_p, ffn_conv_w, fcb, w_down_p, l, seq,
                 final_g=final_norm_g if last else None)

        if not last:
            mla_ctx = _flash(q_c, 0, k_c, 0, v_c, 0, MLA_HEADS, MLA_DQ, MLA_V, batch)
            na_ctx = _flash(proj_c, C_NAQ // NA_DH, proj_c, C_NAK // NA_DH, proj_c, C_NAV // NA_DH,
                            NA_HEADS, NA_DH, NA_DH, batch, scale=NA_DH ** -0.5 * LOG2E)
            hc = _outproj(mla_ctx, hf_c, hb_c, proj_c, na_ctx, w_out_p, w2_p, l, hc, mod4, 2, row_ctx)
            hc = _ffn(hc, norm2_g[l], mod4, row_ctx, w_up_p, ffn_conv_w, fcb, w_down_p, l, tctx)

    return h.reshape(batch, seq, d)
```

```python
import functools
import math

import numpy as np
import jax
import jax.numpy as jnp
from jax import lax
from jax.experimental import pallas as pl
from jax.experimental.pallas import tpu as pltpu

F32 = jnp.float32
BF16 = jnp.bfloat16

D_MODEL = 2048
GRID_W = 64
MLA_HEADS = 6
MLA_NOPE = 128
MLA_ROPE = 64
MLA_V = 128
MLA_Q_LORA = 512
MLA_KV_LORA = 256
ML_HEADS = 4
ML_DH = 192
NA_HEADS = 4
NA_DH = 128
NA_KR = 8
NA_KC = 16
ROPE_BASE = 10000.0
EPS = 1e-6
N_MOD = 6
ML_W = ML_HEADS * ML_DH
NA_W = NA_HEADS * NA_DH
MLA_W = MLA_HEADS * MLA_V
IN_SIZES = (MLA_Q_LORA, MLA_KV_LORA, MLA_ROPE, ML_W, ML_W, ML_W, ML_W, 2 * ML_HEADS, 2 * ML_HEADS, NA_W, NA_W, NA_W)
IN_OFFSETS = tuple(sum(IN_SIZES[:i + 1]) for i in range(len(IN_SIZES) - 1))
LOG2E = math.log2(math.e)

LANE = 128
ML_DP = 256
ML_WP = ML_HEADS * ML_DP
ML_L = 256
MLA_DQ = 256
TM = 512
TM_BIG = 1024
HALO = 16
VMEM_LIMIT = 58 * 1024 * 1024

C_MLQ = 0
C_MLK = C_MLQ + ML_W
C_MLV = C_MLK + ML_W
C_MLO = C_MLV + ML_W
C_QLAT = C_MLO + ML_W
C_KVLAT = C_QLAT + MLA_Q_LORA
C_KROPE = C_KVLAT + MLA_KV_LORA
C_GATE0 = C_KROPE + LANE
C_GATES = LANE
C_NAQ = C_GATE0 + C_GATES
C_NAK = C_NAQ + NA_W
C_NAV = C_NAK + NA_W
C_IN = C_NAV + NA_W
TN_IN = 512
ML_PAIR = 2 * ML_DH


def _cp(sem):
    return pltpu.CompilerParams(dimension_semantics=sem, vmem_limit_bytes=VMEM_LIMIT)


def _sigmoid(x):
    return 1.0 / (1.0 + jnp.exp(-x))


def _rms(x, g):
    return x * lax.rsqrt(jnp.mean(x * x, axis=-1, keepdims=True) + EPS) * g


def _mod_spec(chunk, rowf, tm, width=D_MODEL):
    return pl.BlockSpec((None, None, 1, width), lambda i, j: (rowf(i, tm), chunk, 0, j if width != D_MODEL else 0))


def _ada_kernel(c_ref, w_ref, b_ref, o_ref):
    c = c_ref[...]
    a = c * _sigmoid(c)
    o_ref[...] = jnp.dot(a, w_ref[...], preferred_element_type=F32,
                         precision=lax.Precision.HIGHEST) + b_ref[...]


def _ada(cond8, w, b, l):
    _, d, n = w.shape
    tn = 1024
    return pl.pallas_call(
        _ada_kernel,
        grid=(n // tn,),
        in_specs=[pl.BlockSpec((8, d), lambda j: (0, 0)),
                  pl.BlockSpec((None, d, tn), lambda j: (l, 0, j)),
                  pl.BlockSpec((None, 1, tn), lambda j: (l, 0, j))],
        out_specs=pl.BlockSpec((8, tn), lambda j: (0, j)),
        out_shape=jax.ShapeDtypeStruct((8, n), F32),
        compiler_params=_cp(("arbitrary",)),
    )(cond8, w, b.reshape(b.shape[0], 1, n))


def _nm_mm_kernel(x_ref, g_ref, sc_ref, sh_ref, w_ref, o_ref, t_ref, hn_ref, *, f32_col):
    j = pl.program_id(1)
    tn = w_ref.shape[1]

    @pl.when(j == 0)
    def _():
        y = _rms(x_ref[...], g_ref[...])
        hn_ref[...] = (y * (1.0 + sc_ref[...]) + sh_ref[...]).astype(BF16)

    acc = jnp.dot(hn_ref[...], w_ref[...], preferred_element_type=F32)
    o_ref[...] = acc.astype(o_ref.dtype)

    @pl.when(j == f32_col // tn)
    def _():
        t_ref[...] = acc[:, f32_col % tn:f32_col % tn + t_ref.shape[1]]


def _nm_mm(x, g, mod4, sc_chunk, sh_chunk, rowf, w, l, tn, f32_col, f32_cols):
    m, d = x.shape
    n = w.shape[2]
    tm = min(TM_BIG, m)
    return pl.pallas_call(
        functools.partial(_nm_mm_kernel, f32_col=f32_col),
        grid=(m // tm, n // tn),
        in_specs=[pl.BlockSpec((tm, d), lambda i, j: (i, 0)),
                  pl.BlockSpec((1, d), lambda i, j: (0, 0)),
                  _mod_spec(sc_chunk, rowf, tm), _mod_spec(sh_chunk, rowf, tm),
                  pl.BlockSpec((None, d, tn), lambda i, j: (l, 0, j))],
        out_specs=[pl.BlockSpec((tm, tn), lambda i, j: (i, j)),
                   pl.BlockSpec((tm, f32_cols), lambda i, j: (i, 0))],
        out_shape=[jax.ShapeDtypeStruct((m, n), BF16), jax.ShapeDtypeStruct((m, f32_cols), F32)],
        scratch_shapes=[pltpu.VMEM((tm, d), BF16)],
        compiler_params=_cp(("parallel", "arbitrary")),
    )(x, g.reshape(1, d), mod4, mod4, w)


def _mla_up_kernel(ql_ref, kvl_ref, kr_ref, tab_ref, qg_ref, kvg_ref, wuq_ref, wuk_ref, wuv_ref,
                   q_ref, k_ref, v_ref, *, scale):
    tab = tab_ref[...]
    qn = _rms(ql_ref[...].astype(F32), qg_ref[...]).astype(BF16)
    qf = jnp.dot(qn, wuq_ref[...], preferred_element_type=F32)
    for h in range(MLA_HEADS):
        c0 = h * MLA_DQ
        y = qf[:, c0 + MLA_NOPE:c0 + MLA_DQ] * tab
        z = y + pltpu.roll(y, MLA_ROPE, axis=1)
        q_ref[:, c0:c0 + MLA_NOPE] = (qf[:, c0:c0 + MLA_NOPE] * scale).astype(BF16)
        q_ref[:, c0 + MLA_NOPE:c0 + MLA_DQ] = (z * scale).astype(BF16)
    cn = _rms(kvl_ref[...].astype(F32), kvg_ref[...]).astype(BF16)
    kn = jnp.dot(cn, wuk_ref[...], preferred_element_type=F32)
    v_ref[...] = jnp.dot(cn, wuv_ref[...], preferred_element_type=F32).astype(BF16)
    ky = kr_ref[...].astype(F32) * tab
    kz = ky + pltpu.roll(ky, MLA_ROPE, axis=1)
    lane = lax.broadcasted_iota(jnp.int32, kz.shape, 1)
    kz = jnp.where(lane < MLA_ROPE, kz, 0.0).astype(BF16)
    for h in range(MLA_HEADS):
        c0 = h * MLA_DQ
        k_ref[:, c0:c0 + MLA_NOPE] = kn[:, h * MLA_NOPE:(h + 1) * MLA_NOPE].astype(BF16)
        k_ref[:, c0 + MLA_NOPE:c0 + MLA_DQ] = kz


def _mla_up(proj, tab, qg, kvg, wuq, wuk, wuv):
    m = proj.shape[0]
    tm = min(TM, tab.shape[0])
    nt = tab.shape[0] // tm
    full = lambda a: pl.BlockSpec(a.shape, lambda i: (0, 0))
    return pl.pallas_call(
        functools.partial(_mla_up_kernel, scale=(MLA_NOPE + MLA_ROPE) ** -0.5 * LOG2E),
        grid=(m // tm,),
        in_specs=[pl.BlockSpec((tm, MLA_Q_LORA), lambda i: (i, C_QLAT // MLA_Q_LORA)),
                  pl.BlockSpec((tm, MLA_KV_LORA), lambda i: (i, C_KVLAT // MLA_KV_LORA)),
                  pl.BlockSpec((tm, LANE), lambda i: (i, C_KROPE // LANE)),
                  pl.BlockSpec((tm, LANE), lambda i: (i % nt, 0)),
                  full(qg), full(kvg), full(wuq), full(wuk), full(wuv)],
        out_specs=[pl.BlockSpec((tm, MLA_HEADS * MLA_DQ), lambda i: (i, 0)),
                   pl.BlockSpec((tm, MLA_HEADS * MLA_DQ), lambda i: (i, 0)),
                   pl.BlockSpec((tm, MLA_W), lambda i: (i, 0))],
        out_shape=[jax.ShapeDtypeStruct((m, MLA_HEADS * MLA_DQ), BF16),
                   jax.ShapeDtypeStruct((m, MLA_HEADS * MLA_DQ), BF16),
                   jax.ShapeDtypeStruct((m, MLA_W), BF16)],
        compiler_params=_cp(("parallel",)),
    )(proj, proj, proj, tab, qg, kvg, wuq, wuk, wuv)


def _flash_kernel(*refs, has_lat, scale, nsub, tk):
    if has_lat:
        q_ref, kc_ref, vc_ref, kl_ref, vl_ref, o_ref, m_scr, acc_scr = refs
    else:
        q_ref, kc_ref, vc_ref, o_ref, m_scr, acc_scr = refs
    ts = q_ref.shape[0] // nsub
    dv = vc_ref.shape[1]
    dn = (((1,), (1,)), ((), ()))

    def scores(q, k):
        s = lax.dot_general(q, k, dn, preferred_element_type=F32)
        return s if scale == 1.0 else s * scale

    def with_ones(v):
        lane = lax.broadcasted_iota(jnp.int32, v.shape, 1)
        return jnp.concatenate([v, jnp.where(lane == 0, 1.0, 0.0).astype(v.dtype)], axis=1)

    def lanes(x, n):
        return x if n == LANE else pltpu.repeat(x, n // LANE, axis=1)

    def probs(s, m):
        return jnp.exp2(s - lanes(m, s.shape[1])).astype(BF16)

    kc = kc_ref[...]
    vc = with_ones(vc_ref[...])
    for u in range(nsub):
        s = scores(q_ref[u * ts:(u + 1) * ts, :], kc)
        m = jnp.broadcast_to(jnp.max(s, axis=-1, keepdims=True), (ts, LANE))
        m_scr[u] = m
        acc_scr[u] = jnp.dot(probs(s, m), vc, preferred_element_type=F32)

    if has_lat:
        def body(kb, carry):
            off = pl.multiple_of(kb * tk, tk)
            k = kl_ref[pl.ds(off, tk), :]
            v = with_ones(vl_ref[pl.ds(off, tk), :])
            s_next = scores(q_ref[0:ts, :], k)
            for u in range(nsub):
                s = s_next
                if u + 1 < nsub:
                    s_next = scores(q_ref[(u + 1) * ts:(u + 2) * ts, :], k)
                m_prev = m_scr[u]
                m_new = jnp.maximum(m_prev, jnp.max(s, axis=-1, keepdims=True))
                m_scr[u] = m_new
                acc_scr[u] = (lanes(jnp.exp2(m_prev - m_new), 2 * dv) * acc_scr[u]
                              + jnp.dot(probs(s, m_new), v, preferred_element_type=F32))
            return carry

        lax.fori_loop(0, kl_ref.shape[0] // tk, body, 0)

    for u in range(nsub):
        acc = acc_scr[u]
        o_ref[u * ts:(u + 1) * ts, :] = (acc[:, :dv] / acc[:, dv:dv + 1]).astype(o_ref.dtype)


def _flash(q, qcol, kc, kccol, vc, vccol, heads, dq, dv, batch, lat=None, scale=1.0, tq=2048, nsub=2, tk=2048):
    mq = q.shape[0]
    sq = mq // batch
    tq = min(tq, sq)
    nsub = min(nsub, tq // 256)
    nq = sq // tq
    tc = kc.shape[0] // batch
    has_lat = lat is not None
    in_specs = [pl.BlockSpec((tq, dq), lambda b, h, i: (b * nq + i, qcol + h)),
                pl.BlockSpec((tc, dq), lambda b, h, i: (b, kccol + h)),
                pl.BlockSpec((tc, dv), lambda b, h, i: (b, vccol + h))]
    args = [q, kc, vc]
    if has_lat:
        kl, klcol, vl, vlcol = lat
        sk = kl.shape[0] // batch
        tk = min(tk, sk)
        in_specs += [pl.BlockSpec((sk, dq), lambda b, h, i: (b, klcol + h)),
                     pl.BlockSpec((sk, dv), lambda b, h, i: (b, vlcol + h))]
        args += [kl, vl]
    return pl.pallas_call(
        functools.partial(_flash_kernel, has_lat=has_lat, scale=scale, nsub=nsub, tk=tk),
        grid=(batch, heads, nq),
        in_specs=in_specs,
        out_specs=pl.BlockSpec((tq, dv), lambda b, h, i: (b * nq + i, h)),
        out_shape=jax.ShapeDtypeStruct((mq, heads * dv), BF16),
        scratch_shapes=[pltpu.VMEM((nsub, tq // nsub, LANE), F32), pltpu.VMEM((nsub, tq // nsub, 2 * dv), F32)],
        compiler_params=_cp(("parallel", "parallel", "arbitrary")),
    )(*args)


NA_RB = 8
NA_BLK = NA_RB * GRID_W
NA_GROUP = 2


def _na_kernel(q_ref, kp_ref, kc_ref, kn_ref, vp_ref, vc_ref, vn_ref, kx_ref, vx_ref, bias_ref, o_ref,
               kw_scr, vw_scr, *, rows, scale):
    j = pl.program_id(1)
    for t, (kr, vr) in enumerate(((kp_ref, vp_ref), (kc_ref, vc_ref), (kn_ref, vn_ref))):
        kw_scr[t * NA_BLK:(t + 1) * NA_BLK, :] = kr[...]
        vw_scr[t * NA_BLK:(t + 1) * NA_BLK, :] = vr[...]
    nwin = NA_KR * GRID_W
    dn = (((1,), (1,)), ((), ()))
    heads = [slice(h * NA_DH, (h + 1) * NA_DH) for h in range(NA_HEADS)]
    for i0 in range(0, NA_RB, NA_GROUP):
        units = []
        for i in range(i0, i0 + NA_GROUP):
            r = NA_RB * j + i
            r0 = jnp.clip(r - NA_KR // 2, 0, rows - NA_KR)
            start = pl.multiple_of((r0 - NA_RB * j + NA_RB) * GRID_W, GRID_W)
            for h, cs in enumerate(heads):
                q = q_ref[i * GRID_W:(i + 1) * GRID_W, cs]
                s_loc = (lax.dot_general(q, kw_scr[pl.ds(start, nwin), cs], dn, preferred_element_type=F32) * scale
                         + bias_ref[h, r - r0])
                s_ctx = lax.dot_general(q, kx_ref[:, cs], dn, preferred_element_type=F32) * scale
                units.append((i, cs, start, s_loc, s_ctx))
        outs = {}
        for i, cs, start, s_loc, s_ctx in units:
            m = jnp.maximum(jnp.max(s_loc, axis=-1, keepdims=True), jnp.max(s_ctx, axis=-1, keepdims=True))
            p_loc = jnp.exp(s_loc - m)
            p_ctx = jnp.exp(s_ctx - m)
            l = jnp.sum(p_loc, axis=-1, keepdims=True) + jnp.sum(p_ctx, axis=-1, keepdims=True)
            o = (jnp.dot(p_loc.astype(BF16), vw_scr[pl.ds(start, nwin), cs], preferred_element_type=F32)
                 + jnp.dot(p_ctx.astype(BF16), vx_ref[:, cs], preferred_element_type=F32))
            outs.setdefault(i, []).append((o / l).astype(o_ref.dtype))
        for i, parts in outs.items():
            o_ref[i * GRID_W:(i + 1) * GRID_W, :] = jnp.concatenate(parts, axis=1)


def _na_bias_table(rpb):
    c = np.arange(GRID_W)
    cs = np.clip(c - NA_KC // 2, 0, GRID_W - NA_KC)
    valid = (c[None, :] >= cs[:, None]) & (c[None, :] < cs[:, None] + NA_KC)
    dc = np.clip(c[None, :] - c[:, None] + (NA_KC - 1), 0, 2 * NA_KC - 2)
    dr = np.arange(NA_KR)[None, :] - np.arange(NA_KR)[:, None] + (NA_KR - 1)
    sel_r = (dr[:, :, None] == np.arange(2 * NA_KR - 1)).astype(np.float32)
    sel_c = ((dc[:, :, None] == np.arange(2 * NA_KC - 1)) & valid[:, :, None]).astype(np.float32)
    tab = jnp.einsum('hab,via,cdb->hvcid', rpb.astype(F32), sel_r, sel_c, precision=lax.Precision.HIGHEST)
    tab = tab + jnp.where(valid[None, None, :, None, :], 0.0, -1e30)
    return tab.reshape(NA_HEADS, NA_KR, GRID_W, NA_KR * GRID_W)


def _na(proj_l, proj_c, bias, batch):
    m = proj_l.shape[0]
    s = m // batch
    rows = s // GRID_W
    nb = rows // NA_RB
    cq, ck, cv = C_NAQ // NA_W, C_NAK // NA_W, C_NAV // NA_W
    tc = proj_c.shape[0] // batch
    prev = lambda b, j: b * nb + jnp.maximum(j - 1, 0)
    cur = lambda b, j: b * nb + j
    nxt = lambda b, j: b * nb + jnp.minimum(j + 1, nb - 1)
    blk = lambda rf, col: pl.BlockSpec((NA_BLK, NA_W), lambda b, j: (rf(b, j), col))
    return pl.pallas_call(
        functools.partial(_na_kernel, rows=rows, scale=NA_DH ** -0.5),
        grid=(batch, nb),
        in_specs=[blk(cur, cq), blk(prev, ck), blk(cur, ck), blk(nxt, ck),
                  blk(prev, cv), blk(cur, cv), blk(nxt, cv),
                  pl.BlockSpec((tc, NA_W), lambda b, j: (b, ck)),
                  pl.BlockSpec((tc, NA_W), lambda b, j: (b, cv)),
                  pl.BlockSpec(bias.shape, lambda b, j: (0, 0, 0, 0))],
        out_specs=pl.BlockSpec((NA_BLK, NA_W), lambda b, j: (b * nb + j, 0)),
        out_shape=jax.ShapeDtypeStruct((m, NA_W), BF16),
        scratch_shapes=[pltpu.VMEM((3 * NA_BLK, NA_W), BF16), pltpu.VMEM((3 * NA_BLK, NA_W), BF16)],
        compiler_params=_cp(("parallel", "arbitrary")),
    )(proj_l, proj_l, proj_l, proj_l, proj_l, proj_l, proj_l, proj_c, proj_c, bias)


def _ml_prep_kernel(x_ref, p_ref, n_ref, w_ref, b_ref, sc_ref, o_ref, *, tiles_per_seg, n_conv):
    j = pl.program_id(1)
    x = x_ref[...].astype(F32)
    tm = x.shape[0]

    def widen(y, fill):
        lane = lax.broadcasted_iota(jnp.int32, (tm, ML_DP - ML_DH), 1)
        spare = jnp.where(lane == 0, fill, 0.0)
        return jnp.concatenate([y[:, :ML_DH], spare, y[:, ML_DH:], spare], axis=1).astype(o_ref.dtype)

    @pl.when(j < n_conv)
    def _():
        t = pl.program_id(0) % tiles_per_seg
        prev_row = jnp.where(t == 0, 0.0, p_ref[HALO - 1:HALO, :].astype(F32))
        next_row = jnp.where(t == tiles_per_seg - 1, 0.0, n_ref[0:1, :].astype(F32))
        rid = lax.broadcasted_iota(jnp.int32, x.shape, 0)
        xp = jnp.where(rid == 0, prev_row, pltpu.roll(x, 1, axis=0))
        xn = jnp.where(rid == tm - 1, next_row, pltpu.roll(x, tm - 1, axis=0))
        w = w_ref[...]
        y = xp * w[0:1, :] + x * w[1:2, :] + xn * w[2:3, :] + b_ref[...]
        o_ref[...] = widen(y * _sigmoid(y) * sc_ref[...], 0.0)

    @pl.when(j >= n_conv)
    def _():
        o_ref[...] = widen(x, 1.0)


def _ml_prep(proj, w, b, colscale, seg):
    m = proj.shape[0]
    tm = min(TM_BIG, seg)
    hb = tm // HALO
    last = m // HALO - 1
    n_conv = w.shape[1] // ML_PAIR
    n_all = 3 * ML_W // ML_PAIR
    vec = lambda r: pl.BlockSpec((r, ML_PAIR), lambda i, j: (0, jnp.minimum(j, n_conv - 1)))
    return pl.pallas_call(
        functools.partial(_ml_prep_kernel, tiles_per_seg=seg // tm, n_conv=n_conv),
        grid=(m // tm, n_all),
        in_specs=[pl.BlockSpec((tm, ML_PAIR), lambda i, j: (i, j)),
                  pl.BlockSpec((HALO, ML_PAIR), lambda i, j: (jnp.maximum(i * hb - 1, 0), j)),
                  pl.BlockSpec((HALO, ML_PAIR), lambda i, j: (jnp.minimum((i + 1) * hb, last), j)),
                  vec(3), vec(1), vec(1)],
        out_specs=pl.BlockSpec((tm, 2 * ML_DP), lambda i, j: (i, j)),
        out_shape=jax.ShapeDtypeStruct((m, 3 * ML_WP), BF16),
        compiler_params=_cp(("parallel", "parallel")),
    )(proj, proj, proj, w, b, colscale)


def _split3(x):
    x1 = x.astype(BF16)
    r = x - x1.astype(F32)
    x2 = r.astype(BF16)
    x3 = (r - x2.astype(F32)).astype(BF16)
    return x1, x2, x3


def _log_sigmoid(x):
    return jnp.minimum(x, 0.0) - jnp.log(1.0 + jnp.exp(-jnp.abs(x)))


def _mlstm_dir(d, q_ref, k_ref, v_ref, gc_ref, gr_ref, bc_ref, br_ref, h_ref, c_scr, m_scr):
    L = ML_L
    ri = lax.broadcasted_iota(jnp.int32, (L, L), 0)
    ci = lax.broadcasted_iota(jnp.int32, (L, L), 1)
    mask = (ci <= ri) if d == 0 else (ci >= ri)
    tri = jnp.where(mask, 1.0, 0.0).astype(BF16)
    tri_t = jnp.where((ri <= ci) if d == 0 else (ri >= ci), 1.0, 0.0).astype(BF16)
    lane = lax.broadcasted_iota(jnp.int32, (L, ML_DP), 1)

    gc = gc_ref[...] + bc_ref[...]
    gr = gr_ref[...] + br_ref[...]
    bc_all = sum(jnp.dot(tri, p, preferred_element_type=F32) for p in _split3(_log_sigmoid(gc)))
    br_all = sum(jnp.dot(p, tri_t, preferred_element_type=F32) for p in _split3(_log_sigmoid(gr)))

    hs = range(ML_HEADS)
    cols = [slice(h * ML_DP, (h + 1) * ML_DP) for h in hs]
    q = [q_ref[:, cs] for cs in cols]
    k = [k_ref[:, cs] for cs in cols]
    v = [v_ref[:, cs] for cs in cols]
    m_prev = [m_scr[h, d][0:1, 0:1] for h in hs]
    ct = [c_scr[h, d] for h in hs]
    qk = [lax.dot_general(q[h], k[h], (((1,), (1,)), ((), ())), preferred_element_type=F32) for h in hs]
    inter = [jnp.dot(q[h], ct[h].astype(BF16), preferred_element_type=F32) for h in hs]

    rep = lambda col: jnp.broadcast_to(col, (L, LANE))
    wide = lambda x: pltpu.repeat(x, ML_DP // LANE, axis=1)
    b_rep = [rep(bc_all[:, 4 * h + 2 + d:4 * h + 3 + d]) for h in hs]
    li_rep = [rep(gc[:, 4 * h + d:4 * h + d + 1]) for h in hs]
    a, w_st, m_t = [], [], []
    for h in hs:
        b_row = br_all[4 * h + 2 + d:4 * h + 3 + d, :]
        li_row = gr[4 * h + d:4 * h + d + 1, :]
        dmat = jnp.where(mask, wide(b_rep[h]) - b_row + li_row, -jnp.inf)
        g = b_rep[h] + m_prev[h]
        m_t.append(jnp.maximum(g, jnp.max(dmat, axis=-1, keepdims=True)))
        w_st.append(jnp.exp(g - m_t[h]))
        a.append((qk[h] * jnp.exp(dmat - wide(m_t[h]))).astype(BF16))

    for h in hs:
        num = wide(w_st[h]) * inter[h] + jnp.dot(a[h], v[h], preferred_element_type=F32)
        den = rep(num[:, ML_DH:ML_DH + 1])
        inv = 1.0 / jnp.maximum(jnp.abs(den), jnp.exp(-m_t[h]))
        h_ref[:, cols[h]] = jnp.where(lane < ML_DH, num * wide(inv), 0.0).astype(h_ref.dtype)

    for h in hs:
        b_end = b_rep[h][L - 1:L, :] if d == 0 else b_rep[h][0:1, :]
        e_log = b_end - b_rep[h] + li_rep[h]
        m_new = jnp.maximum(b_end + m_prev[h], jnp.max(e_log, axis=0, keepdims=True))
        decay = jnp.exp(b_end + m_prev[h] - m_new)[:, 0:1]
        ek = (wide(jnp.exp(e_log - m_new)) * k[h].astype(F32)).astype(BF16)
        c_scr[h, d] = decay * ct[h] + lax.dot_general(ek, v[h], (((0,), (0,)), ((), ())),
                                                      preferred_element_type=F32)
        m_scr[h, d] = jnp.broadcast_to(m_new, m_scr.shape[2:])


def _mlstm_kernel(qf_ref, kf_ref, vf_ref, gcf_ref, grf_ref, qb_ref, kb_ref, vb_ref, gcb_ref, grb_ref,
                  bc_ref, br_ref, c0_ref, m0_ref, hf_ref, hb_ref, c1_ref, m1_ref, c_scr, m_scr):
    c = pl.program_id(1)

    @pl.when(c == 0)
    def _():
        c_scr[...] = c0_ref[...]
        m_scr[...] = m0_ref[...]

    _mlstm_dir(0, qf_ref, kf_ref, vf_ref, gcf_ref, grf_ref, bc_ref, br_ref, hf_ref, c_scr, m_scr)
    _mlstm_dir(1, qb_ref, kb_ref, vb_ref, gcb_ref, grb_ref, bc_ref, br_ref, hb_ref, c_scr, m_scr)

    @pl.when(c == pl.num_programs(1) - 1)
    def _():
        c1_ref[...] = c_scr[...]
        m1_ref[...] = m_scr[...]


def _mlstm(qkv, gates, gates_row, bias_col, bias_row, c0, m0, batch):
    m = qkv.shape[0]
    s = m // batch
    nc = s // ML_L
    fwd = lambda b, c: b * nc + c
    bwd = lambda b, c: b * nc + (nc - 1 - c)

    def dir_specs(rf):
        return [pl.BlockSpec((ML_L, ML_WP), lambda b, c: (rf(b, c), 0)),
                pl.BlockSpec((ML_L, ML_WP), lambda b, c: (rf(b, c), 1)),
                pl.BlockSpec((ML_L, ML_WP), lambda b, c: (rf(b, c), 2)),
                pl.BlockSpec((ML_L, C_GATES), lambda b, c: (rf(b, c), 0)),
                pl.BlockSpec((4 * ML_HEADS, ML_L), lambda b, c: (0, rf(b, c)))]

    st_c = pl.BlockSpec((None, ML_HEADS, 2, ML_DP, ML_DP), lambda b, c: (b, 0, 0, 0, 0))
    st_m = pl.BlockSpec((None, ML_HEADS, 2, 8, LANE), lambda b, c: (b, 0, 0, 0, 0))
    return pl.pallas_call(
        _mlstm_kernel,
        grid=(batch, nc),
        in_specs=dir_specs(fwd) + dir_specs(bwd)
        + [pl.BlockSpec((1, C_GATES), lambda b, c: (0, 0)),
           pl.BlockSpec((4 * ML_HEADS, 1), lambda b, c: (0, 0)), st_c, st_m],
        out_specs=[pl.BlockSpec((ML_L, ML_WP), lambda b, c: (fwd(b, c), 0)),
                   pl.BlockSpec((ML_L, ML_WP), lambda b, c: (bwd(b, c), 0)), st_c, st_m],
        out_shape=[jax.ShapeDtypeStruct((m, ML_WP), BF16), jax.ShapeDtypeStruct((m, ML_WP), BF16),
                   jax.ShapeDtypeStruct(c0.shape, F32), jax.ShapeDtypeStruct(m0.shape, F32)],
        scratch_shapes=[pltpu.VMEM((ML_HEADS, 2, ML_DP, ML_DP), F32), pltpu.VMEM((ML_HEADS, 2, 8, LANE), F32)],
        compiler_params=_cp(("parallel", "arbitrary")),
    )(qkv, qkv, qkv, gates, gates_row, qkv, qkv, qkv, gates, gates_row, bias_col, bias_row, c0, m0)


def _outproj_kernel(mla_ref, hf_ref, hb_ref, og_ref, na_ref, w1_ref, w2_ref, w3_ref, res_ref, gate_ref,
                    out_ref, ml_scr):
    @pl.when(pl.program_id(1) == 0)
    def _():
        og = _sigmoid(og_ref[...].astype(F32))
        for h in range(ML_HEADS):
            cs = slice(h * ML_DP, (h + 1) * ML_DP)
            o_h = jnp.concatenate([og[:, h * ML_DH:(h + 1) * ML_DH],
                                   jnp.zeros((og.shape[0], ML_DP - ML_DH), F32)], axis=1)
            ml_scr[:, cs] = ((hf_ref[:, cs].astype(F32) + hb_ref[:, cs].astype(F32)) * o_h).astype(BF16)

    acc = (jnp.dot(mla_ref[...], w1_ref[...], preferred_element_type=F32)
           + jnp.dot(ml_scr[...], w2_ref[...], preferred_element_type=F32)
           + jnp.dot(na_ref[...], w3_ref[...], preferred_element_type=F32))
    out_ref[...] = res_ref[...] + gate_ref[...] * acc


def _outproj(mla, hf, hb, proj, na, w_out, w2, l, res, mod4, gate_chunk, rowf):
    m, d = res.shape
    tm = min(TM, m)
    tn = d
    rows = lambda width, col: pl.BlockSpec((tm, width), lambda i, j: (i, col))
    wrows = lambda nrows, blk: pl.BlockSpec((None, nrows, tn), lambda i, j: (l, blk, j),
                                            pipeline_mode=pl.Buffered(1))
    return pl.pallas_call(
        _outproj_kernel,
        grid=(m // tm, d // tn),
        in_specs=[rows(MLA_W, 0), rows(ML_WP, 0), rows(ML_WP, 0), rows(ML_W, C_MLO // ML_W), rows(NA_W, 0),
                  wrows(MLA_W, 0), wrows(ML_WP, 0), wrows(NA_W, (MLA_W + ML_W) // NA_W),
                  pl.BlockSpec((tm, tn), lambda i, j: (i, j)), _mod_spec(gate_chunk, rowf, tm, tn)],
        out_specs=pl.BlockSpec((tm, tn), lambda i, j: (i, j)),
        out_shape=jax.ShapeDtypeStruct((m, d), F32),
        scratch_shapes=[pltpu.VMEM((tm, ML_WP), BF16)],
        compiler_params=_cp(("parallel", "arbitrary")),
    )(mla, hf, hb, proj, na, w_out, w2, w_out, res, mod4)


def _ffn_kernel(x_ref, xp_ref, xn_ref, g_ref, sc_ref, sh_ref, wg_ref, wv_ref, cwg_ref, cwv_ref, cbg_ref, cbv_ref,
                wd_ref, gate_ref, *rest, tiles_per_seg, final):
    if final:
        fg_ref, o_ref, hn_scr = rest
    else:
        o_ref, hn_scr = rest
    i = pl.program_id(0)
    j = pl.program_id(1)
    tm = x_ref.shape[0]
    te = hn_scr.shape[0]

    @pl.when(j == 0)
    def _():
        nm = lambda x: (_rms(x, g_ref[...]) * (1.0 + sc_ref[...]) + sh_ref[...]).astype(BF16)
        hn_scr[0:HALO, :] = nm(xp_ref[...])
        hn_scr[HALO:HALO + tm, :] = nm(x_ref[...])
        hn_scr[HALO + tm:te, :] = nm(xn_ref[...])
        o_ref[...] = jnp.zeros_like(o_ref)

    t = i % tiles_per_seg
    keep_prev = jnp.where(t == 0, 0.0, 1.0)
    keep_next = jnp.where(t == tiles_per_seg - 1, 0.0, 1.0)
    hn = hn_scr[...]

    def conv(w_ref, cw_ref, cb_ref):
        u = jnp.dot(hn, w_ref[...], preferred_element_type=F32)
        rid = lax.broadcasted_iota(jnp.int32, (tm, u.shape[1]), 0)
        up = pltpu.roll(u, 1, axis=0)[HALO:HALO + tm]
        un = pltpu.roll(u, te - 1, axis=0)[HALO:HALO + tm]
        up = jnp.where(rid == 0, up * keep_prev, up)
        un = jnp.where(rid == tm - 1, un * keep_next, un)
        cw = cw_ref[...]
        return up * cw[0:1, :] + u[HALO:HALO + tm] * cw[1:2, :] + un * cw[2:3, :] + cb_ref[...]

    gg = conv(wg_ref, cwg_ref, cbg_ref)
    vv = conv(wv_ref, cwv_ref, cbv_ref)
    act = (gg * _sigmoid(gg) * vv).astype(BF16)
    o_ref[...] += jnp.dot(act, wd_ref[...], preferred_element_type=F32)

    @pl.when(j == pl.num_programs(1) - 1)
    def _():
        y = x_ref[...] + gate_ref[...] * o_ref[...]
        o_ref[...] = _rms(y, fg_ref[...]) if final else y


def _ffn(x, g, mod4, rowf, w_up, cw, cb, w_down, l, seg, final_g=None, tf=512):
    m, d = x.shape
    dff = w_down.shape[1]
    tm = min(TM_BIG, seg)
    nj = dff // tf
    hb = tm // HALO
    last = m // HALO - 1
    final = final_g is not None
    vec = lambda r, off: pl.BlockSpec((None, r, tf), lambda i, j: (l, 0, off + j))
    once = pl.Buffered(1)
    in_specs = [pl.BlockSpec((tm, d), lambda i, j: (i, 0)),
                pl.BlockSpec((HALO, d), lambda i, j: (jnp.maximum(i * hb - 1, 0), 0)),
                pl.BlockSpec((HALO, d), lambda i, j: (jnp.minimum((i + 1) * hb, last), 0)),
                pl.BlockSpec((1, d), lambda i, j: (0, 0)),
                _mod_spec(4, rowf, tm), _mod_spec(3, rowf, tm),
                pl.BlockSpec((None, d, tf), lambda i, j: (l, 0, j)),
                pl.BlockSpec((None, d, tf), lambda i, j: (l, 0, nj + j)),
                vec(3, 0), vec(3, nj), vec(1, 0), vec(1, nj),
                pl.BlockSpec((None, tf, d), lambda i, j: (l, j, 0)), _mod_spec(5, rowf, tm)]
    args = [x, x, x, g.reshape(1, d), mod4, mod4, w_up, w_up, cw, cw, cb, cb, w_down, mod4]
    if final:
        in_specs.append(pl.BlockSpec((1, d), lambda i, j: (0, 0)))
        args.append(final_g.reshape(1, d))
    return pl.pallas_call(
        functools.partial(_ffn_kernel, tiles_per_seg=seg // tm, final=final),
        grid=(m // tm, nj),
        in_specs=in_specs,
        out_specs=pl.BlockSpec((tm, d), lambda i, j: (i, 0), pipeline_mode=once),
        out_shape=jax.ShapeDtypeStruct((m, d), F32),
        scratch_shapes=[pltpu.VMEM((tm + 2 * HALO, d), BF16)],
        compiler_params=_cp(("parallel", "arbitrary")),
    )(*args)


def _pad_heads(a, heads, dh, dp, axis):
    shp = a.shape
    a = a.reshape(shp[:axis] + (heads, dh) + shp[axis + 1:])
    pad = [(0, 0)] * a.ndim
    pad[axis + 1] = (0, dp - dh)
    a = jnp.pad(a, pad)
    return a.reshape(shp[:axis] + (heads * dp,) + shp[axis + 1:])


def _rope_cols(a):
    ev, od = a[..., 0::2], a[..., 1::2]
    return jnp.concatenate([ev, od, od, ev], axis=-1)


def _prep_w_in(w):
    w = w.astype(BF16)
    lead = w.shape[:-1]
    o_rope, o_ml, o_gi, o_na = IN_OFFSETS[1], IN_OFFSETS[2], IN_OFFSETS[6], IN_OFFSETS[8]
    gi = w[..., o_gi:o_gi + 2 * ML_HEADS].reshape(lead + (2, ML_HEADS))
    gf = w[..., o_gi + 2 * ML_HEADS:o_na].reshape(lead + (2, ML_HEADS))
    g4 = jnp.stack([gi[..., 0, :], gi[..., 1, :], gf[..., 0, :], gf[..., 1, :]], axis=-1)
    gates = jnp.pad(g4.reshape(lead + (4 * ML_HEADS,)), [(0, 0)] * len(lead) + [(0, C_GATES - 4 * ML_HEADS)])
    out = jnp.concatenate([w[..., o_ml:o_gi], w[..., :o_rope], _rope_cols(w[..., o_rope:o_ml]), gates, w[..., o_na:]],
                          axis=-1)
    assert out.shape[-1] == C_IN
    return out


def _prep_w_uq(w):
    w = w.reshape(w.shape[0], MLA_HEADS, MLA_NOPE + MLA_ROPE)
    out = jnp.concatenate([w[..., :MLA_NOPE], _rope_cols(w[..., MLA_NOPE:])], axis=-1)
    return out.reshape(w.shape[0], MLA_HEADS * MLA_DQ).astype(BF16)


def _gate_bias(i_bias, f_bias):
    g4 = jnp.stack([i_bias[0], i_bias[1], f_bias[0], f_bias[1]], axis=-1).astype(F32).reshape(4 * ML_HEADS)
    return jnp.pad(g4, (0, LANE - 4 * ML_HEADS)).reshape(1, LANE), g4.reshape(4 * ML_HEADS, 1)


def _rope_table(n_tokens):
    t = jnp.arange(n_tokens)
    row = (t // GRID_W).astype(F32)
    col = (t % GRID_W).astype(F32)
    n_freq = MLA_ROPE // 4
    inv = ROPE_BASE ** (-jnp.arange(n_freq, dtype=F32) / n_freq)
    ang = jnp.concatenate([row[:, None] * inv, col[:, None] * inv], axis=-1)
    cos, sin = jnp.cos(ang), jnp.sin(ang)
    return jnp.concatenate([cos, cos, -sin, sin], axis=-1)


def kernel(x, c, ctx, c_ctx, ada_w, ada_b, norm1_g, norm2_g, w_in, mla_q_norm, mla_kv_norm, mla_w_uq, mla_w_uk, mla_w_uv, ml_conv_w, ml_conv_b, ml_i_bias, ml_f_bias, na_rpb, w_out, ffn_w_up, ffn_conv_w, ffn_conv_b, ffn_w_down, final_norm_g):
    batch, seq, d = x.shape
    tctx = ctx.shape[1]
    depth = ada_w.shape[0]
    assert d == D_MODEL and batch <= 7 and tctx == ML_L and tctx % HALO == 0
    assert seq % TM_BIG == 0 and seq % NA_BLK == 0 and seq // GRID_W >= NA_KR

    h = x.reshape(batch * seq, d)
    hc = ctx.reshape(batch * tctx, d)
    row_lat = lambda i, tm: i // (seq // tm)
    row_ctx = lambda i, tm: batch
    cond8 = jnp.zeros((8, d), F32).at[:batch].set(c).at[batch].set(c_ctx)

    tab_lat = _rope_table(seq)
    tab_ctx = jnp.concatenate([jnp.ones((tctx, MLA_ROPE), F32), jnp.zeros((tctx, MLA_ROPE), F32)], axis=-1)
    ml_scale = jnp.concatenate([jnp.ones((1, ML_W), F32), jnp.full((1, ML_W), ML_DH ** -0.5, F32)], axis=-1)

    w_in_p = _prep_w_in(w_in)
    w_out_p = w_out.astype(BF16)
    w2_p = _pad_heads(w_out_p[:, MLA_W:MLA_W + ML_W], ML_HEADS, ML_DH, ML_DP, 1)
    w_up_p = ffn_w_up.astype(BF16)
    w_down_p = ffn_w_down.astype(BF16)
    fcb = ffn_conv_b.reshape(depth, 1, -1)

    for l in range(depth):
        last = l == depth - 1
        mod4 = _ada(cond8, ada_w, ada_b, l).reshape(8, N_MOD, 1, d)

        wuq = _prep_w_uq(mla_w_uq[l])
        wuk = mla_w_uk[l].astype(BF16)
        wuv = mla_w_uv[l].astype(BF16)
        qg = mla_q_norm[l].reshape(1, -1)
        kvg = mla_kv_norm[l].reshape(1, -1)
        cw = ml_conv_w[l]
        cb = ml_conv_b[l].reshape(1, -1)
        bias_col, bias_row = _gate_bias(ml_i_bias[l], ml_f_bias[l])
        na_bias = _na_bias_table(na_rpb[l])

        def mixers_in(hh, rowf):
            proj, gates = _nm_mm(hh, norm1_g[l], mod4, 1, 0, rowf, w_in_p, l, TN_IN, C_GATE0, C_GATES)
            return proj, gates, gates[:, :4 * ML_HEADS].T

        proj_c, gates_c, grow_c = mixers_in(hc, row_ctx)
        proj_l, gates_l, grow_l = mixers_in(h, row_lat)

        q_c, k_c, v_c = _mla_up(proj_c, tab_ctx, qg, kvg, wuq, wuk, wuv)
        q_l, k_l, v_l = _mla_up(proj_l, tab_lat, qg, kvg, wuq, wuk, wuv)
        mla_lat = _flash(q_l, 0, k_c, 0, v_c, 0, MLA_HEADS, MLA_DQ, MLA_V, batch, lat=(k_l, 0, v_l, 0))

        qkv_c = _ml_prep(proj_c, cw, cb, ml_scale, tctx)
        qkv_l = _ml_prep(proj_l, cw, cb, ml_scale, seq)
        c0 = jnp.zeros((batch, ML_HEADS, 2, ML_DP, ML_DP), F32)
        m0 = jnp.zeros((batch, ML_HEADS, 2, 8, LANE), F32)
        hf_c, hb_c, c1, m1 = _mlstm(qkv_c, gates_c, grow_c, bias_col, bias_row, c0, m0, batch)
        hf_l, hb_l, _, _ = _mlstm(qkv_l, gates_l, grow_l, bias_col, bias_row, c1, m1, batch)

        na_lat = _na(proj_l, proj_c, na_bias, batch)

        h = _outproj(mla_lat, hf_l, hb_l, proj_l, na_lat, w_out_p, w2_p, l, h, mod4, 2, row_lat)
        h = _ffn(h, norm2_g[l], mod4, row_lat, w_up_p, ffn_conv_w, fcb, w_down_p, l, seq,
                 final_g=final_norm_g if last else None)

        if not last:
            mla_ctx = _flash(q_c, 0, k_c, 0, v_c, 0, MLA_HEADS, MLA_DQ, MLA_V, batch)
            na_ctx = _flash(proj_c, C_NAQ // NA_DH, proj_c, C_NAK // NA_DH, proj_c, C_NAV // NA_DH,
                            NA_HEADS, NA_DH, NA_DH, batch, scale=NA_DH ** -0.5 * LOG2E)
            hc = _outproj(mla_ctx, hf_c, hb_c, proj_c, na_ctx, w_out_p, w2_p, l, hc, mod4, 2, row_ctx)
            hc = _ffn(hc, norm2_g[l], mod4, row_ctx, w_up_p, ffn_conv_w, fcb, w_down_p, l, tctx)

    return h.reshape(batch, seq, d)
```

```python
import functools
import math

import numpy as np
import jax
import jax.numpy as jnp
from jax import lax
from jax.experimental import pallas as pl
from jax.experimental.pallas import tpu as pltpu

F32 = jnp.float32
BF16 = jnp.bfloat16

D_MODEL = 2048
GRID_W = 64
MLA_HEADS = 6
MLA_NOPE = 128
MLA_ROPE = 64
MLA_V = 128
MLA_Q_LORA = 512
MLA_KV_LORA = 256
ML_HEADS = 4
ML_DH = 192
NA_HEADS = 4
NA_DH = 128
NA_KR = 8
NA_KC = 16
ROPE_BASE = 10000.0
EPS = 1e-6
N_MOD = 6
ML_W = ML_HEADS * ML_DH
NA_W = NA_HEADS * NA_DH
MLA_W = MLA_HEADS * MLA_V
IN_SIZES = (MLA_Q_LORA, MLA_KV_LORA, MLA_ROPE, ML_W, ML_W, ML_W, ML_W, 2 * ML_HEADS, 2 * ML_HEADS, NA_W, NA_W, NA_W)
IN_OFFSETS = tuple(sum(IN_SIZES[:i + 1]) for i in range(len(IN_SIZES) - 1))
LOG2E = math.log2(math.e)

LANE = 128
ML_DP = 256
ML_WP = ML_HEADS * ML_DP
ML_L = 256
MLA_DQ = 256
TM = 512
TM_BIG = 1024
HALO = 16
VMEM_LIMIT = 58 * 1024 * 1024

C_MLQ = 0
C_MLK = C_MLQ + ML_W
C_MLV = C_MLK + ML_W
C_MLO = C_MLV + ML_W
C_QLAT = C_MLO + ML_W
C_KVLAT = C_QLAT + MLA_Q_LORA
C_KROPE = C_KVLAT + MLA_KV_LORA
C_GATE0 = C_KROPE + LANE
C_GATES = LANE
C_NAQ = C_GATE0 + C_GATES
C_NAK = C_NAQ + NA_W
C_NAV = C_NAK + NA_W
C_IN = C_NAV + NA_W
TN_IN = 512
ML_PAIR = 2 * ML_DH


def _cp(sem):
    return pltpu.CompilerParams(dimension_semantics=sem, vmem_limit_bytes=VMEM_LIMIT)


def _sigmoid(x):
    return 1.0 / (1.0 + jnp.exp(-x))


def _rms(x, g):
    return x * lax.rsqrt(jnp.mean(x * x, axis=-1, keepdims=True) + EPS) * g


def _mod_spec(chunk, rowf, tm, width=D_MODEL):
    return pl.BlockSpec((None, None, 1, width), lambda i, j: (rowf(i, tm), chunk, 0, j if width != D_MODEL else 0))


def _ada_kernel(c_ref, w_ref, b_ref, o_ref):
    c = c_ref[...]
    a = c * _sigmoid(c)
    o_ref[...] = jnp.dot(a, w_ref[...], preferred_element_type=F32,
                         precision=lax.Precision.HIGHEST) + b_ref[...]


def _ada(cond8, w, b, l):
    _, d, n = w.shape
    tn = 1024
    return pl.pallas_call(
        _ada_kernel,
        grid=(n // tn,),
        in_specs=[pl.BlockSpec((8, d), lambda j: (0, 0)),
                  pl.BlockSpec((None, d, tn), lambda j: (l, 0, j)),
                  pl.BlockSpec((None, 1, tn), lambda j: (l, 0, j))],
        out_specs=pl.BlockSpec((8, tn), lambda j: (0, j)),
        out_shape=jax.ShapeDtypeStruct((8, n), F32),
        compiler_params=_cp(("arbitrary",)),
    )(cond8, w, b.reshape(b.shape[0], 1, n))


def _nm_mm_kernel(x_ref, g_ref, sc_ref, sh_ref, w_ref, o_ref, t_ref, tt_ref, hn_ref, *, f32_col):
    j = pl.program_id(1)
    tn = w_ref.shape[1]

    @pl.when(j == 0)
    def _():
        y = _rms(x_ref[...], g_ref[...])
        hn_ref[...] = (y * (1.0 + sc_ref[...]) + sh_ref[...]).astype(BF16)

    acc = jnp.dot(hn_ref[...], w_ref[...], preferred_element_type=F32)
    o_ref[...] = acc.astype(o_ref.dtype)

    @pl.when(j == f32_col // tn)
    def _():
        t = acc[:, f32_col % tn:f32_col % tn + t_ref.shape[1]]
        t_ref[...] = t
        tt_ref[...] = t.T


def _nm_mm(x, g, mod4, sc_chunk, sh_chunk, rowf, w, l, tn, f32_col, f32_cols):
    m, d = x.shape
    n = w.shape[2]
    tm = min(TM_BIG, m)
    return pl.pallas_call(
        functools.partial(_nm_mm_kernel, f32_col=f32_col),
        grid=(m // tm, n // tn),
        in_specs=[pl.BlockSpec((tm, d), lambda i, j: (i, 0)),
                  pl.BlockSpec((1, d), lambda i, j: (0, 0)),
                  _mod_spec(sc_chunk, rowf, tm), _mod_spec(sh_chunk, rowf, tm),
                  pl.BlockSpec((None, d, tn), lambda i, j: (l, 0, j))],
        out_specs=[pl.BlockSpec((tm, tn), lambda i, j: (i, j)),
                   pl.BlockSpec((tm, f32_cols), lambda i, j: (i, 0)),
                   pl.BlockSpec((f32_cols, tm), lambda i, j: (0, i))],
        out_shape=[jax.ShapeDtypeStruct((m, n), BF16), jax.ShapeDtypeStruct((m, f32_cols), F32),
                   jax.ShapeDtypeStruct((f32_cols, m), F32)],
        scratch_shapes=[pltpu.VMEM((tm, d), BF16)],
        compiler_params=_cp(("parallel", "arbitrary")),
    )(x, g.reshape(1, d), mod4, mod4, w)


def _mla_up_kernel(ql_ref, kvl_ref, kr_ref, tab_ref, qg_ref, kvg_ref, wuq_ref, wuk_ref, wuv_ref,
                   q_ref, k_ref, v_ref, *, scale):
    tab = tab_ref[...]
    qn = _rms(ql_ref[...].astype(F32), qg_ref[...]).astype(BF16)
    qf = jnp.dot(qn, wuq_ref[...], preferred_element_type=F32)
    for h in range(MLA_HEADS):
        c0 = h * MLA_DQ
        y = qf[:, c0 + MLA_NOPE:c0 + MLA_DQ] * tab
        z = y + pltpu.roll(y, MLA_ROPE, axis=1)
        q_ref[:, c0:c0 + MLA_NOPE] = (qf[:, c0:c0 + MLA_NOPE] * scale).astype(BF16)
        q_ref[:, c0 + MLA_NOPE:c0 + MLA_DQ] = (z * scale).astype(BF16)
    cn = _rms(kvl_ref[...].astype(F32), kvg_ref[...]).astype(BF16)
    kn = jnp.dot(cn, wuk_ref[...], preferred_element_type=F32)
    v_ref[...] = jnp.dot(cn, wuv_ref[...], preferred_element_type=F32).astype(BF16)
    ky = kr_ref[...].astype(F32) * tab
    kz = ky + pltpu.roll(ky, MLA_ROPE, axis=1)
    lane = lax.broadcasted_iota(jnp.int32, kz.shape, 1)
    kz = jnp.where(lane < MLA_ROPE, kz, 0.0).astype(BF16)
    for h in range(MLA_HEADS):
        c0 = h * MLA_DQ
        k_ref[:, c0:c0 + MLA_NOPE] = kn[:, h * MLA_NOPE:(h + 1) * MLA_NOPE].astype(BF16)
        k_ref[:, c0 + MLA_NOPE:c0 + MLA_DQ] = kz


def _mla_up(proj, tab, qg, kvg, wuq, wuk, wuv):
    m = proj.shape[0]
    tm = min(TM, tab.shape[0])
    nt = tab.shape[0] // tm
    full = lambda a: pl.BlockSpec(a.shape, lambda i: (0, 0))
    return pl.pallas_call(
        functools.partial(_mla_up_kernel, scale=(MLA_NOPE + MLA_ROPE) ** -0.5 * LOG2E),
        grid=(m // tm,),
        in_specs=[pl.BlockSpec((tm, MLA_Q_LORA), lambda i: (i, C_QLAT // MLA_Q_LORA)),
                  pl.BlockSpec((tm, MLA_KV_LORA), lambda i: (i, C_KVLAT // MLA_KV_LORA)),
                  pl.BlockSpec((tm, LANE), lambda i: (i, C_KROPE // LANE)),
                  pl.BlockSpec((tm, LANE), lambda i: (i % nt, 0)),
                  full(qg), full(kvg), full(wuq), full(wuk), full(wuv)],
        out_specs=[pl.BlockSpec((tm, MLA_HEADS * MLA_DQ), lambda i: (i, 0)),
                   pl.BlockSpec((tm, MLA_HEADS * MLA_DQ), lambda i: (i, 0)),
                   pl.BlockSpec((tm, MLA_W), lambda i: (i, 0))],
        out_shape=[jax.ShapeDtypeStruct((m, MLA_HEADS * MLA_DQ), BF16),
                   jax.ShapeDtypeStruct((m, MLA_HEADS * MLA_DQ), BF16),
                   jax.ShapeDtypeStruct((m, MLA_W), BF16)],
        compiler_params=_cp(("parallel",)),
    )(proj, proj, proj, tab, qg, kvg, wuq, wuk, wuv)


def _flash_kernel(*refs, has_lat, scale, nsub, tk):
    if has_lat:
        q_ref, kc_ref, vc_ref, kl_ref, vl_ref, o_ref, m_scr, acc_scr = refs
    else:
        q_ref, kc_ref, vc_ref, o_ref, m_scr, acc_scr = refs
    ts = q_ref.shape[0] // nsub
    dv = vc_ref.shape[1]
    dn = (((1,), (1,)), ((), ()))

    def scores(q, k):
        s = lax.dot_general(q, k, dn, preferred_element_type=F32)
        return s if scale == 1.0 else s * scale

    def with_ones(v):
        lane = lax.broadcasted_iota(jnp.int32, v.shape, 1)
        return jnp.concatenate([v, jnp.where(lane == 0, 1.0, 0.0).astype(v.dtype)], axis=1)

    def lanes(x, n):
        return x if n == LANE else pltpu.repeat(x, n // LANE, axis=1)

    def probs(s, m):
        return jnp.exp2(s - lanes(m, s.shape[1])).astype(BF16)

    kc = kc_ref[...]
    vc = with_ones(vc_ref[...])
    for u in range(nsub):
        s = scores(q_ref[u * ts:(u + 1) * ts, :], kc)
        m = jnp.broadcast_to(jnp.max(s, axis=-1, keepdims=True), (ts, LANE))
        m_scr[u] = m
        acc_scr[u] = jnp.dot(probs(s, m), vc, preferred_element_type=F32)

    if has_lat:
        def body(kb, carry):
            off = pl.multiple_of(kb * tk, tk)
            k = kl_ref[pl.ds(off, tk), :]
            v = with_ones(vl_ref[pl.ds(off, tk), :])
            s_next = scores(q_ref[0:ts, :], k)
            for u in range(nsub):
                s = s_next
                if u + 1 < nsub:
                    s_next = scores(q_ref[(u + 1) * ts:(u + 2) * ts, :], k)
                m_prev = m_scr[u]
                m_new = jnp.maximum(m_prev, jnp.max(s, axis=-1, keepdims=True))
                m_scr[u] = m_new
                acc_scr[u] = (lanes(jnp.exp2(m_prev - m_new), 2 * dv) * acc_scr[u]
                              + jnp.dot(probs(s, m_new), v, preferred_element_type=F32))
            return carry

        lax.fori_loop(0, kl_ref.shape[0] // tk, body, 0)

    for u in range(nsub):
        acc = acc_scr[u]
        o_ref[u * ts:(u + 1) * ts, :] = (acc[:, :dv] / acc[:, dv:dv + 1]).astype(o_ref.dtype)


def _flash(q, qcol, kc, kccol, vc, vccol, heads, dq, dv, batch, lat=None, scale=1.0, tq=2048, nsub=2, tk=2048):
    mq = q.shape[0]
    sq = mq // batch
    tq = min(tq, sq)
    nsub = min(nsub, tq // 256)
    nq = sq // tq
    tc = kc.shape[0] // batch
    has_lat = lat is not None
    in_specs = [pl.BlockSpec((tq, dq), lambda b, h, i: (b * nq + i, qcol + h)),
                pl.BlockSpec((tc, dq), lambda b, h, i: (b, kccol + h)),
                pl.BlockSpec((tc, dv), lambda b, h, i: (b, vccol + h))]
    args = [q, kc, vc]
    if has_lat:
        kl, klcol, vl, vlcol = lat
        sk = kl.shape[0] // batch
        tk = min(tk, sk)
        in_specs += [pl.BlockSpec((sk, dq), lambda b, h, i: (b, klcol + h)),
                     pl.BlockSpec((sk, dv), lambda b, h, i: (b, vlcol + h))]
        args += [kl, vl]
    return pl.pallas_call(
        functools.partial(_flash_kernel, has_lat=has_lat, scale=scale, nsub=nsub, tk=tk),
        grid=(batch, heads, nq),
        in_specs=in_specs,
        out_specs=pl.BlockSpec((tq, dv), lambda b, h, i: (b * nq + i, h)),
        out_shape=jax.ShapeDtypeStruct((mq, heads * dv), BF16),
        scratch_shapes=[pltpu.VMEM((nsub, tq // nsub, LANE), F32), pltpu.VMEM((nsub, tq // nsub, 2 * dv), F32)],
        compiler_params=_cp(("parallel", "parallel", "arbitrary")),
    )(*args)


NA_RB = 8
NA_BLK = NA_RB * GRID_W
NA_GROUP = 2


def _na_kernel(q_ref, kp_ref, kc_ref, kn_ref, vp_ref, vc_ref, vn_ref, kx_ref, vx_ref, bias_ref, o_ref,
               kw_scr, vw_scr, *, rows, scale):
    j = pl.program_id(1)
    for t, (kr, vr) in enumerate(((kp_ref, vp_ref), (kc_ref, vc_ref), (kn_ref, vn_ref))):
        kw_scr[t * NA_BLK:(t + 1) * NA_BLK, :] = kr[...]
        vw_scr[t * NA_BLK:(t + 1) * NA_BLK, :] = vr[...]
    nwin = NA_KR * GRID_W
    dn = (((1,), (1,)), ((), ()))
    heads = [slice(h * NA_DH, (h + 1) * NA_DH) for h in range(NA_HEADS)]
    for i0 in range(0, NA_RB, NA_GROUP):
        units = []
        for i in range(i0, i0 + NA_GROUP):
            r = NA_RB * j + i
            r0 = jnp.clip(r - NA_KR // 2, 0, rows - NA_KR)
            start = pl.multiple_of((r0 - NA_RB * j + NA_RB) * GRID_W, GRID_W)
            for h, cs in enumerate(heads):
                q = q_ref[i * GRID_W:(i + 1) * GRID_W, cs]
                s_loc = (lax.dot_general(q, kw_scr[pl.ds(start, nwin), cs], dn, preferred_element_type=F32) * scale
                         + bias_ref[h, r - r0])
                s_ctx = lax.dot_general(q, kx_ref[:, cs], dn, preferred_element_type=F32) * scale
                units.append((i, cs, start, s_loc, s_ctx))
        outs = {}
        for i, cs, start, s_loc, s_ctx in units:
            m = jnp.maximum(jnp.max(s_loc, axis=-1, keepdims=True), jnp.max(s_ctx, axis=-1, keepdims=True))
            p_loc = jnp.exp(s_loc - m)
            p_ctx = jnp.exp(s_ctx - m)
            l = jnp.sum(p_loc, axis=-1, keepdims=True) + jnp.sum(p_ctx, axis=-1, keepdims=True)
            o = (jnp.dot(p_loc.astype(BF16), vw_scr[pl.ds(start, nwin), cs], preferred_element_type=F32)
                 + jnp.dot(p_ctx.astype(BF16), vx_ref[:, cs], preferred_element_type=F32))
            outs.setdefault(i, []).append((o / l).astype(o_ref.dtype))
        for i, parts in outs.items():
            o_ref[i * GRID_W:(i + 1) * GRID_W, :] = jnp.concatenate(parts, axis=1)


def _na_bias_table(rpb):
    c = np.arange(GRID_W)
    cs = np.clip(c - NA_KC // 2, 0, GRID_W - NA_KC)
    valid = (c[None, :] >= cs[:, None]) & (c[None, :] < cs[:, None] + NA_KC)
    dc = np.clip(c[None, :] - c[:, None] + (NA_KC - 1), 0, 2 * NA_KC - 2)
    dr = np.arange(NA_KR)[None, :] - np.arange(NA_KR)[:, None] + (NA_KR - 1)
    sel_r = (dr[:, :, None] == np.arange(2 * NA_KR - 1)).astype(np.float32)
    sel_c = ((dc[:, :, None] == np.arange(2 * NA_KC - 1)) & valid[:, :, None]).astype(np.float32)
    tab = jnp.einsum('lhab,via,cdb->lhvcid', rpb.astype(F32), sel_r, sel_c, precision=lax.Precision.HIGHEST)
    tab = tab + jnp.where(valid[None, None, None, :, None, :], 0.0, -1e30)
    return tab.reshape(rpb.shape[0], NA_HEADS, NA_KR, GRID_W, NA_KR * GRID_W)


def _na(proj_l, proj_c, bias, l, batch):
    m = proj_l.shape[0]
    s = m // batch
    rows = s // GRID_W
    nb = rows // NA_RB
    cq, ck, cv = C_NAQ // NA_W, C_NAK // NA_W, C_NAV // NA_W
    tc = proj_c.shape[0] // batch
    prev = lambda b, j: b * nb + jnp.maximum(j - 1, 0)
    cur = lambda b, j: b * nb + j
    nxt = lambda b, j: b * nb + jnp.minimum(j + 1, nb - 1)
    blk = lambda rf, col: pl.BlockSpec((NA_BLK, NA_W), lambda b, j: (rf(b, j), col))
    return pl.pallas_call(
        functools.partial(_na_kernel, rows=rows, scale=NA_DH ** -0.5),
        grid=(batch, nb),
        in_specs=[blk(cur, cq), blk(prev, ck), blk(cur, ck), blk(nxt, ck),
                  blk(prev, cv), blk(cur, cv), blk(nxt, cv),
                  pl.BlockSpec((tc, NA_W), lambda b, j: (b, ck)),
                  pl.BlockSpec((tc, NA_W), lambda b, j: (b, cv)),
                  pl.BlockSpec((None,) + bias.shape[1:], lambda b, j: (l, 0, 0, 0, 0))],
        out_specs=pl.BlockSpec((NA_BLK, NA_W), lambda b, j: (b * nb + j, 0)),
        out_shape=jax.ShapeDtypeStruct((m, NA_W), BF16),
        scratch_shapes=[pltpu.VMEM((3 * NA_BLK, NA_W), BF16), pltpu.VMEM((3 * NA_BLK, NA_W), BF16)],
        compiler_params=_cp(("parallel", "arbitrary")),
    )(proj_l, proj_l, proj_l, proj_l, proj_l, proj_l, proj_l, proj_c, proj_c, bias)


def _ml_prep_kernel(x_ref, p_ref, n_ref, w_ref, b_ref, sc_ref, o_ref, *, tiles_per_seg, n_conv):
    j = pl.program_id(1)
    x = x_ref[...].astype(F32)
    tm = x.shape[0]

    def widen(y, fill):
        lane = lax.broadcasted_iota(jnp.int32, (tm, ML_DP - ML_DH), 1)
        spare = jnp.where(lane == 0, fill, 0.0)
        return jnp.concatenate([y[:, :ML_DH], spare, y[:, ML_DH:], spare], axis=1).astype(o_ref.dtype)

    @pl.when(j < n_conv)
    def _():
        t = pl.program_id(0) % tiles_per_seg
        prev_row = jnp.where(t == 0, 0.0, p_ref[HALO - 1:HALO, :].astype(F32))
        next_row = jnp.where(t == tiles_per_seg - 1, 0.0, n_ref[0:1, :].astype(F32))
        rid = lax.broadcasted_iota(jnp.int32, x.shape, 0)
        xp = jnp.where(rid == 0, prev_row, pltpu.roll(x, 1, axis=0))
        xn = jnp.where(rid == tm - 1, next_row, pltpu.roll(x, tm - 1, axis=0))
        w = w_ref[...]
        y = xp * w[0:1, :] + x * w[1:2, :] + xn * w[2:3, :] + b_ref[...]
        o_ref[...] = widen(y * _sigmoid(y) * sc_ref[...], 0.0)

    @pl.when(j >= n_conv)
    def _():
        o_ref[...] = widen(x, 1.0)


def _ml_prep(proj, w, b, colscale, seg):
    m = proj.shape[0]
    tm = min(TM_BIG, seg)
    hb = tm // HALO
    last = m // HALO - 1
    n_conv = w.shape[1] // ML_PAIR
    n_all = 3 * ML_W // ML_PAIR
    vec = lambda r: pl.BlockSpec((r, ML_PAIR), lambda i, j: (0, jnp.minimum(j, n_conv - 1)))
    return pl.pallas_call(
        functools.partial(_ml_prep_kernel, tiles_per_seg=seg // tm, n_conv=n_conv),
        grid=(m // tm, n_all),
        in_specs=[pl.BlockSpec((tm, ML_PAIR), lambda i, j: (i, j)),
                  pl.BlockSpec((HALO, ML_PAIR), lambda i, j: (jnp.maximum(i * hb - 1, 0), j)),
                  pl.BlockSpec((HALO, ML_PAIR), lambda i, j: (jnp.minimum((i + 1) * hb, last), j)),
                  vec(3), vec(1), vec(1)],
        out_specs=pl.BlockSpec((tm, 2 * ML_DP), lambda i, j: (i, j)),
        out_shape=jax.ShapeDtypeStruct((m, 3 * ML_WP), BF16),
        compiler_params=_cp(("parallel", "parallel")),
    )(proj, proj, proj, w, b, colscale)


def _split3(x):
    x1 = x.astype(BF16)
    r = x - x1.astype(F32)
    x2 = r.astype(BF16)
    x3 = (r - x2.astype(F32)).astype(BF16)
    return x1, x2, x3


def _log_sigmoid(x):
    return jnp.minimum(x, 0.0) - jnp.log(1.0 + jnp.exp(-jnp.abs(x)))


def _mlstm_dir(d, q_ref, k_ref, v_ref, gc_ref, gr_ref, bc_ref, br_ref, h_ref, c_scr, m_scr):
    L = ML_L
    ri = lax.broadcasted_iota(jnp.int32, (L, L), 0)
    ci = lax.broadcasted_iota(jnp.int32, (L, L), 1)
    mask = (ci <= ri) if d == 0 else (ci >= ri)
    tri = jnp.where(mask, 1.0, 0.0).astype(BF16)
    tri_t = jnp.where((ri <= ci) if d == 0 else (ri >= ci), 1.0, 0.0).astype(BF16)
    lane = lax.broadcasted_iota(jnp.int32, (L, ML_DP), 1)

    gc = gc_ref[...] + bc_ref[...]
    gr = gr_ref[...] + br_ref[...]
    bc_all = sum(jnp.dot(tri, p, preferred_element_type=F32) for p in _split3(_log_sigmoid(gc)))
    br_all = sum(jnp.dot(p, tri_t, preferred_element_type=F32) for p in _split3(_log_sigmoid(gr)))

    hs = range(ML_HEADS)
    cols = [slice(h * ML_DP, (h + 1) * ML_DP) for h in hs]
    q = [q_ref[:, cs] for cs in cols]
    k = [k_ref[:, cs] for cs in cols]
    v = [v_ref[:, cs] for cs in cols]
    m_prev = [m_scr[h, d][0:1, 0:1] for h in hs]
    ct = [c_scr[h, d] for h in hs]
    qk = [lax.dot_general(q[h], k[h], (((1,), (1,)), ((), ())), preferred_element_type=F32) for h in hs]
    inter = [jnp.dot(q[h], ct[h].astype(BF16), preferred_element_type=F32) for h in hs]

    rep = lambda col: jnp.broadcast_to(col, (L, LANE))
    wide = lambda x: pltpu.repeat(x, ML_DP // LANE, axis=1)
    b_rep = [rep(bc_all[:, 4 * h + 2 + d:4 * h + 3 + d]) for h in hs]
    li_rep = [rep(gc[:, 4 * h + d:4 * h + d + 1]) for h in hs]
    a, w_st, m_t = [], [], []
    for h in hs:
        b_row = br_all[4 * h + 2 + d:4 * h + 3 + d, :]
        li_row = gr[4 * h + d:4 * h + d + 1, :]
        dmat = jnp.where(mask, wide(b_rep[h]) - b_row + li_row, -jnp.inf)
        g = b_rep[h] + m_prev[h]
        m_t.append(jnp.maximum(g, jnp.max(dmat, axis=-1, keepdims=True)))
        w_st.append(jnp.exp(g - m_t[h]))
        a.append((qk[h] * jnp.exp(dmat - wide(m_t[h]))).astype(BF16))

    for h in hs:
        num = wide(w_st[h]) * inter[h] + jnp.dot(a[h], v[h], preferred_element_type=F32)
        den = rep(num[:, ML_DH:ML_DH + 1])
        inv = 1.0 / jnp.maximum(jnp.abs(den), jnp.exp(-m_t[h]))
        h_ref[:, cols[h]] = jnp.where(lane < ML_DH, num * wide(inv), 0.0).astype(h_ref.dtype)

    for h in hs:
        b_end = b_rep[h][L - 1:L, :] if d == 0 else b_rep[h][0:1, :]
        e_log = b_end - b_rep[h] + li_rep[h]
        m_new = jnp.maximum(b_end + m_prev[h], jnp.max(e_log, axis=0, keepdims=True))
        decay = jnp.exp(b_end + m_prev[h] - m_new)[:, 0:1]
        ek = (wide(jnp.exp(e_log - m_new)) * k[h].astype(F32)).astype(BF16)
        c_scr[h, d] = decay * ct[h] + lax.dot_general(ek, v[h], (((0,), (0,)), ((), ())),
                                                      preferred_element_type=F32)
        m_scr[h, d] = jnp.broadcast_to(m_new, m_scr.shape[2:])


def _mlstm_kernel(qf_ref, kf_ref, vf_ref, gcf_ref, grf_ref, qb_ref, kb_ref, vb_ref, gcb_ref, grb_ref,
                  bc_ref, br_ref, c0_ref, m0_ref, hf_ref, hb_ref, c1_ref, m1_ref, c_scr, m_scr):
    c = pl.program_id(1)

    @pl.when(c == 0)
    def _():
        c_scr[...] = c0_ref[...]
        m_scr[...] = m0_ref[...]

    _mlstm_dir(0, qf_ref, kf_ref, vf_ref, gcf_ref, grf_ref, bc_ref, br_ref, hf_ref, c_scr, m_scr)
    _mlstm_dir(1, qb_ref, kb_ref, vb_ref, gcb_ref, grb_ref, bc_ref, br_ref, hb_ref, c_scr, m_scr)

    @pl.when(c == pl.num_programs(1) - 1)
    def _():
        c1_ref[...] = c_scr[...]
        m1_ref[...] = m_scr[...]


def _mlstm(qkv, gates, gates_row, bias_col, bias_row, c0, m0, batch):
    m = qkv.shape[0]
    s = m // batch
    nc = s // ML_L
    fwd = lambda b, c: b * nc + c
    bwd = lambda b, c: b * nc + (nc - 1 - c)

    def dir_specs(rf):
        return [pl.BlockSpec((ML_L, ML_WP), lambda b, c: (rf(b, c), 0)),
                pl.BlockSpec((ML_L, ML_WP), lambda b, c: (rf(b, c), 1)),
                pl.BlockSpec((ML_L, ML_WP), lambda b, c: (rf(b, c), 2)),
                pl.BlockSpec((ML_L, C_GATES), lambda b, c: (rf(b, c), 0)),
                pl.BlockSpec((4 * ML_HEADS, ML_L), lambda b, c: (0, rf(b, c)))]

    st_c = pl.BlockSpec((None, ML_HEADS, 2, ML_DP, ML_DP), lambda b, c: (b, 0, 0, 0, 0))
    st_m = pl.BlockSpec((None, ML_HEADS, 2, 8, LANE), lambda b, c: (b, 0, 0, 0, 0))
    return pl.pallas_call(
        _mlstm_kernel,
        grid=(batch, nc),
        in_specs=dir_specs(fwd) + dir_specs(bwd)
        + [pl.BlockSpec((1, C_GATES), lambda b, c: (0, 0)),
           pl.BlockSpec((4 * ML_HEADS, 1), lambda b, c: (0, 0)), st_c, st_m],
        out_specs=[pl.BlockSpec((ML_L, ML_WP), lambda b, c: (fwd(b, c), 0)),
                   pl.BlockSpec((ML_L, ML_WP), lambda b, c: (bwd(b, c), 0)), st_c, st_m],
        out_shape=[jax.ShapeDtypeStruct((m, ML_WP), BF16), jax.ShapeDtypeStruct((m, ML_WP), BF16),
                   jax.ShapeDtypeStruct(c0.shape, F32), jax.ShapeDtypeStruct(m0.shape, F32)],
        scratch_shapes=[pltpu.VMEM((ML_HEADS, 2, ML_DP, ML_DP), F32), pltpu.VMEM((ML_HEADS, 2, 8, LANE), F32)],
        compiler_params=_cp(("parallel", "arbitrary")),
    )(qkv, qkv, qkv, gates, gates_row, qkv, qkv, qkv, gates, gates_row, bias_col, bias_row, c0, m0)


def _outproj_kernel(mla_ref, hf_ref, hb_ref, og_ref, na_ref, w1_ref, w2_ref, w3_ref, res_ref, gate_ref,
                    out_ref, ml_scr):
    @pl.when(pl.program_id(1) == 0)
    def _():
        og = _sigmoid(og_ref[...].astype(F32))
        for h in range(ML_HEADS):
            cs = slice(h * ML_DP, (h + 1) * ML_DP)
            o_h = jnp.concatenate([og[:, h * ML_DH:(h + 1) * ML_DH],
                                   jnp.zeros((og.shape[0], ML_DP - ML_DH), F32)], axis=1)
            ml_scr[:, cs] = ((hf_ref[:, cs].astype(F32) + hb_ref[:, cs].astype(F32)) * o_h).astype(BF16)

    acc = (jnp.dot(mla_ref[...], w1_ref[...], preferred_element_type=F32)
           + jnp.dot(ml_scr[...], w2_ref[...], preferred_element_type=F32)
           + jnp.dot(na_ref[...], w3_ref[...], preferred_element_type=F32))
    out_ref[...] = res_ref[...] + gate_ref[...] * acc


def _outproj(mla, hf, hb, proj, na, w_out, w2, l, res, mod4, gate_chunk, rowf):
    m, d = res.shape
    tm = min(TM, m)
    tn = d
    rows = lambda width, col: pl.BlockSpec((tm, width), lambda i, j: (i, col))
    wrows = lambda nrows, blk: pl.BlockSpec((None, nrows, tn), lambda i, j: (l, blk, j),
                                            pipeline_mode=pl.Buffered(1))
    return pl.pallas_call(
        _outproj_kernel,
        grid=(m // tm, d // tn),
        in_specs=[rows(MLA_W, 0), rows(ML_WP, 0), rows(ML_WP, 0), rows(ML_W, C_MLO // ML_W), rows(NA_W, 0),
                  wrows(MLA_W, 0), wrows(ML_WP, 0), wrows(NA_W, (MLA_W + ML_W) // NA_W),
                  pl.BlockSpec((tm, tn), lambda i, j: (i, j)), _mod_spec(gate_chunk, rowf, tm, tn)],
        out_specs=pl.BlockSpec((tm, tn), lambda i, j: (i, j)),
        out_shape=jax.ShapeDtypeStruct((m, d), F32),
        scratch_shapes=[pltpu.VMEM((tm, ML_WP), BF16)],
        compiler_params=_cp(("parallel", "arbitrary")),
    )(mla, hf, hb, proj, na, w_out, w2, w_out, res, mod4)


def _ffn_kernel(x_ref, xp_ref, xn_ref, g_ref, sc_ref, sh_ref, wg_ref, wv_ref, cwg_ref, cwv_ref, cbg_ref, cbv_ref,
                wd_ref, gate_ref, *rest, tiles_per_seg, final):
    if final:
        fg_ref, o_ref, hn_scr = rest
    else:
        o_ref, hn_scr = rest
    i = pl.program_id(0)
    j = pl.program_id(1)
    tm = x_ref.shape[0]
    te = hn_scr.shape[0]

    @pl.when(j == 0)
    def _():
        nm = lambda x: (_rms(x, g_ref[...]) * (1.0 + sc_ref[...]) + sh_ref[...]).astype(BF16)
        hn_scr[0:HALO, :] = nm(xp_ref[...])
        hn_scr[HALO:HALO + tm, :] = nm(x_ref[...])
        hn_scr[HALO + tm:te, :] = nm(xn_ref[...])
        o_ref[...] = jnp.zeros_like(o_ref)

    t = i % tiles_per_seg
    keep_prev = jnp.where(t == 0, 0.0, 1.0)
    keep_next = jnp.where(t == tiles_per_seg - 1, 0.0, 1.0)
    hn = hn_scr[...]

    def conv(w_ref, cw_ref, cb_ref):
        u = jnp.dot(hn, w_ref[...], preferred_element_type=F32)
        rid = lax.broadcasted_iota(jnp.int32, (tm, u.shape[1]), 0)
        up = pltpu.roll(u, 1, axis=0)[HALO:HALO + tm]
        un = pltpu.roll(u, te - 1, axis=0)[HALO:HALO + tm]
        up = jnp.where(rid == 0, up * keep_prev, up)
        un = jnp.where(rid == tm - 1, un * keep_next, un)
        cw = cw_ref[...]
        return up * cw[0:1, :] + u[HALO:HALO + tm] * cw[1:2, :] + un * cw[2:3, :] + cb_ref[...]

    gg = conv(wg_ref, cwg_ref, cbg_ref)
    vv = conv(wv_ref, cwv_ref, cbv_ref)
    act = (gg * _sigmoid(gg) * vv).astype(BF16)
    o_ref[...] += jnp.dot(act, wd_ref[...], preferred_element_type=F32)

    @pl.when(j == pl.num_programs(1) - 1)
    def _():
        y = x_ref[...] + gate_ref[...] * o_ref[...]
        o_ref[...] = _rms(y, fg_ref[...]) if final else y


def _ffn(x, g, mod4, rowf, w_up, cw, cb, w_down, l, seg, final_g=None, tf=512):
    m, d = x.shape
    dff = w_down.shape[1]
    tm = min(TM_BIG, seg)
    nj = dff // tf
    hb = tm // HALO
    last = m // HALO - 1
    final = final_g is not None
    vec = lambda r, off: pl.BlockSpec((None, r, tf), lambda i, j: (l, 0, off + j))
    once = pl.Buffered(1)
    in_specs = [pl.BlockSpec((tm, d), lambda i, j: (i, 0)),
                pl.BlockSpec((HALO, d), lambda i, j: (jnp.maximum(i * hb - 1, 0), 0)),
                pl.BlockSpec((HALO, d), lambda i, j: (jnp.minimum((i + 1) * hb, last), 0)),
                pl.BlockSpec((1, d), lambda i, j: (0, 0)),
                _mod_spec(4, rowf, tm), _mod_spec(3, rowf, tm),
                pl.BlockSpec((None, d, tf), lambda i, j: (l, 0, j)),
                pl.BlockSpec((None, d, tf), lambda i, j: (l, 0, nj + j)),
                vec(3, 0), vec(3, nj), vec(1, 0), vec(1, nj),
                pl.BlockSpec((None, tf, d), lambda i, j: (l, j, 0)), _mod_spec(5, rowf, tm)]
    args = [x, x, x, g.reshape(1, d), mod4, mod4, w_up, w_up, cw, cw, cb, cb, w_down, mod4]
    if final:
        in_specs.append(pl.BlockSpec((1, d), lambda i, j: (0, 0)))
        args.append(final_g.reshape(1, d))
    return pl.pallas_call(
        functools.partial(_ffn_kernel, tiles_per_seg=seg // tm, final=final),
        grid=(m // tm, nj),
        in_specs=in_specs,
        out_specs=pl.BlockSpec((tm, d), lambda i, j: (i, 0), pipeline_mode=once),
        out_shape=jax.ShapeDtypeStruct((m, d), F32),
        scratch_shapes=[pltpu.VMEM((tm + 2 * HALO, d), BF16)],
        compiler_params=_cp(("parallel", "arbitrary")),
    )(*args)


def _pad_heads(a, heads, dh, dp, axis):
    shp = a.shape
    a = a.reshape(shp[:axis] + (heads, dh) + shp[axis + 1:])
    pad = [(0, 0)] * a.ndim
    pad[axis + 1] = (0, dp - dh)
    a = jnp.pad(a, pad)
    return a.reshape(shp[:axis] + (heads * dp,) + shp[axis + 1:])


def _rope_cols(a):
    ev, od = a[..., 0::2], a[..., 1::2]
    return jnp.concatenate([ev, od, od, ev], axis=-1)


def _prep_w_in(w):
    w = w.astype(BF16)
    lead = w.shape[:-1]
    o_rope, o_ml, o_gi, o_na = IN_OFFSETS[1], IN_OFFSETS[2], IN_OFFSETS[6], IN_OFFSETS[8]
    gi = w[..., o_gi:o_gi + 2 * ML_HEADS].reshape(lead + (2, ML_HEADS))
    gf = w[..., o_gi + 2 * ML_HEADS:o_na].reshape(lead + (2, ML_HEADS))
    g4 = jnp.stack([gi[..., 0, :], gi[..., 1, :], gf[..., 0, :], gf[..., 1, :]], axis=-1)
    gates = g4.reshape(lead + (4 * ML_HEADS,))

    def place(a, col):
        return jnp.pad(a, [(0, 0)] * len(lead) + [(col, C_IN - col - a.shape[-1])])

    return (place(w[..., o_ml:o_gi], C_MLQ) + place(w[..., :o_rope], C_QLAT)
            + place(_rope_cols(w[..., o_rope:o_ml]), C_KROPE) + place(gates, C_GATE0) + place(w[..., o_na:], C_NAQ))


def _prep_w_uq(w):
    w = w.reshape(w.shape[0], MLA_HEADS, MLA_NOPE + MLA_ROPE)
    out = jnp.concatenate([w[..., :MLA_NOPE], _rope_cols(w[..., MLA_NOPE:])], axis=-1)
    return out.reshape(w.shape[0], MLA_HEADS * MLA_DQ).astype(BF16)


def _gate_bias(i_bias, f_bias):
    g4 = jnp.stack([i_bias[0], i_bias[1], f_bias[0], f_bias[1]], axis=-1).astype(F32).reshape(4 * ML_HEADS)
    return jnp.pad(g4, (0, LANE - 4 * ML_HEADS)).reshape(1, LANE), g4.reshape(4 * ML_HEADS, 1)


def _rope_table(n_tokens):
    t = jnp.arange(n_tokens)
    row = (t // GRID_W).astype(F32)
    col = (t % GRID_W).astype(F32)
    n_freq = MLA_ROPE // 4
    inv = ROPE_BASE ** (-jnp.arange(n_freq, dtype=F32) / n_freq)
    ang = jnp.concatenate([row[:, None] * inv, col[:, None] * inv], axis=-1)
    cos, sin = jnp.cos(ang), jnp.sin(ang)
    return jnp.concatenate([cos, cos, -sin, sin], axis=-1)


def kernel(x, c, ctx, c_ctx, ada_w, ada_b, norm1_g, norm2_g, w_in, mla_q_norm, mla_kv_norm, mla_w_uq, mla_w_uk, mla_w_uv, ml_conv_w, ml_conv_b, ml_i_bias, ml_f_bias, na_rpb, w_out, ffn_w_up, ffn_conv_w, ffn_conv_b, ffn_w_down, final_norm_g):
    batch, seq, d = x.shape
    tctx = ctx.shape[1]
    depth = ada_w.shape[0]
    assert d == D_MODEL and batch <= 7 and tctx == ML_L and tctx % HALO == 0
    assert seq % TM_BIG == 0 and seq % NA_BLK == 0 and seq // GRID_W >= NA_KR

    h = x.reshape(batch * seq, d)
    hc = ctx.reshape(batch * tctx, d)
    row_lat = lambda i, tm: i // (seq // tm)
    row_ctx = lambda i, tm: batch
    cond8 = jnp.zeros((8, d), F32).at[:batch].set(c).at[batch].set(c_ctx)

    tab_lat = _rope_table(seq)
    tab_ctx = jnp.concatenate([jnp.ones((tctx, MLA_ROPE), F32), jnp.zeros((tctx, MLA_ROPE), F32)], axis=-1)
    ml_scale = jnp.concatenate([jnp.ones((1, ML_W), F32), jnp.full((1, ML_W), ML_DH ** -0.5, F32)], axis=-1)

    w_in_p = _prep_w_in(w_in)
    w_out_p = w_out.astype(BF16)
    w2_p = _pad_heads(w_out_p[:, MLA_W:MLA_W + ML_W], ML_HEADS, ML_DH, ML_DP, 1)
    w_up_p = ffn_w_up.astype(BF16)
    w_down_p = ffn_w_down.astype(BF16)
    fcb = ffn_conv_b.reshape(depth, 1, -1)
    na_bias = _na_bias_table(na_rpb)

    for l in range(depth):
        last = l == depth - 1
        mod4 = _ada(cond8, ada_w, ada_b, l).reshape(8, N_MOD, 1, d)

        wuq = _prep_w_uq(mla_w_uq[l])
        wuk = mla_w_uk[l].astype(BF16)
        wuv = mla_w_uv[l].astype(BF16)
        qg = mla_q_norm[l].reshape(1, -1)
        kvg = mla_kv_norm[l].reshape(1, -1)
        cw = ml_conv_w[l]
        cb = ml_conv_b[l].reshape(1, -1)
        bias_col, bias_row = _gate_bias(ml_i_bias[l], ml_f_bias[l])

        proj_c, gates_c, grow_c = _nm_mm(hc, norm1_g[l], mod4, 1, 0, row_ctx, w_in_p, l, TN_IN, C_GATE0, C_GATES)
        proj_l, gates_l, grow_l = _nm_mm(h, norm1_g[l], mod4, 1, 0, row_lat, w_in_p, l, TN_IN, C_GATE0, C_GATES)

        q_c, k_c, v_c = _mla_up(proj_c, tab_ctx, qg, kvg, wuq, wuk, wuv)
        q_l, k_l, v_l = _mla_up(proj_l, tab_lat, qg, kvg, wuq, wuk, wuv)
        mla_lat = _flash(q_l, 0, k_c, 0, v_c, 0, MLA_HEADS, MLA_DQ, MLA_V, batch, lat=(k_l, 0, v_l, 0))

        qkv_c = _ml_prep(proj_c, cw, cb, ml_scale, tctx)
        qkv_l = _ml_prep(proj_l, cw, cb, ml_scale, seq)
        c0 = jnp.zeros((batch, ML_HEADS, 2, ML_DP, ML_DP), F32)
        m0 = jnp.zeros((batch, ML_HEADS, 2, 8, LANE), F32)
        hf_c, hb_c, c1, m1 = _mlstm(qkv_c, gates_c, grow_c, bias_col, bias_row, c0, m0, batch)
        hf_l, hb_l, _, _ = _mlstm(qkv_l, gates_l, grow_l, bias_col, bias_row, c1, m1, batch)

        na_lat = _na(proj_l, proj_c, na_bias, l, batch)

        h = _outproj(mla_lat, hf_l, hb_l, proj_l, na_lat, w_out_p, w2_p, l, h, mod4, 2, row_lat)
        h = _ffn(h, norm2_g[l], mod4, row_lat, w_up_p, ffn_conv_w, fcb, w_down_p, l, seq,
                 final_g=final_norm_g if last else None)

        if not last:
            mla_ctx = _flash(q_c, 0, k_c, 0, v_c, 0, MLA_HEADS, MLA_DQ, MLA_V, batch)
            na_ctx = _flash(proj_c, C_NAQ // NA_DH, proj_c, C_NAK // NA_DH, proj_c, C_NAV // NA_DH,
                            NA_HEADS, NA_DH, NA_DH, batch, scale=NA_DH ** -0.5 * LOG2E)
            hc = _outproj(mla_ctx, hf_c, hb_c, proj_c, na_ctx, w_out_p, w2_p, l, hc, mod4, 2, row_ctx)
            hc = _ffn(hc, norm2_g[l], mod4, row_ctx, w_up_p, ffn_conv_w, fcb, w_down_p, l, tctx)

    return h.reshape(batch, seq, d)
```

```python
import functools
import math

import numpy as np
import jax
import jax.numpy as jnp
from jax import lax
from jax.experimental import pallas as pl
from jax.experimental.pallas import tpu as pltpu

F32 = jnp.float32
BF16 = jnp.bfloat16

D_MODEL = 2048
GRID_W = 64
MLA_HEADS = 6
MLA_NOPE = 128
MLA_ROPE = 64
MLA_V = 128
MLA_Q_LORA = 512
MLA_KV_LORA = 256
ML_HEADS = 4
ML_DH = 192
NA_HEADS = 4
NA_DH = 128
NA_KR = 8
NA_KC = 16
ROPE_BASE = 10000.0
EPS = 1e-6
N_MOD = 6
ML_W = ML_HEADS * ML_DH
NA_W = NA_HEADS * NA_DH
MLA_W = MLA_HEADS * MLA_V
IN_SIZES = (MLA_Q_LORA, MLA_KV_LORA, MLA_ROPE, ML_W, ML_W, ML_W, ML_W, 2 * ML_HEADS, 2 * ML_HEADS, NA_W, NA_W, NA_W)
IN_OFFSETS = tuple(sum(IN_SIZES[:i + 1]) for i in range(len(IN_SIZES) - 1))
LOG2E = math.log2(math.e)

LANE = 128
ML_DP = 256
ML_WP = ML_HEADS * ML_DP
ML_L = 256
MLA_DQ = 256
TM = 512
TM_BIG = 1024
ROWS_BF16 = 16
HALO = ROWS_BF16
VMEM_LIMIT = 58 * 1024 * 1024

C_MLQ = 0
C_MLK = C_MLQ + ML_W
C_MLV = C_MLK + ML_W
C_MLO = C_MLV + ML_W
C_QLAT = C_MLO + ML_W
C_KVLAT = C_QLAT + MLA_Q_LORA
C_KROPE = C_KVLAT + MLA_KV_LORA
C_GATE0 = C_KROPE + LANE
C_GATES = LANE
C_NAQ = C_GATE0 + C_GATES
C_NAK = C_NAQ + NA_W
C_NAV = C_NAK + NA_W
C_IN = C_NAV + NA_W
TN_IN = 512
ML_PAIR = 2 * ML_DH


def _cp(sem):
    return pltpu.CompilerParams(dimension_semantics=sem, vmem_limit_bytes=VMEM_LIMIT)


def _sigmoid(x):
    return 1.0 / (1.0 + jnp.exp(-x))


def _rms(x, g):
    return x * lax.rsqrt(jnp.mean(x * x, axis=-1, keepdims=True) + EPS) * g


def _norm_mod_rows(x_ref, dst_ref, dst_row0, g_ref, sc_ref, sh_ref):
    gs = g_ref[...] * (1.0 + sc_ref[...])
    sh = sh_ref[...]

    def body(r, carry):
        row = pl.multiple_of(r * ROWS_BF16, ROWS_BF16)
        x = x_ref[pl.ds(row, ROWS_BF16), :]
        y = x * lax.rsqrt(jnp.mean(x * x, axis=-1, keepdims=True) + EPS)
        dst_ref[pl.ds(dst_row0 + row, ROWS_BF16), :] = (y * gs + sh).astype(BF16)
        return carry

    lax.fori_loop(0, x_ref.shape[0] // ROWS_BF16, body, 0, unroll=min(16, x_ref.shape[0] // ROWS_BF16))


def _mod_spec(chunk, rowf, tm, width=D_MODEL):
    return pl.BlockSpec((None, None, 1, width), lambda i, j: (rowf(i, tm), chunk, 0, j if width != D_MODEL else 0))


def _ada_kernel(c_ref, w_ref, b_ref, o_ref):
    c = c_ref[...]
    a = c * _sigmoid(c)
    o_ref[...] = jnp.dot(a, w_ref[...], preferred_element_type=F32,
                         precision=lax.Precision.HIGHEST) + b_ref[...]


def _ada(cond8, w, b, l):
    _, d, n = w.shape
    tn = 1024
    return pl.pallas_call(
        _ada_kernel,
        grid=(n // tn,),
        in_specs=[pl.BlockSpec((8, d), lambda j: (0, 0)),
                  pl.BlockSpec((None, d, tn), lambda j: (l, 0, j)),
                  pl.BlockSpec((None, 1, tn), lambda j: (l, 0, j))],
        out_specs=pl.BlockSpec((8, tn), lambda j: (0, j)),
        out_shape=jax.ShapeDtypeStruct((8, n), F32),
        compiler_params=_cp(("arbitrary",)),
    )(cond8, w, b.reshape(b.shape[0], 1, n))


def _nm_mm_kernel(x_ref, g_ref, sc_ref, sh_ref, w_ref, o_ref, t_ref, tt_ref, hn_ref, *, f32_col):
    j = pl.program_id(1)
    tn = w_ref.shape[1]

    @pl.when(j == 0)
    def _():
        _norm_mod_rows(x_ref, hn_ref, 0, g_ref, sc_ref, sh_ref)

    acc = jnp.dot(hn_ref[...], w_ref[...], preferred_element_type=F32)
    o_ref[...] = acc.astype(o_ref.dtype)

    @pl.when(j == f32_col // tn)
    def _():
        t = acc[:, f32_col % tn:f32_col % tn + t_ref.shape[1]]
        t_ref[...] = t
        tt_ref[...] = t.T


def _nm_mm(x, g, mod4, sc_chunk, sh_chunk, rowf, w, l, tn, f32_col, f32_cols):
    m, d = x.shape
    n = w.shape[2]
    tm = min(TM_BIG, m)
    return pl.pallas_call(
        functools.partial(_nm_mm_kernel, f32_col=f32_col),
        grid=(m // tm, n // tn),
        in_specs=[pl.BlockSpec((tm, d), lambda i, j: (i, 0)),
                  pl.BlockSpec((1, d), lambda i, j: (0, 0)),
                  _mod_spec(sc_chunk, rowf, tm), _mod_spec(sh_chunk, rowf, tm),
                  pl.BlockSpec((None, d, tn), lambda i, j: (l, 0, j))],
        out_specs=[pl.BlockSpec((tm, tn), lambda i, j: (i, j)),
                   pl.BlockSpec((tm, f32_cols), lambda i, j: (i, 0)),
                   pl.BlockSpec((f32_cols, tm), lambda i, j: (0, i))],
        out_shape=[jax.ShapeDtypeStruct((m, n), BF16), jax.ShapeDtypeStruct((m, f32_cols), F32),
                   jax.ShapeDtypeStruct((f32_cols, m), F32)],
        scratch_shapes=[pltpu.VMEM((tm, d), BF16)],
        compiler_params=_cp(("parallel", "arbitrary")),
    )(x, g.reshape(1, d), mod4, mod4, w)


def _mla_up_kernel(ql_ref, kvl_ref, kr_ref, tab_ref, qg_ref, kvg_ref, wuq_ref, wuk_ref, wuv_ref,
                   q_ref, k_ref, v_ref, *, scale):
    tab = tab_ref[...]
    qn = _rms(ql_ref[...].astype(F32), qg_ref[...]).astype(BF16)
    qf = jnp.dot(qn, wuq_ref[...], preferred_element_type=F32)
    for h in range(MLA_HEADS):
        c0 = h * MLA_DQ
        y = qf[:, c0 + MLA_NOPE:c0 + MLA_DQ] * tab
        z = y + pltpu.roll(y, MLA_ROPE, axis=1)
        q_ref[:, c0:c0 + MLA_NOPE] = (qf[:, c0:c0 + MLA_NOPE] * scale).astype(BF16)
        q_ref[:, c0 + MLA_NOPE:c0 + MLA_DQ] = (z * scale).astype(BF16)
    cn = _rms(kvl_ref[...].astype(F32), kvg_ref[...]).astype(BF16)
    kn = jnp.dot(cn, wuk_ref[...], preferred_element_type=F32)
    v_ref[...] = jnp.dot(cn, wuv_ref[...], preferred_element_type=F32).astype(BF16)
    ky = kr_ref[...].astype(F32) * tab
    kz = ky + pltpu.roll(ky, MLA_ROPE, axis=1)
    lane = lax.broadcasted_iota(jnp.int32, kz.shape, 1)
    kz = jnp.where(lane < MLA_ROPE, kz, 0.0).astype(BF16)
    for h in range(MLA_HEADS):
        c0 = h * MLA_DQ
        k_ref[:, c0:c0 + MLA_NOPE] = kn[:, h * MLA_NOPE:(h + 1) * MLA_NOPE].astype(BF16)
        k_ref[:, c0 + MLA_NOPE:c0 + MLA_DQ] = kz


def _mla_up(proj, tab, qg, kvg, wuq, wuk, wuv):
    m = proj.shape[0]
    tm = min(TM, tab.shape[0])
    nt = tab.shape[0] // tm
    full = lambda a: pl.BlockSpec(a.shape, lambda i: (0, 0))
    return pl.pallas_call(
        functools.partial(_mla_up_kernel, scale=(MLA_NOPE + MLA_ROPE) ** -0.5 * LOG2E),
        grid=(m // tm,),
        in_specs=[pl.BlockSpec((tm, MLA_Q_LORA), lambda i: (i, C_QLAT // MLA_Q_LORA)),
                  pl.BlockSpec((tm, MLA_KV_LORA), lambda i: (i, C_KVLAT // MLA_KV_LORA)),
                  pl.BlockSpec((tm, LANE), lambda i: (i, C_KROPE // LANE)),
                  pl.BlockSpec((tm, LANE), lambda i: (i % nt, 0)),
                  full(qg), full(kvg), full(wuq), full(wuk), full(wuv)],
        out_specs=[pl.BlockSpec((tm, MLA_HEADS * MLA_DQ), lambda i: (i, 0)),
                   pl.BlockSpec((tm, MLA_HEADS * MLA_DQ), lambda i: (i, 0)),
                   pl.BlockSpec((tm, MLA_W), lambda i: (i, 0))],
        out_shape=[jax.ShapeDtypeStruct((m, MLA_HEADS * MLA_DQ), BF16),
                   jax.ShapeDtypeStruct((m, MLA_HEADS * MLA_DQ), BF16),
                   jax.ShapeDtypeStruct((m, MLA_W), BF16)],
        compiler_params=_cp(("parallel",)),
    )(proj, proj, proj, tab, qg, kvg, wuq, wuk, wuv)


def _flash_kernel(*refs, has_lat, scale, nsub, tk):
    if has_lat:
        q_ref, kc_ref, vc_ref, kl_ref, vl_ref, o_ref, m_scr, acc_scr = refs
    else:
        q_ref, kc_ref, vc_ref, o_ref, m_scr, acc_scr = refs
    ts = q_ref.shape[0] // nsub
    dv = vc_ref.shape[1]
    dn = (((1,), (1,)), ((), ()))

    def scores(q, k):
        s = lax.dot_general(q, k, dn, preferred_element_type=F32)
        return s if scale == 1.0 else s * scale

    def with_ones(v):
        lane = lax.broadcasted_iota(jnp.int32, v.shape, 1)
        return jnp.concatenate([v, jnp.where(lane == 0, 1.0, 0.0).astype(v.dtype)], axis=1)

    def lanes(x, n):
        return x if n == LANE else pltpu.repeat(x, n // LANE, axis=1)

    def probs(s, m):
        return jnp.exp2(s - lanes(m, s.shape[1])).astype(BF16)

    kc = kc_ref[...]
    vc = with_ones(vc_ref[...])
    for u in range(nsub):
        s = scores(q_ref[u * ts:(u + 1) * ts, :], kc)
        m = jnp.broadcast_to(jnp.max(s, axis=-1, keepdims=True), (ts, LANE))
        m_scr[u] = m
        acc_scr[u] = jnp.dot(probs(s, m), vc, preferred_element_type=F32)

    if has_lat:
        def body(kb, carry):
            off = pl.multiple_of(kb * tk, tk)
            k = kl_ref[pl.ds(off, tk), :]
            v = with_ones(vl_ref[pl.ds(off, tk), :])
            s_next = scores(q_ref[0:ts, :], k)
            for u in range(nsub):
                s = s_next
                if u + 1 < nsub:
                    s_next = scores(q_ref[(u + 1) * ts:(u + 2) * ts, :], k)
                m_prev = m_scr[u]
                m_new = jnp.maximum(m_prev, jnp.max(s, axis=-1, keepdims=True))
                m_scr[u] = m_new
                acc_scr[u] = (lanes(jnp.exp2(m_prev - m_new), 2 * dv) * acc_scr[u]
                              + jnp.dot(probs(s, m_new), v, preferred_element_type=F32))
            return carry

        lax.fori_loop(0, kl_ref.shape[0] // tk, body, 0)

    for u in range(nsub):
        acc = acc_scr[u]
        o_ref[u * ts:(u + 1) * ts, :] = (acc[:, :dv] / acc[:, dv:dv + 1]).astype(o_ref.dtype)


def _flash(q, qcol, kc, kccol, vc, vccol, heads, dq, dv, batch, lat=None, scale=1.0, tq=2048, nsub=2, tk=2048):
    mq = q.shape[0]
    sq = mq // batch
    tq = min(tq, sq)
    nsub = min(nsub, tq // 256)
    nq = sq // tq
    tc = kc.shape[0] // batch
    has_lat = lat is not None
    in_specs = [pl.BlockSpec((tq, dq), lambda b, h, i: (b * nq + i, qcol + h)),
                pl.BlockSpec((tc, dq), lambda b, h, i: (b, kccol + h)),
                pl.BlockSpec((tc, dv), lambda b, h, i: (b, vccol + h))]
    args = [q, kc, vc]
    if has_lat:
        kl, klcol, vl, vlcol = lat
        sk = kl.shape[0] // batch
        tk = min(tk, sk)
        in_specs += [pl.BlockSpec((sk, dq), lambda b, h, i: (b, klcol + h)),
                     pl.BlockSpec((sk, dv), lambda b, h, i: (b, vlcol + h))]
        args += [kl, vl]
    return pl.pallas_call(
        functools.partial(_flash_kernel, has_lat=has_lat, scale=scale, nsub=nsub, tk=tk),
        grid=(batch, heads, nq),
        in_specs=in_specs,
        out_specs=pl.BlockSpec((tq, dv), lambda b, h, i: (b * nq + i, h)),
        out_shape=jax.ShapeDtypeStruct((mq, heads * dv), BF16),
        scratch_shapes=[pltpu.VMEM((nsub, tq // nsub, LANE), F32), pltpu.VMEM((nsub, tq // nsub, 2 * dv), F32)],
        compiler_params=_cp(("parallel", "parallel", "arbitrary")),
    )(*args)


NA_RB = 8
NA_BLK = NA_RB * GRID_W
NA_GROUP = 2


def _na_kernel(q_ref, kp_ref, kc_ref, kn_ref, vp_ref, vc_ref, vn_ref, kx_ref, vx_ref, bias_ref, o_ref,
               kw_scr, vw_scr, *, rows, scale):
    j = pl.program_id(1)
    for t, (kr, vr) in enumerate(((kp_ref, vp_ref), (kc_ref, vc_ref), (kn_ref, vn_ref))):
        kw_scr[t * NA_BLK:(t + 1) * NA_BLK, :] = kr[...]
        vw_scr[t * NA_BLK:(t + 1) * NA_BLK, :] = vr[...]
    nwin = NA_KR * GRID_W
    dn = (((1,), (1,)), ((), ()))
    heads = [slice(h * NA_DH, (h + 1) * NA_DH) for h in range(NA_HEADS)]
    for i0 in range(0, NA_RB, NA_GROUP):
        units = []
        for i in range(i0, i0 + NA_GROUP):
            r = NA_RB * j + i
            r0 = jnp.clip(r - NA_KR // 2, 0, rows - NA_KR)
            start = pl.multiple_of((r0 - NA_RB * j + NA_RB) * GRID_W, GRID_W)
            for h, cs in enumerate(heads):
                q = q_ref[i * GRID_W:(i + 1) * GRID_W, cs]
                s_loc = (lax.dot_general(q, kw_scr[pl.ds(start, nwin), cs], dn, preferred_element_type=F32) * scale
                         + bias_ref[h, r - r0])
                s_ctx = lax.dot_general(q, kx_ref[:, cs], dn, preferred_element_type=F32) * scale
                units.append((i, cs, start, s_loc, s_ctx))
        outs = {}
        for i, cs, start, s_loc, s_ctx in units:
            m = jnp.maximum(jnp.max(s_loc, axis=-1, keepdims=True), jnp.max(s_ctx, axis=-1, keepdims=True))
            p_loc = jnp.exp(s_loc - m)
            p_ctx = jnp.exp(s_ctx - m)
            l = jnp.sum(p_loc, axis=-1, keepdims=True) + jnp.sum(p_ctx, axis=-1, keepdims=True)
            o = (jnp.dot(p_loc.astype(BF16), vw_scr[pl.ds(start, nwin), cs], preferred_element_type=F32)
                 + jnp.dot(p_ctx.astype(BF16), vx_ref[:, cs], preferred_element_type=F32))
            outs.setdefault(i, []).append((o / l).astype(o_ref.dtype))
        for i, parts in outs.items():
            o_ref[i * GRID_W:(i + 1) * GRID_W, :] = jnp.concatenate(parts, axis=1)


def _na_bias_table(rpb):
    c = np.arange(GRID_W)
    cs = np.clip(c - NA_KC // 2, 0, GRID_W - NA_KC)
    valid = (c[None, :] >= cs[:, None]) & (c[None, :] < cs[:, None] + NA_KC)
    dc = np.clip(c[None, :] - c[:, None] + (NA_KC - 1), 0, 2 * NA_KC - 2)
    dr = np.arange(NA_KR)[None, :] - np.arange(NA_KR)[:, None] + (NA_KR - 1)
    sel_r = (dr[:, :, None] == np.arange(2 * NA_KR - 1)).astype(np.float32)
    sel_c = ((dc[:, :, None] == np.arange(2 * NA_KC - 1)) & valid[:, :, None]).astype(np.float32)
    tab = jnp.einsum('lhab,via,cdb->lhvcid', rpb.astype(F32), sel_r, sel_c, precision=lax.Precision.HIGHEST)
    tab = tab + jnp.where(valid[None, None, None, :, None, :], 0.0, -1e30)
    return tab.reshape(rpb.shape[0], NA_HEADS, NA_KR, GRID_W, NA_KR * GRID_W)


def _na(proj_l, proj_c, bias, l, batch):
    m = proj_l.shape[0]
    s = m // batch
    rows = s // GRID_W
    nb = rows // NA_RB
    cq, ck, cv = C_NAQ // NA_W, C_NAK // NA_W, C_NAV // NA_W
    tc = proj_c.shape[0] // batch
    prev = lambda b, j: b * nb + jnp.maximum(j - 1, 0)
    cur = lambda b, j: b * nb + j
    nxt = lambda b, j: b * nb + jnp.minimum(j + 1, nb - 1)
    blk = lambda rf, col: pl.BlockSpec((NA_BLK, NA_W), lambda b, j: (rf(b, j), col))
    return pl.pallas_call(
        functools.partial(_na_kernel, rows=rows, scale=NA_DH ** -0.5),
        grid=(batch, nb),
        in_specs=[blk(cur, cq), blk(prev, ck), blk(cur, ck), blk(nxt, ck),
                  blk(prev, cv), blk(cur, cv), blk(nxt, cv),
                  pl.BlockSpec((tc, NA_W), lambda b, j: (b, ck)),
                  pl.BlockSpec((tc, NA_W), lambda b, j: (b, cv)),
                  pl.BlockSpec((None,) + bias.shape[1:], lambda b, j: (l, 0, 0, 0, 0))],
        out_specs=pl.BlockSpec((NA_BLK, NA_W), lambda b, j: (b * nb + j, 0)),
        out_shape=jax.ShapeDtypeStruct((m, NA_W), BF16),
        scratch_shapes=[pltpu.VMEM((3 * NA_BLK, NA_W), BF16), pltpu.VMEM((3 * NA_BLK, NA_W), BF16)],
        compiler_params=_cp(("parallel", "arbitrary")),
    )(proj_l, proj_l, proj_l, proj_l, proj_l, proj_l, proj_l, proj_c, proj_c, bias)


def _ml_prep_kernel(x_ref, p_ref, n_ref, w_ref, b_ref, sc_ref, o_ref, *, tiles_per_seg, n_conv):
    j = pl.program_id(1)
    x = x_ref[...].astype(F32)
    tm = x.shape[0]

    def widen(y, fill):
        lane = lax.broadcasted_iota(jnp.int32, (tm, ML_DP - ML_DH), 1)
        spare = jnp.where(lane == 0, fill, 0.0)
        return jnp.concatenate([y[:, :ML_DH], spare, y[:, ML_DH:], spare], axis=1).astype(o_ref.dtype)

    @pl.when(j < n_conv)
    def _():
        t = pl.program_id(0) % tiles_per_seg
        prev_row = jnp.where(t == 0, 0.0, p_ref[HALO - 1:HALO, :].astype(F32))
        next_row = jnp.where(t == tiles_per_seg - 1, 0.0, n_ref[0:1, :].astype(F32))
        rid = lax.broadcasted_iota(jnp.int32, x.shape, 0)
        xp = jnp.where(rid == 0, prev_row, pltpu.roll(x, 1, axis=0))
        xn = jnp.where(rid == tm - 1, next_row, pltpu.roll(x, tm - 1, axis=0))
        w = w_ref[...]
        y = xp * w[0:1, :] + x * w[1:2, :] + xn * w[2:3, :] + b_ref[...]
        o_ref[...] = widen(y * _sigmoid(y) * sc_ref[...], 0.0)

    @pl.when(j >= n_conv)
    def _():
        o_ref[...] = widen(x, 1.0)


def _ml_prep(proj, w, b, colscale, seg):
    m = proj.shape[0]
    tm = min(TM_BIG, seg)
    hb = tm // HALO
    last = m // HALO - 1
    n_conv = w.shape[1] // ML_PAIR
    n_all = 3 * ML_W // ML_PAIR
    vec = lambda r: pl.BlockSpec((r, ML_PAIR), lambda i, j: (0, jnp.minimum(j, n_conv - 1)))
    return pl.pallas_call(
        functools.partial(_ml_prep_kernel, tiles_per_seg=seg // tm, n_conv=n_conv),
        grid=(m // tm, n_all),
        in_specs=[pl.BlockSpec((tm, ML_PAIR), lambda i, j: (i, j)),
                  pl.BlockSpec((HALO, ML_PAIR), lambda i, j: (jnp.maximum(i * hb - 1, 0), j)),
                  pl.BlockSpec((HALO, ML_PAIR), lambda i, j: (jnp.minimum((i + 1) * hb, last), j)),
                  vec(3), vec(1), vec(1)],
        out_specs=pl.BlockSpec((tm, 2 * ML_DP), lambda i, j: (i, j)),
        out_shape=jax.ShapeDtypeStruct((m, 3 * ML_WP), BF16),
        compiler_params=_cp(("parallel", "parallel")),
    )(proj, proj, proj, w, b, colscale)


def _split3(x):
    x1 = x.astype(BF16)
    r = x - x1.astype(F32)
    x2 = r.astype(BF16)
    x3 = (r - x2.astype(F32)).astype(BF16)
    return x1, x2, x3


def _log_sigmoid(x):
    return jnp.minimum(x, 0.0) - jnp.log(1.0 + jnp.exp(-jnp.abs(x)))


def _mlstm_dir(d, q_ref, k_ref, v_ref, gc_ref, gr_ref, bc_ref, br_ref, h_ref, c_scr, m_scr):
    L = ML_L
    ri = lax.broadcasted_iota(jnp.int32, (L, L), 0)
    ci = lax.broadcasted_iota(jnp.int32, (L, L), 1)
    mask = (ci <= ri) if d == 0 else (ci >= ri)
    tri = jnp.where(mask, 1.0, 0.0).astype(BF16)
    tri_t = jnp.where((ri <= ci) if d == 0 else (ri >= ci), 1.0, 0.0).astype(BF16)
    lane = lax.broadcasted_iota(jnp.int32, (L, ML_DP), 1)

    gc = gc_ref[...] + bc_ref[...]
    gr = gr_ref[...] + br_ref[...]
    bc_all = sum(jnp.dot(tri, p, preferred_element_type=F32) for p in _split3(_log_sigmoid(gc)))
    br_all = sum(jnp.dot(p, tri_t, preferred_element_type=F32) for p in _split3(_log_sigmoid(gr)))

    hs = range(ML_HEADS)
    cols = [slice(h * ML_DP, (h + 1) * ML_DP) for h in hs]
    q = [q_ref[:, cs] for cs in cols]
    k = [k_ref[:, cs] for cs in cols]
    v = [v_ref[:, cs] for cs in cols]
    m_prev = [m_scr[h, d][0:1, 0:1] for h in hs]
    ct = [c_scr[h, d] for h in hs]
    qk = [lax.dot_general(q[h], k[h], (((1,), (1,)), ((), ())), preferred_element_type=F32) for h in hs]
    inter = [jnp.dot(q[h], ct[h].astype(BF16), preferred_element_type=F32) for h in hs]

    rep = lambda col: jnp.broadcast_to(col, (L, LANE))
    wide = lambda x: pltpu.repeat(x, ML_DP // LANE, axis=1)
    b_rep = [rep(bc_all[:, 4 * h + 2 + d:4 * h + 3 + d]) for h in hs]
    li_rep = [rep(gc[:, 4 * h + d:4 * h + d + 1]) for h in hs]
    a, w_st, m_t = [], [], []
    for h in hs:
        b_row = br_all[4 * h + 2 + d:4 * h + 3 + d, :]
        li_row = gr[4 * h + d:4 * h + d + 1, :]
        dmat = jnp.where(mask, wide(b_rep[h]) - b_row + li_row, -jnp.inf)
        g = b_rep[h] + m_prev[h]
        m_t.append(jnp.maximum(g, jnp.max(dmat, axis=-1, keepdims=True)))
        w_st.append(jnp.exp(g - m_t[h]))
        a.append((qk[h] * jnp.exp(dmat - wide(m_t[h]))).astype(BF16))

    for h in hs:
        num = wide(w_st[h]) * inter[h] + jnp.dot(a[h], v[h], preferred_element_type=F32)
        den = rep(num[:, ML_DH:ML_DH + 1])
        inv = 1.0 / jnp.maximum(jnp.abs(den), jnp.exp(-m_t[h]))
        h_ref[:, cols[h]] = jnp.where(lane < ML_DH, num * wide(inv), 0.0).astype(h_ref.dtype)

    for h in hs:
        b_end = b_rep[h][L - 1:L, :] if d == 0 else b_rep[h][0:1, :]
        e_log = b_end - b_rep[h] + li_rep[h]
        m_new = jnp.maximum(b_end + m_prev[h], jnp.max(e_log, axis=0, keepdims=True))
        decay = jnp.exp(b_end + m_prev[h] - m_new)[:, 0:1]
        ek = (wide(jnp.exp(e_log - m_new)) * k[h].astype(F32)).astype(BF16)
        c_scr[h, d] = decay * ct[h] + lax.dot_general(ek, v[h], (((0,), (0,)), ((), ())),
                                                      preferred_element_type=F32)
        m_scr[h, d] = jnp.broadcast_to(m_new, m_scr.shape[2:])


def _mlstm_kernel(qf_ref, kf_ref, vf_ref, gcf_ref, grf_ref, qb_ref, kb_ref, vb_ref, gcb_ref, grb_ref,
                  bc_ref, br_ref, c0_ref, m0_ref, hf_ref, hb_ref, c1_ref, m1_ref, c_scr, m_scr):
    c = pl.program_id(1)

    @pl.when(c == 0)
    def _():
        c_scr[...] = c0_ref[...]
        m_scr[...] = m0_ref[...]

    _mlstm_dir(0, qf_ref, kf_ref, vf_ref, gcf_ref, grf_ref, bc_ref, br_ref, hf_ref, c_scr, m_scr)
    _mlstm_dir(1, qb_ref, kb_ref, vb_ref, gcb_ref, grb_ref, bc_ref, br_ref, hb_ref, c_scr, m_scr)

    @pl.when(c == pl.num_programs(1) - 1)
    def _():
        c1_ref[...] = c_scr[...]
        m1_ref[...] = m_scr[...]


def _mlstm(qkv, gates, gates_row, bias_col, bias_row, c0, m0, batch):
    m = qkv.shape[0]
    s = m // batch
    nc = s // ML_L
    fwd = lambda b, c: b * nc + c
    bwd = lambda b, c: b * nc + (nc - 1 - c)

    def dir_specs(rf):
        return [pl.BlockSpec((ML_L, ML_WP), lambda b, c: (rf(b, c), 0)),
                pl.BlockSpec((ML_L, ML_WP), lambda b, c: (rf(b, c), 1)),
                pl.BlockSpec((ML_L, ML_WP), lambda b, c: (rf(b, c), 2)),
                pl.BlockSpec((ML_L, C_GATES), lambda b, c: (rf(b, c), 0)),
                pl.BlockSpec((4 * ML_HEADS, ML_L), lambda b, c: (0, rf(b, c)))]

    st_c = pl.BlockSpec((None, ML_HEADS, 2, ML_DP, ML_DP), lambda b, c: (b, 0, 0, 0, 0))
    st_m = pl.BlockSpec((None, ML_HEADS, 2, 8, LANE), lambda b, c: (b, 0, 0, 0, 0))
    return pl.pallas_call(
        _mlstm_kernel,
        grid=(batch, nc),
        in_specs=dir_specs(fwd) + dir_specs(bwd)
        + [pl.BlockSpec((1, C_GATES), lambda b, c: (0, 0)),
           pl.BlockSpec((4 * ML_HEADS, 1), lambda b, c: (0, 0)), st_c, st_m],
        out_specs=[pl.BlockSpec((ML_L, ML_WP), lambda b, c: (fwd(b, c), 0)),
                   pl.BlockSpec((ML_L, ML_WP), lambda b, c: (bwd(b, c), 0)), st_c, st_m],
        out_shape=[jax.ShapeDtypeStruct((m, ML_WP), BF16), jax.ShapeDtypeStruct((m, ML_WP), BF16),
                   jax.ShapeDtypeStruct(c0.shape, F32), jax.ShapeDtypeStruct(m0.shape, F32)],
        scratch_shapes=[pltpu.VMEM((ML_HEADS, 2, ML_DP, ML_DP), F32), pltpu.VMEM((ML_HEADS, 2, 8, LANE), F32)],
        compiler_params=_cp(("parallel", "arbitrary")),
    )(qkv, qkv, qkv, gates, gates_row, qkv, qkv, qkv, gates, gates_row, bias_col, bias_row, c0, m0)


def _outproj_kernel(mla_ref, hf_ref, hb_ref, og_ref, na_ref, w1_ref, w2_ref, w3_ref, res_ref, gate_ref,
                    out_ref, ml_scr):
    @pl.when(pl.program_id(1) == 0)
    def _():
        og = _sigmoid(og_ref[...].astype(F32))
        for h in range(ML_HEADS):
            cs = slice(h * ML_DP, (h + 1) * ML_DP)
            o_h = jnp.concatenate([og[:, h * ML_DH:(h + 1) * ML_DH],
                                   jnp.zeros((og.shape[0], ML_DP - ML_DH), F32)], axis=1)
            ml_scr[:, cs] = ((hf_ref[:, cs].astype(F32) + hb_ref[:, cs].astype(F32)) * o_h).astype(BF16)

    acc = (jnp.dot(mla_ref[...], w1_ref[...], preferred_element_type=F32)
           + jnp.dot(ml_scr[...], w2_ref[...], preferred_element_type=F32)
           + jnp.dot(na_ref[...], w3_ref[...], preferred_element_type=F32))
    out_ref[...] = res_ref[...] + gate_ref[...] * acc


def _outproj(mla, hf, hb, proj, na, w_out, w2, l, res, mod4, gate_chunk, rowf):
    m, d = res.shape
    tm = min(TM, m)
    tn = d
    rows = lambda width, col: pl.BlockSpec((tm, width), lambda i, j: (i, col))
    wrows = lambda nrows, blk: pl.BlockSpec((None, nrows, tn), lambda i, j: (l, blk, j),
                                            pipeline_mode=pl.Buffered(1))
    return pl.pallas_call(
        _outproj_kernel,
        grid=(m // tm, d // tn),
        in_specs=[rows(MLA_W, 0), rows(ML_WP, 0), rows(ML_WP, 0), rows(ML_W, C_MLO // ML_W), rows(NA_W, 0),
                  wrows(MLA_W, 0), wrows(ML_WP, 0), wrows(NA_W, (MLA_W + ML_W) // NA_W),
                  pl.BlockSpec((tm, tn), lambda i, j: (i, j)), _mod_spec(gate_chunk, rowf, tm, tn)],
        out_specs=pl.BlockSpec((tm, tn), lambda i, j: (i, j)),
        out_shape=jax.ShapeDtypeStruct((m, d), F32),
        scratch_shapes=[pltpu.VMEM((tm, ML_WP), BF16)],
        compiler_params=_cp(("parallel", "arbitrary")),
    )(mla, hf, hb, proj, na, w_out, w2, w_out, res, mod4)


def _ffn_kernel(x_ref, xp_ref, xn_ref, g_ref, sc_ref, sh_ref, wg_ref, wv_ref, cwg_ref, cwv_ref, cbg_ref, cbv_ref,
                wd_ref, gate_ref, *rest, tiles_per_seg, final):
    if final:
        fg_ref, o_ref, hn_scr = rest
    else:
        o_ref, hn_scr = rest
    i = pl.program_id(0)
    j = pl.program_id(1)
    tm = x_ref.shape[0]
    te = hn_scr.shape[0]

    @pl.when(j == 0)
    def _():
        _norm_mod_rows(xp_ref, hn_scr, 0, g_ref, sc_ref, sh_ref)
        _norm_mod_rows(x_ref, hn_scr, HALO, g_ref, sc_ref, sh_ref)
        _norm_mod_rows(xn_ref, hn_scr, HALO + tm, g_ref, sc_ref, sh_ref)
        o_ref[...] = jnp.zeros_like(o_ref)

    t = i % tiles_per_seg
    keep_prev = jnp.where(t == 0, 0.0, 1.0)
    keep_next = jnp.where(t == tiles_per_seg - 1, 0.0, 1.0)
    hn = hn_scr[...]

    def conv(w_ref, cw_ref, cb_ref):
        u = jnp.dot(hn, w_ref[...], preferred_element_type=F32)
        rid = lax.broadcasted_iota(jnp.int32, (tm, u.shape[1]), 0)
        up = pltpu.roll(u, 1, axis=0)[HALO:HALO + tm]
        un = pltpu.roll(u, te - 1, axis=0)[HALO:HALO + tm]
        up = jnp.where(rid == 0, up * keep_prev, up)
        un = jnp.where(rid == tm - 1, un * keep_next, un)
        cw = cw_ref[...]
        return up * cw[0:1, :] + u[HALO:HALO + tm] * cw[1:2, :] + un * cw[2:3, :] + cb_ref[...]

    gg = conv(wg_ref, cwg_ref, cbg_ref)
    vv = conv(wv_ref, cwv_ref, cbv_ref)
    act = (gg * _sigmoid(gg) * vv).astype(BF16)
    o_ref[...] += jnp.dot(act, wd_ref[...], preferred_element_type=F32)

    @pl.when(j == pl.num_programs(1) - 1)
    def _():
        y = x_ref[...] + gate_ref[...] * o_ref[...]
        o_ref[...] = _rms(y, fg_ref[...]) if final else y


def _ffn(x, g, mod4, rowf, w_up, cw, cb, w_down, l, seg, final_g=None, tf=512):
    m, d = x.shape
    dff = w_down.shape[1]
    tm = min(TM_BIG, seg)
    nj = dff // tf
    hb = tm // HALO
    last = m // HALO - 1
    final = final_g is not None
    vec = lambda r, off: pl.BlockSpec((None, r, tf), lambda i, j: (l, 0, off + j))
    once = pl.Buffered(1)
    in_specs = [pl.BlockSpec((tm, d), lambda i, j: (i, 0)),
                pl.BlockSpec((HALO, d), lambda i, j: (jnp.maximum(i * hb - 1, 0), 0)),
                pl.BlockSpec((HALO, d), lambda i, j: (jnp.minimum((i + 1) * hb, last), 0)),
                pl.BlockSpec((1, d), lambda i, j: (0, 0)),
                _mod_spec(4, rowf, tm), _mod_spec(3, rowf, tm),
                pl.BlockSpec((None, d, tf), lambda i, j: (l, 0, j)),
                pl.BlockSpec((None, d, tf), lambda i, j: (l, 0, nj + j)),
                vec(3, 0), vec(3, nj), vec(1, 0), vec(1, nj),
                pl.BlockSpec((None, tf, d), lambda i, j: (l, j, 0)), _mod_spec(5, rowf, tm)]
    args = [x, x, x, g.reshape(1, d), mod4, mod4, w_up, w_up, cw, cw, cb, cb, w_down, mod4]
    if final:
        in_specs.append(pl.BlockSpec((1, d), lambda i, j: (0, 0)))
        args.append(final_g.reshape(1, d))
    return pl.pallas_call(
        functools.partial(_ffn_kernel, tiles_per_seg=seg // tm, final=final),
        grid=(m // tm, nj),
        in_specs=in_specs,
        out_specs=pl.BlockSpec((tm, d), lambda i, j: (i, 0), pipeline_mode=once),
        out_shape=jax.ShapeDtypeStruct((m, d), F32),
        scratch_shapes=[pltpu.VMEM((tm + 2 * HALO, d), BF16)],
        compiler_params=_cp(("parallel", "arbitrary")),
    )(*args)


def _pad_heads(a, heads, dh, dp, axis):
    shp = a.shape
    a = a.reshape(shp[:axis] + (heads, dh) + shp[axis + 1:])
    pad = [(0, 0)] * a.ndim
    pad[axis + 1] = (0, dp - dh)
    a = jnp.pad(a, pad)
    return a.reshape(shp[:axis] + (heads * dp,) + shp[axis + 1:])


def _rope_cols(a):
    ev, od = a[..., 0::2], a[..., 1::2]
    return jnp.concatenate([ev, od, od, ev], axis=-1)


def _prep_w_in(w):
    w = w.astype(BF16)
    lead = w.shape[:-1]
    o_rope, o_ml, o_gi, o_na = IN_OFFSETS[1], IN_OFFSETS[2], IN_OFFSETS[6], IN_OFFSETS[8]
    gi = w[..., o_gi:o_gi + 2 * ML_HEADS].reshape(lead + (2, ML_HEADS))
    gf = w[..., o_gi + 2 * ML_HEADS:o_na].reshape(lead + (2, ML_HEADS))
    g4 = jnp.stack([gi[..., 0, :], gi[..., 1, :], gf[..., 0, :], gf[..., 1, :]], axis=-1)
    gates = g4.reshape(lead + (4 * ML_HEADS,))

    def place(a, col):
        return jnp.pad(a, [(0, 0)] * len(lead) + [(col, C_IN - col - a.shape[-1])])

    return (place(w[..., o_ml:o_gi], C_MLQ) + place(w[..., :o_rope], C_QLAT)
            + place(_rope_cols(w[..., o_rope:o_ml]), C_KROPE) + place(gates, C_GATE0) + place(w[..., o_na:], C_NAQ))


def _prep_w_uq(w):
    w = w.reshape(w.shape[0], MLA_HEADS, MLA_NOPE + MLA_ROPE)
    out = jnp.concatenate([w[..., :MLA_NOPE], _rope_cols(w[..., MLA_NOPE:])], axis=-1)
    return out.reshape(w.shape[0], MLA_HEADS * MLA_DQ).astype(BF16)


def _gate_bias(i_bias, f_bias):
    g4 = jnp.stack([i_bias[0], i_bias[1], f_bias[0], f_bias[1]], axis=-1).astype(F32).reshape(4 * ML_HEADS)
    return jnp.pad(g4, (0, LANE - 4 * ML_HEADS)).reshape(1, LANE), g4.reshape(4 * ML_HEADS, 1)


def _rope_table(n_tokens):
    t = jnp.arange(n_tokens)
    row = (t // GRID_W).astype(F32)
    col = (t % GRID_W).astype(F32)
    n_freq = MLA_ROPE // 4
    inv = ROPE_BASE ** (-jnp.arange(n_freq, dtype=F32) / n_freq)
    ang = jnp.concatenate([row[:, None] * inv, col[:, None] * inv], axis=-1)
    cos, sin = jnp.cos(ang), jnp.sin(ang)
    return jnp.concatenate([cos, cos, -sin, sin], axis=-1)


def kernel(x, c, ctx, c_ctx, ada_w, ada_b, norm1_g, norm2_g, w_in, mla_q_norm, mla_kv_norm, mla_w_uq, mla_w_uk, mla_w_uv, ml_conv_w, ml_conv_b, ml_i_bias, ml_f_bias, na_rpb, w_out, ffn_w_up, ffn_conv_w, ffn_conv_b, ffn_w_down, final_norm_g):
    batch, seq, d = x.shape
    tctx = ctx.shape[1]
    depth = ada_w.shape[0]
    assert d == D_MODEL and batch <= 7 and tctx == ML_L and tctx % HALO == 0
    assert seq % TM_BIG == 0 and seq % NA_BLK == 0 and seq // GRID_W >= NA_KR

    h = x.reshape(batch * seq, d)
    hc = ctx.reshape(batch * tctx, d)
    row_lat = lambda i, tm: i // (seq // tm)
    row_ctx = lambda i, tm: batch
    cond8 = jnp.zeros((8, d), F32).at[:batch].set(c).at[batch].set(c_ctx)

    tab_lat = _rope_table(seq)
    tab_ctx = jnp.concatenate([jnp.ones((tctx, MLA_ROPE), F32), jnp.zeros((tctx, MLA_ROPE), F32)], axis=-1)
    ml_scale = jnp.concatenate([jnp.ones((1, ML_W), F32), jnp.full((1, ML_W), ML_DH ** -0.5, F32)], axis=-1)

    w_in_p = _prep_w_in(w_in)
    w_out_p = w_out.astype(BF16)
    w2_p = _pad_heads(w_out_p[:, MLA_W:MLA_W + ML_W], ML_HEADS, ML_DH, ML_DP, 1)
    w_up_p = ffn_w_up.astype(BF16)
    w_down_p = ffn_w_down.astype(BF16)
    fcb = ffn_conv_b.reshape(depth, 1, -1)
    na_bias = _na_bias_table(na_rpb)

    for l in range(depth):
        last = l == depth - 1
        mod4 = _ada(cond8, ada_w, ada_b, l).reshape(8, N_MOD, 1, d)

        wuq = _prep_w_uq(mla_w_uq[l])
        wuk = mla_w_uk[l].astype(BF16)
        wuv = mla_w_uv[l].astype(BF16)
        qg = mla_q_norm[l].reshape(1, -1)
        kvg = mla_kv_norm[l].reshape(1, -1)
        cw = ml_conv_w[l]
        cb = ml_conv_b[l].reshape(1, -1)
        bias_col, bias_row = _gate_bias(ml_i_bias[l], ml_f_bias[l])

        proj_c, gates_c, grow_c = _nm_mm(hc, norm1_g[l], mod4, 1, 0, row_ctx, w_in_p, l, TN_IN, C_GATE0, C_GATES)
        proj_l, gates_l, grow_l = _nm_mm(h, norm1_g[l], mod4, 1, 0, row_lat, w_in_p, l, TN_IN, C_GATE0, C_GATES)

        q_c, k_c, v_c = _mla_up(proj_c, tab_ctx, qg, kvg, wuq, wuk, wuv)
        q_l, k_l, v_l = _mla_up(proj_l, tab_lat, qg, kvg, wuq, wuk, wuv)
        mla_lat = _flash(q_l, 0, k_c, 0, v_c, 0, MLA_HEADS, MLA_DQ, MLA_V, batch, lat=(k_l, 0, v_l, 0))

        qkv_c = _ml_prep(proj_c, cw, cb, ml_scale, tctx)
        qkv_l = _ml_prep(proj_l, cw, cb, ml_scale, seq)
        c0 = jnp.zeros((batch, ML_HEADS, 2, ML_DP, ML_DP), F32)
        m0 = jnp.zeros((batch, ML_HEADS, 2, 8, LANE), F32)
        hf_c, hb_c, c1, m1 = _mlstm(qkv_c, gates_c, grow_c, bias_col, bias_row, c0, m0, batch)
        hf_l, hb_l, _, _ = _mlstm(qkv_l, gates_l, grow_l, bias_col, bias_row, c1, m1, batch)

        na_lat = _na(proj_l, proj_c, na_bias, l, batch)

        h = _outproj(mla_lat, hf_l, hb_l, proj_l, na_lat, w_out_p, w2_p, l, h, mod4, 2, row_lat)
        h = _ffn(h, norm2_g[l], mod4, row_lat, w_up_p, ffn_conv_w, fcb, w_down_p, l, seq,
                 final_g=final_norm_g if last else None)

        if not last:
            mla_ctx = _flash(q_c, 0, k_c, 0, v_c, 0, MLA_HEADS, MLA_DQ, MLA_V, batch)
            na_ctx = _flash(proj_c, C_NAQ // NA_DH, proj_c, C_NAK // NA_DH, proj_c, C_NAV // NA_DH,
                            NA_HEADS, NA_DH, NA_DH, batch, scale=NA_DH ** -0.5 * LOG2E)
            hc = _outproj(mla_ctx, hf_c, hb_c, proj_c, na_ctx, w_out_p, w2_p, l, hc, mod4, 2, row_ctx)
            hc = _ffn(hc, norm2_g[l], mod4, row_ctx, w_up_p, ffn_conv_w, fcb, w_down_p, l, tctx)

    return h.reshape(batch, seq, d)
```

```python
import functools
import math

import numpy as np
import jax
import jax.numpy as jnp
from jax import lax
from jax.experimental import pallas as pl
from jax.experimental.pallas import tpu as pltpu

F32 = jnp.float32
BF16 = jnp.bfloat16

D_MODEL = 2048
GRID_W = 64
MLA_HEADS = 6
MLA_NOPE = 128
MLA_ROPE = 64
MLA_V = 128
MLA_Q_LORA = 512
MLA_KV_LORA = 256
ML_HEADS = 4
ML_DH = 192
NA_HEADS = 4
NA_DH = 128
NA_KR = 8
NA_KC = 16
ROPE_BASE = 10000.0
EPS = 1e-6
N_MOD = 6
ML_W = ML_HEADS * ML_DH
NA_W = NA_HEADS * NA_DH
MLA_W = MLA_HEADS * MLA_V
IN_SIZES = (MLA_Q_LORA, MLA_KV_LORA, MLA_ROPE, ML_W, ML_W, ML_W, ML_W, 2 * ML_HEADS, 2 * ML_HEADS, NA_W, NA_W, NA_W)
IN_OFFSETS = tuple(sum(IN_SIZES[:i + 1]) for i in range(len(IN_SIZES) - 1))
LOG2E = math.log2(math.e)

LANE = 128
ML_DP = 256
ML_WP = ML_HEADS * ML_DP
ML_L = 256
MLA_DQ = 256
TM = 512
TM_BIG = 1024
ROWS_F32 = 8
ROWS_BF16 = 16
HALO = ROWS_BF16
VMEM_LIMIT = 58 * 1024 * 1024

C_MLQ = 0
C_MLK = C_MLQ + ML_W
C_MLV = C_MLK + ML_W
C_MLO = C_MLV + ML_W
C_QLAT = C_MLO + ML_W
C_KVLAT = C_QLAT + MLA_Q_LORA
C_KROPE = C_KVLAT + MLA_KV_LORA
C_GATE0 = C_KROPE + LANE
C_GATES = LANE
C_NAQ = C_GATE0 + C_GATES
C_NAK = C_NAQ + NA_W
C_NAV = C_NAK + NA_W
C_IN = C_NAV + NA_W
TN_IN = 512
ML_PAIR = 2 * ML_DH


def _cp(sem):
    return pltpu.CompilerParams(dimension_semantics=sem, vmem_limit_bytes=VMEM_LIMIT)


def _sigmoid(x):
    return 1.0 / (1.0 + jnp.exp(-x))


def _rms(x, g):
    return x * lax.rsqrt(jnp.mean(x * x, axis=-1, keepdims=True) + EPS) * g


def _norm_mod_rows(x_ref, dst_ref, dst_row0, g_ref, sc_ref, sh_ref):
    gs = g_ref[...] * (1.0 + sc_ref[...])
    sh = sh_ref[...]

    def body(r, carry):
        row = pl.multiple_of(r * ROWS_BF16, ROWS_BF16)
        x = x_ref[pl.ds(row, ROWS_BF16), :]
        y = x * lax.rsqrt(jnp.mean(x * x, axis=-1, keepdims=True) + EPS)
        dst_ref[pl.ds(dst_row0 + row, ROWS_BF16), :] = (y * gs + sh).astype(BF16)
        return carry

    lax.fori_loop(0, x_ref.shape[0] // ROWS_BF16, body, 0, unroll=min(16, x_ref.shape[0] // ROWS_BF16))


def _mod_spec(chunk, rowf, tm, width=D_MODEL):
    return pl.BlockSpec((None, None, 1, width), lambda i, j: (rowf(i, tm), chunk, 0, j if width != D_MODEL else 0))


def _ada_kernel(c_ref, w_ref, b_ref, o_ref):
    c = c_ref[...]
    a = c * _sigmoid(c)
    o_ref[...] = jnp.dot(a, w_ref[...], preferred_element_type=F32,
                         precision=lax.Precision.HIGHEST) + b_ref[...]


def _ada(cond8, w, b, l):
    _, d, n = w.shape
    tn = 1024
    return pl.pallas_call(
        _ada_kernel,
        grid=(n // tn,),
        in_specs=[pl.BlockSpec((8, d), lambda j: (0, 0)),
                  pl.BlockSpec((None, d, tn), lambda j: (l, 0, j)),
                  pl.BlockSpec((None, 1, tn), lambda j: (l, 0, j))],
        out_specs=pl.BlockSpec((8, tn), lambda j: (0, j)),
        out_shape=jax.ShapeDtypeStruct((8, n), F32),
        compiler_params=_cp(("arbitrary",)),
    )(cond8, w, b.reshape(b.shape[0], 1, n))


def _nm_mm_kernel(x_ref, g_ref, sc_ref, sh_ref, w_ref, o_ref, t_ref, tt_ref, hn_ref, *, f32_col):
    j = pl.program_id(1)
    tn = w_ref.shape[1]

    @pl.when(j == 0)
    def _():
        _norm_mod_rows(x_ref, hn_ref, 0, g_ref, sc_ref, sh_ref)

    acc = jnp.dot(hn_ref[...], w_ref[...], preferred_element_type=F32)
    o_ref[...] = acc.astype(o_ref.dtype)

    @pl.when(j == f32_col // tn)
    def _():
        t = acc[:, f32_col % tn:f32_col % tn + t_ref.shape[1]]
        t_ref[...] = t
        tt_ref[...] = t.T


def _nm_mm(x, g, mod4, sc_chunk, sh_chunk, rowf, w, l, tn, f32_col, f32_cols):
    m, d = x.shape
    n = w.shape[2]
    tm = min(TM_BIG, m)
    return pl.pallas_call(
        functools.partial(_nm_mm_kernel, f32_col=f32_col),
        grid=(m // tm, n // tn),
        in_specs=[pl.BlockSpec((tm, d), lambda i, j: (i, 0)),
                  pl.BlockSpec((1, d), lambda i, j: (0, 0)),
                  _mod_spec(sc_chunk, rowf, tm), _mod_spec(sh_chunk, rowf, tm),
                  pl.BlockSpec((None, d, tn), lambda i, j: (l, 0, j))],
        out_specs=[pl.BlockSpec((tm, tn), lambda i, j: (i, j)),
                   pl.BlockSpec((tm, f32_cols), lambda i, j: (i, 0)),
                   pl.BlockSpec((f32_cols, tm), lambda i, j: (0, i))],
        out_shape=[jax.ShapeDtypeStruct((m, n), BF16), jax.ShapeDtypeStruct((m, f32_cols), F32),
                   jax.ShapeDtypeStruct((f32_cols, m), F32)],
        scratch_shapes=[pltpu.VMEM((tm, d), BF16)],
        compiler_params=_cp(("parallel", "arbitrary")),
    )(x, g.reshape(1, d), mod4, mod4, w)


def _mla_up_kernel(ql_ref, kvl_ref, kr_ref, tab_ref, qg_ref, kvg_ref, wuq_ref, wuk_ref, wuv_ref,
                   q_ref, k_ref, v_ref, *, scale):
    tab = tab_ref[...]
    qn = _rms(ql_ref[...].astype(F32), qg_ref[...]).astype(BF16)
    qf = jnp.dot(qn, wuq_ref[...], preferred_element_type=F32)
    for h in range(MLA_HEADS):
        c0 = h * MLA_DQ
        y = qf[:, c0 + MLA_NOPE:c0 + MLA_DQ] * tab
        z = y + pltpu.roll(y, MLA_ROPE, axis=1)
        q_ref[:, c0:c0 + MLA_NOPE] = (qf[:, c0:c0 + MLA_NOPE] * scale).astype(BF16)
        q_ref[:, c0 + MLA_NOPE:c0 + MLA_DQ] = (z * scale).astype(BF16)
    cn = _rms(kvl_ref[...].astype(F32), kvg_ref[...]).astype(BF16)
    kn = jnp.dot(cn, wuk_ref[...], preferred_element_type=F32)
    v_ref[...] = jnp.dot(cn, wuv_ref[...], preferred_element_type=F32).astype(BF16)
    ky = kr_ref[...].astype(F32) * tab
    kz = ky + pltpu.roll(ky, MLA_ROPE, axis=1)
    lane = lax.broadcasted_iota(jnp.int32, kz.shape, 1)
    kz = jnp.where(lane < MLA_ROPE, kz, 0.0).astype(BF16)
    for h in range(MLA_HEADS):
        c0 = h * MLA_DQ
        k_ref[:, c0:c0 + MLA_NOPE] = kn[:, h * MLA_NOPE:(h + 1) * MLA_NOPE].astype(BF16)
        k_ref[:, c0 + MLA_NOPE:c0 + MLA_DQ] = kz


def _mla_up(proj, tab, qg, kvg, wuq, wuk, wuv):
    m = proj.shape[0]
    tm = min(TM, tab.shape[0])
    nt = tab.shape[0] // tm
    full = lambda a: pl.BlockSpec(a.shape, lambda i: (0, 0))
    return pl.pallas_call(
        functools.partial(_mla_up_kernel, scale=(MLA_NOPE + MLA_ROPE) ** -0.5 * LOG2E),
        grid=(m // tm,),
        in_specs=[pl.BlockSpec((tm, MLA_Q_LORA), lambda i: (i, C_QLAT // MLA_Q_LORA)),
                  pl.BlockSpec((tm, MLA_KV_LORA), lambda i: (i, C_KVLAT // MLA_KV_LORA)),
                  pl.BlockSpec((tm, LANE), lambda i: (i, C_KROPE // LANE)),
                  pl.BlockSpec((tm, LANE), lambda i: (i % nt, 0)),
                  full(qg), full(kvg), full(wuq), full(wuk), full(wuv)],
        out_specs=[pl.BlockSpec((tm, MLA_HEADS * MLA_DQ), lambda i: (i, 0)),
                   pl.BlockSpec((tm, MLA_HEADS * MLA_DQ), lambda i: (i, 0)),
                   pl.BlockSpec((tm, MLA_W), lambda i: (i, 0))],
        out_shape=[jax.ShapeDtypeStruct((m, MLA_HEADS * MLA_DQ), BF16),
                   jax.ShapeDtypeStruct((m, MLA_HEADS * MLA_DQ), BF16),
                   jax.ShapeDtypeStruct((m, MLA_W), BF16)],
        compiler_params=_cp(("parallel",)),
    )(proj, proj, proj, tab, qg, kvg, wuq, wuk, wuv)


def _flash_kernel(*refs, has_lat, scale, nsub, tk):
    if has_lat:
        q_ref, kc_ref, vc_ref, kl_ref, vl_ref, o_ref, m_scr, acc_scr = refs
    else:
        q_ref, kc_ref, vc_ref, o_ref, m_scr, acc_scr = refs
    ts = q_ref.shape[0] // nsub
    dv = vc_ref.shape[1]
    dn = (((1,), (1,)), ((), ()))

    def scores(q, k):
        s = lax.dot_general(q, k, dn, preferred_element_type=F32)
        return s if scale == 1.0 else s * scale

    def with_ones(v):
        lane = lax.broadcasted_iota(jnp.int32, v.shape, 1)
        return jnp.concatenate([v, jnp.where(lane == 0, 1.0, 0.0).astype(v.dtype)], axis=1)

    def lanes(x, n):
        return x if n == LANE else pltpu.repeat(x, n // LANE, axis=1)

    def probs(s, m):
        return jnp.exp2(s - lanes(m, s.shape[1])).astype(BF16)

    kc = kc_ref[...]
    vc = with_ones(vc_ref[...])
    for u in range(nsub):
        s = scores(q_ref[u * ts:(u + 1) * ts, :], kc)
        m = jnp.broadcast_to(jnp.max(s, axis=-1, keepdims=True), (ts, LANE))
        m_scr[u] = m
        acc_scr[u] = jnp.dot(probs(s, m), vc, preferred_element_type=F32)

    if has_lat:
        def body(kb, carry):
            off = pl.multiple_of(kb * tk, tk)
            k = kl_ref[pl.ds(off, tk), :]
            v = with_ones(vl_ref[pl.ds(off, tk), :])
            s_next = scores(q_ref[0:ts, :], k)
            for u in range(nsub):
                s = s_next
                if u + 1 < nsub:
                    s_next = scores(q_ref[(u + 1) * ts:(u + 2) * ts, :], k)
                m_prev = m_scr[u]
                m_new = jnp.maximum(m_prev, jnp.max(s, axis=-1, keepdims=True))
                m_scr[u] = m_new
                acc_scr[u] = (lanes(jnp.exp2(m_prev - m_new), 2 * dv) * acc_scr[u]
                              + jnp.dot(probs(s, m_new), v, preferred_element_type=F32))
            return carry

        lax.fori_loop(0, kl_ref.shape[0] // tk, body, 0)

    for u in range(nsub):
        acc = acc_scr[u]
        o_ref[u * ts:(u + 1) * ts, :] = (acc[:, :dv] / acc[:, dv:dv + 1]).astype(o_ref.dtype)


def _flash(q, qcol, kc, kccol, vc, vccol, heads, dq, dv, batch, lat=None, scale=1.0, tq=2048, nsub=2, tk=2048):
    mq = q.shape[0]
    sq = mq // batch
    tq = min(tq, sq)
    nsub = min(nsub, tq // 256)
    nq = sq // tq
    tc = kc.shape[0] // batch
    has_lat = lat is not None
    in_specs = [pl.BlockSpec((tq, dq), lambda b, h, i: (b * nq + i, qcol + h)),
                pl.BlockSpec((tc, dq), lambda b, h, i: (b, kccol + h)),
                pl.BlockSpec((tc, dv), lambda b, h, i: (b, vccol + h))]
    args = [q, kc, vc]
    if has_lat:
        kl, klcol, vl, vlcol = lat
        sk = kl.shape[0] // batch
        tk = min(tk, sk)
        in_specs += [pl.BlockSpec((sk, dq), lambda b, h, i: (b, klcol + h)),
                     pl.BlockSpec((sk, dv), lambda b, h, i: (b, vlcol + h))]
        args += [kl, vl]
    return pl.pallas_call(
        functools.partial(_flash_kernel, has_lat=has_lat, scale=scale, nsub=nsub, tk=tk),
        grid=(batch, heads, nq),
        in_specs=in_specs,
        out_specs=pl.BlockSpec((tq, dv), lambda b, h, i: (b * nq + i, h)),
        out_shape=jax.ShapeDtypeStruct((mq, heads * dv), BF16),
        scratch_shapes=[pltpu.VMEM((nsub, tq // nsub, LANE), F32), pltpu.VMEM((nsub, tq // nsub, 2 * dv), F32)],
        compiler_params=_cp(("parallel", "parallel", "arbitrary")),
    )(*args)


NA_RB = 8
NA_BLK = NA_RB * GRID_W
NA_GROUP = 2


def _na_kernel(q_ref, kp_ref, kc_ref, kn_ref, vp_ref, vc_ref, vn_ref, kx_ref, vx_ref, bias_ref, o_ref,
               kw_scr, vw_scr, *, rows, scale):
    j = pl.program_id(1)
    for t, (kr, vr) in enumerate(((kp_ref, vp_ref), (kc_ref, vc_ref), (kn_ref, vn_ref))):
        kw_scr[t * NA_BLK:(t + 1) * NA_BLK, :] = kr[...]
        vw_scr[t * NA_BLK:(t + 1) * NA_BLK, :] = vr[...]
    nwin = NA_KR * GRID_W
    dn = (((1,), (1,)), ((), ()))
    heads = [slice(h * NA_DH, (h + 1) * NA_DH) for h in range(NA_HEADS)]
    for i0 in range(0, NA_RB, NA_GROUP):
        units = []
        for i in range(i0, i0 + NA_GROUP):
            r = NA_RB * j + i
            r0 = jnp.clip(r - NA_KR // 2, 0, rows - NA_KR)
            start = pl.multiple_of((r0 - NA_RB * j + NA_RB) * GRID_W, GRID_W)
            for h, cs in enumerate(heads):
                q = q_ref[i * GRID_W:(i + 1) * GRID_W, cs]
                s_loc = (lax.dot_general(q, kw_scr[pl.ds(start, nwin), cs], dn, preferred_element_type=F32) * scale
                         + bias_ref[h, r - r0])
                s_ctx = lax.dot_general(q, kx_ref[:, cs], dn, preferred_element_type=F32) * scale
                units.append((i, cs, start, s_loc, s_ctx))
        outs = {}
        for i, cs, start, s_loc, s_ctx in units:
            m = jnp.maximum(jnp.max(s_loc, axis=-1, keepdims=True), jnp.max(s_ctx, axis=-1, keepdims=True))
            p_loc = jnp.exp(s_loc - m)
            p_ctx = jnp.exp(s_ctx - m)
            l = jnp.sum(p_loc, axis=-1, keepdims=True) + jnp.sum(p_ctx, axis=-1, keepdims=True)
            o = (jnp.dot(p_loc.astype(BF16), vw_scr[pl.ds(start, nwin), cs], preferred_element_type=F32)
                 + jnp.dot(p_ctx.astype(BF16), vx_ref[:, cs], preferred_element_type=F32))
            outs.setdefault(i, []).append((o / l).astype(o_ref.dtype))
        for i, parts in outs.items():
            o_ref[i * GRID_W:(i + 1) * GRID_W, :] = jnp.concatenate(parts, axis=1)


def _na_bias_table(rpb):
    c = np.arange(GRID_W)
    cs = np.clip(c - NA_KC // 2, 0, GRID_W - NA_KC)
    valid = (c[None, :] >= cs[:, None]) & (c[None, :] < cs[:, None] + NA_KC)
    dc = np.clip(c[None, :] - c[:, None] + (NA_KC - 1), 0, 2 * NA_KC - 2)
    dr = np.arange(NA_KR)[None, :] - np.arange(NA_KR)[:, None] + (NA_KR - 1)
    sel_r = (dr[:, :, None] == np.arange(2 * NA_KR - 1)).astype(np.float32)
    sel_c = ((dc[:, :, None] == np.arange(2 * NA_KC - 1)) & valid[:, :, None]).astype(np.float32)
    tab = jnp.einsum('lhab,via,cdb->lhvcid', rpb.astype(F32), sel_r, sel_c, precision=lax.Precision.HIGHEST)
    tab = tab + jnp.where(valid[None, None, None, :, None, :], 0.0, -1e30)
    return tab.reshape(rpb.shape[0], NA_HEADS, NA_KR, GRID_W, NA_KR * GRID_W)


def _na(proj_l, proj_c, bias, l, batch):
    m = proj_l.shape[0]
    s = m // batch
    rows = s // GRID_W
    nb = rows // NA_RB
    cq, ck, cv = C_NAQ // NA_W, C_NAK // NA_W, C_NAV // NA_W
    tc = proj_c.shape[0] // batch
    prev = lambda b, j: b * nb + jnp.maximum(j - 1, 0)
    cur = lambda b, j: b * nb + j
    nxt = lambda b, j: b * nb + jnp.minimum(j + 1, nb - 1)
    blk = lambda rf, col: pl.BlockSpec((NA_BLK, NA_W), lambda b, j: (rf(b, j), col))
    return pl.pallas_call(
        functools.partial(_na_kernel, rows=rows, scale=NA_DH ** -0.5),
        grid=(batch, nb),
        in_specs=[blk(cur, cq), blk(prev, ck), blk(cur, ck), blk(nxt, ck),
                  blk(prev, cv), blk(cur, cv), blk(nxt, cv),
                  pl.BlockSpec((tc, NA_W), lambda b, j: (b, ck)),
                  pl.BlockSpec((tc, NA_W), lambda b, j: (b, cv)),
                  pl.BlockSpec((None,) + bias.shape[1:], lambda b, j: (l, 0, 0, 0, 0))],
        out_specs=pl.BlockSpec((NA_BLK, NA_W), lambda b, j: (b * nb + j, 0)),
        out_shape=jax.ShapeDtypeStruct((m, NA_W), BF16),
        scratch_shapes=[pltpu.VMEM((3 * NA_BLK, NA_W), BF16), pltpu.VMEM((3 * NA_BLK, NA_W), BF16)],
        compiler_params=_cp(("parallel", "arbitrary")),
    )(proj_l, proj_l, proj_l, proj_l, proj_l, proj_l, proj_l, proj_c, proj_c, bias)


def _ml_prep_kernel(x_ref, p_ref, n_ref, w_ref, b_ref, sc_ref, o_ref, *, tiles_per_seg, n_conv):
    j = pl.program_id(1)
    x = x_ref[...].astype(F32)
    tm = x.shape[0]

    def widen(y, fill):
        lane = lax.broadcasted_iota(jnp.int32, (tm, ML_DP - ML_DH), 1)
        spare = jnp.where(lane == 0, fill, 0.0)
        return jnp.concatenate([y[:, :ML_DH], spare, y[:, ML_DH:], spare], axis=1).astype(o_ref.dtype)

    @pl.when(j < n_conv)
    def _():
        t = pl.program_id(0) % tiles_per_seg
        prev_row = jnp.where(t == 0, 0.0, p_ref[HALO - 1:HALO, :].astype(F32))
        next_row = jnp.where(t == tiles_per_seg - 1, 0.0, n_ref[0:1, :].astype(F32))
        rid = lax.broadcasted_iota(jnp.int32, x.shape, 0)
        xp = jnp.where(rid == 0, prev_row, pltpu.roll(x, 1, axis=0))
        xn = jnp.where(rid == tm - 1, next_row, pltpu.roll(x, tm - 1, axis=0))
        w = w_ref[...]
        y = xp * w[0:1, :] + x * w[1:2, :] + xn * w[2:3, :] + b_ref[...]
        o_ref[...] = widen(y * _sigmoid(y) * sc_ref[...], 0.0)

    @pl.when(j >= n_conv)
    def _():
        o_ref[...] = widen(x, 1.0)


def _ml_prep(proj, w, b, colscale, seg):
    m = proj.shape[0]
    tm = min(TM_BIG, seg)
    hb = tm // HALO
    last = m // HALO - 1
    n_conv = w.shape[1] // ML_PAIR
    n_all = 3 * ML_W // ML_PAIR
    vec = lambda r: pl.BlockSpec((r, ML_PAIR), lambda i, j: (0, jnp.minimum(j, n_conv - 1)))
    return pl.pallas_call(
        functools.partial(_ml_prep_kernel, tiles_per_seg=seg // tm, n_conv=n_conv),
        grid=(m // tm, n_all),
        in_specs=[pl.BlockSpec((tm, ML_PAIR), lambda i, j: (i, j)),
                  pl.BlockSpec((HALO, ML_PAIR), lambda i, j: (jnp.maximum(i * hb - 1, 0), j)),
                  pl.BlockSpec((HALO, ML_PAIR), lambda i, j: (jnp.minimum((i + 1) * hb, last), j)),
                  vec(3), vec(1), vec(1)],
        out_specs=pl.BlockSpec((tm, 2 * ML_DP), lambda i, j: (i, j)),
        out_shape=jax.ShapeDtypeStruct((m, 3 * ML_WP), BF16),
        compiler_params=_cp(("parallel", "parallel")),
    )(proj, proj, proj, w, b, colscale)


def _split3(x):
    x1 = x.astype(BF16)
    r = x - x1.astype(F32)
    x2 = r.astype(BF16)
    x3 = (r - x2.astype(F32)).astype(BF16)
    return x1, x2, x3


def _log_sigmoid(x):
    return jnp.minimum(x, 0.0) - jnp.log(1.0 + jnp.exp(-jnp.abs(x)))


def _mlstm_dir(d, q_ref, k_ref, v_ref, gc_ref, gr_ref, bc_ref, br_ref, h_ref, c_scr, m_scr):
    L = ML_L
    ri = lax.broadcasted_iota(jnp.int32, (L, L), 0)
    ci = lax.broadcasted_iota(jnp.int32, (L, L), 1)
    mask = (ci <= ri) if d == 0 else (ci >= ri)
    tri = jnp.where(mask, 1.0, 0.0).astype(BF16)
    tri_t = jnp.where((ri <= ci) if d == 0 else (ri >= ci), 1.0, 0.0).astype(BF16)
    lane = lax.broadcasted_iota(jnp.int32, (L, ML_DP), 1)

    gc = gc_ref[...] + bc_ref[...]
    gr = gr_ref[...] + br_ref[...]
    bc_all = sum(jnp.dot(tri, p, preferred_element_type=F32) for p in _split3(_log_sigmoid(gc)))
    br_all = sum(jnp.dot(p, tri_t, preferred_element_type=F32) for p in _split3(_log_sigmoid(gr)))

    hs = range(ML_HEADS)
    cols = [slice(h * ML_DP, (h + 1) * ML_DP) for h in hs]
    q = [q_ref[:, cs] for cs in cols]
    k = [k_ref[:, cs] for cs in cols]
    v = [v_ref[:, cs] for cs in cols]
    m_prev = [m_scr[h, d][0:1, 0:1] for h in hs]
    ct = [c_scr[h, d] for h in hs]
    qk = [lax.dot_general(q[h], k[h], (((1,), (1,)), ((), ())), preferred_element_type=F32) for h in hs]
    inter = [jnp.dot(q[h], ct[h].astype(BF16), preferred_element_type=F32) for h in hs]

    rep = lambda col: jnp.broadcast_to(col, (L, LANE))
    wide = lambda x: pltpu.repeat(x, ML_DP // LANE, axis=1)
    b_rep = [rep(bc_all[:, 4 * h + 2 + d:4 * h + 3 + d]) for h in hs]
    li_rep = [rep(gc[:, 4 * h + d:4 * h + d + 1]) for h in hs]
    a, w_st, m_t = [], [], []
    for h in hs:
        b_row = br_all[4 * h + 2 + d:4 * h + 3 + d, :]
        li_row = gr[4 * h + d:4 * h + d + 1, :]
        dmat = jnp.where(mask, wide(b_rep[h]) - b_row + li_row, -jnp.inf)
        g = b_rep[h] + m_prev[h]
        m_t.append(jnp.maximum(g, jnp.max(dmat, axis=-1, keepdims=True)))
        w_st.append(jnp.exp(g - m_t[h]))
        a.append((qk[h] * jnp.exp(dmat - wide(m_t[h]))).astype(BF16))

    for h in hs:
        num = wide(w_st[h]) * inter[h] + jnp.dot(a[h], v[h], preferred_element_type=F32)
        den = rep(num[:, ML_DH:ML_DH + 1])
        inv = 1.0 / jnp.maximum(jnp.abs(den), jnp.exp(-m_t[h]))
        h_ref[:, cols[h]] = jnp.where(lane < ML_DH, num * wide(inv), 0.0).astype(h_ref.dtype)

    for h in hs:
        b_end = b_rep[h][L - 1:L, :] if d == 0 else b_rep[h][0:1, :]
        e_log = b_end - b_rep[h] + li_rep[h]
        m_new = jnp.maximum(b_end + m_prev[h], jnp.max(e_log, axis=0, keepdims=True))
        decay = jnp.exp(b_end + m_prev[h] - m_new)[:, 0:1]
        ek = (wide(jnp.exp(e_log - m_new)) * k[h].astype(F32)).astype(BF16)
        c_scr[h, d] = decay * ct[h] + lax.dot_general(ek, v[h], (((0,), (0,)), ((), ())),
                                                      preferred_element_type=F32)
        m_scr[h, d] = jnp.broadcast_to(m_new, m_scr.shape[2:])


def _mlstm_kernel(qf_ref, kf_ref, vf_ref, gcf_ref, grf_ref, qb_ref, kb_ref, vb_ref, gcb_ref, grb_ref,
                  bc_ref, br_ref, c0_ref, m0_ref, hf_ref, hb_ref, c1_ref, m1_ref, c_scr, m_scr):
    c = pl.program_id(1)

    @pl.when(c == 0)
    def _():
        c_scr[...] = c0_ref[...]
        m_scr[...] = m0_ref[...]

    _mlstm_dir(0, qf_ref, kf_ref, vf_ref, gcf_ref, grf_ref, bc_ref, br_ref, hf_ref, c_scr, m_scr)
    _mlstm_dir(1, qb_ref, kb_ref, vb_ref, gcb_ref, grb_ref, bc_ref, br_ref, hb_ref, c_scr, m_scr)

    @pl.when(c == pl.num_programs(1) - 1)
    def _():
        c1_ref[...] = c_scr[...]
        m1_ref[...] = m_scr[...]


def _mlstm(qkv, gates, gates_row, bias_col, bias_row, c0, m0, batch):
    m = qkv.shape[0]
    s = m // batch
    nc = s // ML_L
    fwd = lambda b, c: b * nc + c
    bwd = lambda b, c: b * nc + (nc - 1 - c)

    def dir_specs(rf):
        return [pl.BlockSpec((ML_L, ML_WP), lambda b, c: (rf(b, c), 0)),
                pl.BlockSpec((ML_L, ML_WP), lambda b, c: (rf(b, c), 1)),
                pl.BlockSpec((ML_L, ML_WP), lambda b, c: (rf(b, c), 2)),
                pl.BlockSpec((ML_L, C_GATES), lambda b, c: (rf(b, c), 0)),
                pl.BlockSpec((4 * ML_HEADS, ML_L), lambda b, c: (0, rf(b, c)))]

    st_c = pl.BlockSpec((None, ML_HEADS, 2, ML_DP, ML_DP), lambda b, c: (b, 0, 0, 0, 0))
    st_m = pl.BlockSpec((None, ML_HEADS, 2, 8, LANE), lambda b, c: (b, 0, 0, 0, 0))
    return pl.pallas_call(
        _mlstm_kernel,
        grid=(batch, nc),
        in_specs=dir_specs(fwd) + dir_specs(bwd)
        + [pl.BlockSpec((1, C_GATES), lambda b, c: (0, 0)),
           pl.BlockSpec((4 * ML_HEADS, 1), lambda b, c: (0, 0)), st_c, st_m],
        out_specs=[pl.BlockSpec((ML_L, ML_WP), lambda b, c: (fwd(b, c), 0)),
                   pl.BlockSpec((ML_L, ML_WP), lambda b, c: (bwd(b, c), 0)), st_c, st_m],
        out_shape=[jax.ShapeDtypeStruct((m, ML_WP), BF16), jax.ShapeDtypeStruct((m, ML_WP), BF16),
                   jax.ShapeDtypeStruct(c0.shape, F32), jax.ShapeDtypeStruct(m0.shape, F32)],
        scratch_shapes=[pltpu.VMEM((ML_HEADS, 2, ML_DP, ML_DP), F32), pltpu.VMEM((ML_HEADS, 2, 8, LANE), F32)],
        compiler_params=_cp(("parallel", "arbitrary")),
    )(qkv, qkv, qkv, gates, gates_row, qkv, qkv, qkv, gates, gates_row, bias_col, bias_row, c0, m0)


def _outproj_kernel(mla_ref, hf_ref, hb_ref, og_ref, na_ref, w1_ref, w2_ref, w3_ref, res_ref, gate_ref,
                    out_ref, ml_scr):
    @pl.when(pl.program_id(1) == 0)
    def _():
        og = _sigmoid(og_ref[...].astype(F32))
        for h in range(ML_HEADS):
            cs = slice(h * ML_DP, (h + 1) * ML_DP)
            o_h = jnp.concatenate([og[:, h * ML_DH:(h + 1) * ML_DH],
                                   jnp.zeros((og.shape[0], ML_DP - ML_DH), F32)], axis=1)
            ml_scr[:, cs] = ((hf_ref[:, cs].astype(F32) + hb_ref[:, cs].astype(F32)) * o_h).astype(BF16)

    acc = (jnp.dot(mla_ref[...], w1_ref[...], preferred_element_type=F32)
           + jnp.dot(ml_scr[...], w2_ref[...], preferred_element_type=F32)
           + jnp.dot(na_ref[...], w3_ref[...], preferred_element_type=F32))
    out_ref[...] = res_ref[...] + gate_ref[...] * acc


def _outproj(mla, hf, hb, proj, na, w_out, w2, l, res, mod4, gate_chunk, rowf):
    m, d = res.shape
    tm = min(TM, m)
    tn = d
    rows = lambda width, col: pl.BlockSpec((tm, width), lambda i, j: (i, col))
    wrows = lambda nrows, blk: pl.BlockSpec((None, nrows, tn), lambda i, j: (l, blk, j),
                                            pipeline_mode=pl.Buffered(1))
    return pl.pallas_call(
        _outproj_kernel,
        grid=(m // tm, d // tn),
        in_specs=[rows(MLA_W, 0), rows(ML_WP, 0), rows(ML_WP, 0), rows(ML_W, C_MLO // ML_W), rows(NA_W, 0),
                  wrows(MLA_W, 0), wrows(ML_WP, 0), wrows(NA_W, (MLA_W + ML_W) // NA_W),
                  pl.BlockSpec((tm, tn), lambda i, j: (i, j)), _mod_spec(gate_chunk, rowf, tm, tn)],
        out_specs=pl.BlockSpec((tm, tn), lambda i, j: (i, j)),
        out_shape=jax.ShapeDtypeStruct((m, d), F32),
        scratch_shapes=[pltpu.VMEM((tm, ML_WP), BF16)],
        compiler_params=_cp(("parallel", "arbitrary")),
    )(mla, hf, hb, proj, na, w_out, w2, w_out, res, mod4)


def _ffn_kernel(x_ref, xp_ref, xn_ref, g_ref, sc_ref, sh_ref, wg_ref, wv_ref, cwg_ref, cwv_ref, cbg_ref, cbv_ref,
                wd_ref, gate_ref, *rest, tiles_per_seg, final):
    if final:
        fg_ref, o_ref, hn_scr = rest
    else:
        o_ref, hn_scr = rest
    i = pl.program_id(0)
    j = pl.program_id(1)
    tm = x_ref.shape[0]
    te = hn_scr.shape[0]

    @pl.when(j == 0)
    def _():
        _norm_mod_rows(xp_ref, hn_scr, 0, g_ref, sc_ref, sh_ref)
        _norm_mod_rows(x_ref, hn_scr, HALO, g_ref, sc_ref, sh_ref)
        _norm_mod_rows(xn_ref, hn_scr, HALO + tm, g_ref, sc_ref, sh_ref)
        o_ref[...] = jnp.zeros_like(o_ref)

    t = i % tiles_per_seg
    keep_prev = jnp.where(t == 0, 0.0, 1.0)
    keep_next = jnp.where(t == tiles_per_seg - 1, 0.0, 1.0)
    hn = hn_scr[...]

    def conv(w_ref, cw_ref, cb_ref):
        u = jnp.dot(hn, w_ref[...], preferred_element_type=F32)
        rid = lax.broadcasted_iota(jnp.int32, (tm, u.shape[1]), 0)
        up = pltpu.roll(u, 1, axis=0)[HALO:HALO + tm]
        un = pltpu.roll(u, te - 1, axis=0)[HALO:HALO + tm]
        up = jnp.where(rid == 0, up * keep_prev, up)
        un = jnp.where(rid == tm - 1, un * keep_next, un)
        cw = cw_ref[...]
        return up * cw[0:1, :] + u[HALO:HALO + tm] * cw[1:2, :] + un * cw[2:3, :] + cb_ref[...]

    gg = conv(wg_ref, cwg_ref, cbg_ref)
    vv = conv(wv_ref, cwv_ref, cbv_ref)
    act = (gg * _sigmoid(gg) * vv).astype(BF16)
    o_ref[...] += jnp.dot(act, wd_ref[...], preferred_element_type=F32)

    @pl.when(j == pl.num_programs(1) - 1)
    def _():
        gate = gate_ref[...]
        n_groups = tm // ROWS_F32

        def body(r, carry):
            rows = pl.ds(pl.multiple_of(r * ROWS_F32, ROWS_F32), ROWS_F32)
            y = x_ref[rows, :] + gate * o_ref[rows, :]
            o_ref[rows, :] = _rms(y, fg_ref[...]) if final else y
            return carry

        lax.fori_loop(0, n_groups, body, 0, unroll=min(16, n_groups))


def _ffn(x, g, mod4, rowf, w_up, cw, cb, w_down, l, seg, final_g=None, tf=512):
    m, d = x.shape
    dff = w_down.shape[1]
    tm = min(TM_BIG, seg)
    nj = dff // tf
    hb = tm // HALO
    last = m // HALO - 1
    final = final_g is not None
    vec = lambda r, off: pl.BlockSpec((None, r, tf), lambda i, j: (l, 0, off + j))
    in_specs = [pl.BlockSpec((tm, d), lambda i, j: (i, 0)),
                pl.BlockSpec((HALO, d), lambda i, j: (jnp.maximum(i * hb - 1, 0), 0)),
                pl.BlockSpec((HALO, d), lambda i, j: (jnp.minimum((i + 1) * hb, last), 0)),
                pl.BlockSpec((1, d), lambda i, j: (0, 0)),
                _mod_spec(4, rowf, tm), _mod_spec(3, rowf, tm),
                pl.BlockSpec((None, d, tf), lambda i, j: (l, 0, j)),
                pl.BlockSpec((None, d, tf), lambda i, j: (l, 0, nj + j)),
                vec(3, 0), vec(3, nj), vec(1, 0), vec(1, nj),
                pl.BlockSpec((None, tf, d), lambda i, j: (l, j, 0)), _mod_spec(5, rowf, tm)]
    args = [x, x, x, g.reshape(1, d), mod4, mod4, w_up, w_up, cw, cw, cb, cb, w_down, mod4]
    if final:
        in_specs.append(pl.BlockSpec((1, d), lambda i, j: (0, 0)))
        args.append(final_g.reshape(1, d))
    return pl.pallas_call(
        functools.partial(_ffn_kernel, tiles_per_seg=seg // tm, final=final),
        grid=(m // tm, nj),
        in_specs=in_specs,
        out_specs=pl.BlockSpec((tm, d), lambda i, j: (i, 0)),
        out_shape=jax.ShapeDtypeStruct((m, d), F32),
        scratch_shapes=[pltpu.VMEM((tm + 2 * HALO, d), BF16)],
        compiler_params=_cp(("parallel", "arbitrary")),
    )(*args)


def _pad_heads(a, heads, dh, dp, axis):
    shp = a.shape
    a = a.reshape(shp[:axis] + (heads, dh) + shp[axis + 1:])
    pad = [(0, 0)] * a.ndim
    pad[axis + 1] = (0, dp - dh)
    a = jnp.pad(a, pad)
    return a.reshape(shp[:axis] + (heads * dp,) + shp[axis + 1:])


def _rope_cols(a):
    ev, od = a[..., 0::2], a[..., 1::2]
    return jnp.concatenate([ev, od, od, ev], axis=-1)


def _prep_w_in(w):
    w = w.astype(BF16)
    lead = w.shape[:-1]
    o_rope, o_ml, o_gi, o_na = IN_OFFSETS[1], IN_OFFSETS[2], IN_OFFSETS[6], IN_OFFSETS[8]
    gi = w[..., o_gi:o_gi + 2 * ML_HEADS].reshape(lead + (2, ML_HEADS))
    gf = w[..., o_gi + 2 * ML_HEADS:o_na].reshape(lead + (2, ML_HEADS))
    g4 = jnp.stack([gi[..., 0, :], gi[..., 1, :], gf[..., 0, :], gf[..., 1, :]], axis=-1)
    gates = g4.reshape(lead + (4 * ML_HEADS,))

    def place(a, col):
        return jnp.pad(a, [(0, 0)] * len(lead) + [(col, C_IN - col - a.shape[-1])])

    return (place(w[..., o_ml:o_gi], C_MLQ) + place(w[..., :o_rope], C_QLAT)
            + place(_rope_cols(w[..., o_rope:o_ml]), C_KROPE) + place(gates, C_GATE0) + place(w[..., o_na:], C_NAQ))


def _prep_w_uq(w):
    w = w.reshape(w.shape[0], MLA_HEADS, MLA_NOPE + MLA_ROPE)
    out = jnp.concatenate([w[..., :MLA_NOPE], _rope_cols(w[..., MLA_NOPE:])], axis=-1)
    return out.reshape(w.shape[0], MLA_HEADS * MLA_DQ).astype(BF16)


def _gate_bias(i_bias, f_bias):
    g4 = jnp.stack([i_bias[0], i_bias[1], f_bias[0], f_bias[1]], axis=-1).astype(F32).reshape(4 * ML_HEADS)
    return jnp.pad(g4, (0, LANE - 4 * ML_HEADS)).reshape(1, LANE), g4.reshape(4 * ML_HEADS, 1)


def _rope_table(n_tokens):
    t = jnp.arange(n_tokens)
    row = (t // GRID_W).astype(F32)
    col = (t % GRID_W).astype(F32)
    n_freq = MLA_ROPE // 4
    inv = ROPE_BASE ** (-jnp.arange(n_freq, dtype=F32) / n_freq)
    ang = jnp.concatenate([row[:, None] * inv, col[:, None] * inv], axis=-1)
    cos, sin = jnp.cos(ang), jnp.sin(ang)
    return jnp.concatenate([cos, cos, -sin, sin], axis=-1)


def kernel(x, c, ctx, c_ctx, ada_w, ada_b, norm1_g, norm2_g, w_in, mla_q_norm, mla_kv_norm, mla_w_uq, mla_w_uk, mla_w_uv, ml_conv_w, ml_conv_b, ml_i_bias, ml_f_bias, na_rpb, w_out, ffn_w_up, ffn_conv_w, ffn_conv_b, ffn_w_down, final_norm_g):
    batch, seq, d = x.shape
    tctx = ctx.shape[1]
    depth = ada_w.shape[0]
    assert d == D_MODEL and batch <= 7 and tctx == ML_L and tctx % HALO == 0
    assert seq % TM_BIG == 0 and seq % NA_BLK == 0 and seq // GRID_W >= NA_KR

    h = x.reshape(batch * seq, d)
    hc = ctx.reshape(batch * tctx, d)
    row_lat = lambda i, tm: i // (seq // tm)
    row_ctx = lambda i, tm: batch
    cond8 = jnp.zeros((8, d), F32).at[:batch].set(c).at[batch].set(c_ctx)

    tab_lat = _rope_table(seq)
    tab_ctx = jnp.concatenate([jnp.ones((tctx, MLA_ROPE), F32), jnp.zeros((tctx, MLA_ROPE), F32)], axis=-1)
    ml_scale = jnp.concatenate([jnp.ones((1, ML_W), F32), jnp.full((1, ML_W), ML_DH ** -0.5, F32)], axis=-1)

    w_in_p = _prep_w_in(w_in)
    w_out_p = w_out.astype(BF16)
    w2_p = _pad_heads(w_out_p[:, MLA_W:MLA_W + ML_W], ML_HEADS, ML_DH, ML_DP, 1)
    w_up_p = ffn_w_up.astype(BF16)
    w_down_p = ffn_w_down.astype(BF16)
    fcb = ffn_conv_b.reshape(depth, 1, -1)
    na_bias = _na_bias_table(na_rpb)

    for l in range(depth):
        last = l == depth - 1
        mod4 = _ada(cond8, ada_w, ada_b, l).reshape(8, N_MOD, 1, d)

        wuq = _prep_w_uq(mla_w_uq[l])
        wuk = mla_w_uk[l].astype(BF16)
        wuv = mla_w_uv[l].astype(BF16)
        qg = mla_q_norm[l].reshape(1, -1)
        kvg = mla_kv_norm[l].reshape(1, -1)
        cw = ml_conv_w[l]
        cb = ml_conv_b[l].reshape(1, -1)
        bias_col, bias_row = _gate_bias(ml_i_bias[l], ml_f_bias[l])

        proj_c, gates_c, grow_c = _nm_mm(hc, norm1_g[l], mod4, 1, 0, row_ctx, w_in_p, l, TN_IN, C_GATE0, C_GATES)
        proj_l, gates_l, grow_l = _nm_mm(h, norm1_g[l], mod4, 1, 0, row_lat, w_in_p, l, TN_IN, C_GATE0, C_GATES)

        q_c, k_c, v_c = _mla_up(proj_c, tab_ctx, qg, kvg, wuq, wuk, wuv)
        q_l, k_l, v_l = _mla_up(proj_l, tab_lat, qg, kvg, wuq, wuk, wuv)
        mla_lat = _flash(q_l, 0, k_c, 0, v_c, 0, MLA_HEADS, MLA_DQ, MLA_V, batch, lat=(k_l, 0, v_l, 0))

        qkv_c = _ml_prep(proj_c, cw, cb, ml_scale, tctx)
        qkv_l = _ml_prep(proj_l, cw, cb, ml_scale, seq)
        c0 = jnp.zeros((batch, ML_HEADS, 2, ML_DP, ML_DP), F32)
        m0 = jnp.zeros((batch, ML_HEADS, 2, 8, LANE), F32)
        hf_c, hb_c, c1, m1 = _mlstm(qkv_c, gates_c, grow_c, bias_col, bias_row, c0, m0, batch)
        hf_l, hb_l, _, _ = _mlstm(qkv_l, gates_l, grow_l, bias_col, bias_row, c1, m1, batch)

        na_lat = _na(proj_l, proj_c, na_bias, l, batch)

        h = _outproj(mla_lat, hf_l, hb_l, proj_l, na_lat, w_out_p, w2_p, l, h, mod4, 2, row_lat)
        h = _ffn(h, norm2_g[l], mod4, row_lat, w_up_p, ffn_conv_w, fcb, w_down_p, l, seq,
                 final_g=final_norm_g if last else None)

        if not last:
            mla_ctx = _flash(q_c, 0, k_c, 0, v_c, 0, MLA_HEADS, MLA_DQ, MLA_V, batch)
            na_ctx = _flash(proj_c, C_NAQ // NA_DH, proj_c, C_NAK // NA_DH, proj_c, C_NAV // NA_DH,
                            NA_HEADS, NA_DH, NA_DH, batch, scale=NA_DH ** -0.5 * LOG2E)
            hc = _outproj(mla_ctx, hf_c, hb_c, proj_c, na_ctx, w_out_p, w2_p, l, hc, mod4, 2, row_ctx)
            hc = _ffn(hc, norm2_g[l], mod4, row_ctx, w_up_p, ffn_conv_w, fcb, w_down_p, l, tctx)

    return h.reshape(batch, seq, d)
```

```python
import functools
import math

import numpy as np
import jax
import jax.numpy as jnp
from jax import lax
from jax.experimental import pallas as pl
from jax.experimental.pallas import tpu as pltpu

F32 = jnp.float32
BF16 = jnp.bfloat16

D_MODEL = 2048
GRID_W = 64
MLA_HEADS = 6
MLA_NOPE = 128
MLA_ROPE = 64
MLA_V = 128
MLA_Q_LORA = 512
MLA_KV_LORA = 256
ML_HEADS = 4
ML_DH = 192
NA_HEADS = 4
NA_DH = 128
NA_KR = 8
NA_KC = 16
ROPE_BASE = 10000.0
EPS = 1e-6
N_MOD = 6
ML_W = ML_HEADS * ML_DH
NA_W = NA_HEADS * NA_DH
MLA_W = MLA_HEADS * MLA_V
IN_SIZES = (MLA_Q_LORA, MLA_KV_LORA, MLA_ROPE, ML_W, ML_W, ML_W, ML_W, 2 * ML_HEADS, 2 * ML_HEADS, NA_W, NA_W, NA_W)
IN_OFFSETS = tuple(sum(IN_SIZES[:i + 1]) for i in range(len(IN_SIZES) - 1))
LOG2E = math.log2(math.e)

LANE = 128
ML_DP = 256
ML_WP = ML_HEADS * ML_DP
ML_L = 256
MLA_DQ = 256
TM = 512
TM_BIG = 1024
ROWS_F32 = 8
ROWS_BF16 = 16
HALO = ROWS_BF16
VMEM_LIMIT = 58 * 1024 * 1024

C_MLQ = 0
C_MLK = C_MLQ + ML_W
C_MLV = C_MLK + ML_W
C_MLO = C_MLV + ML_W
C_QLAT = C_MLO + ML_W
C_KVLAT = C_QLAT + MLA_Q_LORA
C_KROPE = C_KVLAT + MLA_KV_LORA
C_GATE0 = C_KROPE + LANE
C_GATES = LANE
C_NAQ = C_GATE0 + C_GATES
C_NAK = C_NAQ + NA_W
C_NAV = C_NAK + NA_W
C_IN = C_NAV + NA_W
TN_IN = 512
ML_PAIR = 2 * ML_DH


def _cp(sem):
    return pltpu.CompilerParams(dimension_semantics=sem, vmem_limit_bytes=VMEM_LIMIT)


def _sigmoid(x):
    return 1.0 / (1.0 + jnp.exp(-x))


def _rms(x, g):
    return x * lax.rsqrt(jnp.mean(x * x, axis=-1, keepdims=True) + EPS) * g


def _norm_mod_rows(x_ref, dst_ref, dst_row0, g_ref, sc_ref, sh_ref):
    gs = g_ref[...] * (1.0 + sc_ref[...])
    sh = sh_ref[...]

    def body(r, carry):
        row = pl.multiple_of(r * ROWS_BF16, ROWS_BF16)
        x = x_ref[pl.ds(row, ROWS_BF16), :]
        y = x * lax.rsqrt(jnp.mean(x * x, axis=-1, keepdims=True) + EPS)
        dst_ref[pl.ds(dst_row0 + row, ROWS_BF16), :] = (y * gs + sh).astype(BF16)
        return carry

    lax.fori_loop(0, x_ref.shape[0] // ROWS_BF16, body, 0, unroll=min(16, x_ref.shape[0] // ROWS_BF16))


def _mod_spec(chunk, rowf, tm, width=D_MODEL):
    return pl.BlockSpec((None, None, 1, width), lambda i, j: (rowf(i, tm), chunk, 0, j if width != D_MODEL else 0))


def _ada_kernel(c_ref, w_ref, b_ref, o_ref):
    c = c_ref[...]
    a = c * _sigmoid(c)
    o_ref[...] = jnp.dot(a, w_ref[...], preferred_element_type=F32,
                         precision=lax.Precision.HIGHEST) + b_ref[...]


def _ada(cond8, w, b, l):
    _, d, n = w.shape
    tn = 1024
    return pl.pallas_call(
        _ada_kernel,
        grid=(n // tn,),
        in_specs=[pl.BlockSpec((8, d), lambda j: (0, 0)),
                  pl.BlockSpec((None, d, tn), lambda j: (l, 0, j)),
                  pl.BlockSpec((None, 1, tn), lambda j: (l, 0, j))],
        out_specs=pl.BlockSpec((8, tn), lambda j: (0, j)),
        out_shape=jax.ShapeDtypeStruct((8, n), F32),
        compiler_params=_cp(("arbitrary",)),
    )(cond8, w, b.reshape(b.shape[0], 1, n))


def _nm_mm_kernel(x_ref, g_ref, sc_ref, sh_ref, w_ref, o_ref, t_ref, tt_ref, hn_ref, *, f32_col):
    j = pl.program_id(1)
    tn = w_ref.shape[1]

    @pl.when(j == 0)
    def _():
        _norm_mod_rows(x_ref, hn_ref, 0, g_ref, sc_ref, sh_ref)

    acc = jnp.dot(hn_ref[...], w_ref[...], preferred_element_type=F32)
    o_ref[...] = acc.astype(o_ref.dtype)

    @pl.when(j == f32_col // tn)
    def _():
        t = acc[:, f32_col % tn:f32_col % tn + t_ref.shape[1]]
        t_ref[...] = t
        tt_ref[...] = t.T


def _nm_mm(x, g, mod4, sc_chunk, sh_chunk, rowf, w, l, tn, f32_col, f32_cols):
    m, d = x.shape
    n = w.shape[2]
    tm = min(TM_BIG, m)
    return pl.pallas_call(
        functools.partial(_nm_mm_kernel, f32_col=f32_col),
        grid=(m // tm, n // tn),
        in_specs=[pl.BlockSpec((tm, d), lambda i, j: (i, 0)),
                  pl.BlockSpec((1, d), lambda i, j: (0, 0)),
                  _mod_spec(sc_chunk, rowf, tm), _mod_spec(sh_chunk, rowf, tm),
                  pl.BlockSpec((None, d, tn), lambda i, j: (l, 0, j))],
        out_specs=[pl.BlockSpec((tm, tn), lambda i, j: (i, j)),
                   pl.BlockSpec((tm, f32_cols), lambda i, j: (i, 0)),
                   pl.BlockSpec((f32_cols, tm), lambda i, j: (0, i))],
        out_shape=[jax.ShapeDtypeStruct((m, n), BF16), jax.ShapeDtypeStruct((m, f32_cols), F32),
                   jax.ShapeDtypeStruct((f32_cols, m), F32)],
        scratch_shapes=[pltpu.VMEM((tm, d), BF16)],
        compiler_params=_cp(("parallel", "arbitrary")),
    )(x, g.reshape(1, d), mod4, mod4, w)


def _mla_up_kernel(ql_ref, kvl_ref, kr_ref, tab_ref, qg_ref, kvg_ref, wuq_ref, wuk_ref, wuv_ref,
                   q_ref, k_ref, v_ref, *, scale):
    tab = tab_ref[...]
    qn = _rms(ql_ref[...].astype(F32), qg_ref[...]).astype(BF16)
    qf = jnp.dot(qn, wuq_ref[...], preferred_element_type=F32)
    for h in range(MLA_HEADS):
        c0 = h * MLA_DQ
        y = qf[:, c0 + MLA_NOPE:c0 + MLA_DQ] * tab
        z = y + pltpu.roll(y, MLA_ROPE, axis=1)
        q_ref[:, c0:c0 + MLA_NOPE] = (qf[:, c0:c0 + MLA_NOPE] * scale).astype(BF16)
        q_ref[:, c0 + MLA_NOPE:c0 + MLA_DQ] = (z * scale).astype(BF16)
    cn = _rms(kvl_ref[...].astype(F32), kvg_ref[...]).astype(BF16)
    kn = jnp.dot(cn, wuk_ref[...], preferred_element_type=F32)
    v_ref[...] = jnp.dot(cn, wuv_ref[...], preferred_element_type=F32).astype(BF16)
    ky = kr_ref[...].astype(F32) * tab
    kz = ky + pltpu.roll(ky, MLA_ROPE, axis=1)
    lane = lax.broadcasted_iota(jnp.int32, kz.shape, 1)
    kz = jnp.where(lane < MLA_ROPE, kz, 0.0).astype(BF16)
    for h in range(MLA_HEADS):
        c0 = h * MLA_DQ
        k_ref[:, c0:c0 + MLA_NOPE] = kn[:, h * MLA_NOPE:(h + 1) * MLA_NOPE].astype(BF16)
        k_ref[:, c0 + MLA_NOPE:c0 + MLA_DQ] = kz


def _mla_up(proj, tab, qg, kvg, wuq, wuk, wuv):
    m = proj.shape[0]
    tm = min(TM, tab.shape[0])
    nt = tab.shape[0] // tm
    full = lambda a: pl.BlockSpec(a.shape, lambda i: (0, 0))
    return pl.pallas_call(
        functools.partial(_mla_up_kernel, scale=(MLA_NOPE + MLA_ROPE) ** -0.5 * LOG2E),
        grid=(m // tm,),
        in_specs=[pl.BlockSpec((tm, MLA_Q_LORA), lambda i: (i, C_QLAT // MLA_Q_LORA)),
                  pl.BlockSpec((tm, MLA_KV_LORA), lambda i: (i, C_KVLAT // MLA_KV_LORA)),
                  pl.BlockSpec((tm, LANE), lambda i: (i, C_KROPE // LANE)),
                  pl.BlockSpec((tm, LANE), lambda i: (i % nt, 0)),
                  full(qg), full(kvg), full(wuq), full(wuk), full(wuv)],
        out_specs=[pl.BlockSpec((tm, MLA_HEADS * MLA_DQ), lambda i: (i, 0)),
                   pl.BlockSpec((tm, MLA_HEADS * MLA_DQ), lambda i: (i, 0)),
                   pl.BlockSpec((tm, MLA_W), lambda i: (i, 0))],
        out_shape=[jax.ShapeDtypeStruct((m, MLA_HEADS * MLA_DQ), BF16),
                   jax.ShapeDtypeStruct((m, MLA_HEADS * MLA_DQ), BF16),
                   jax.ShapeDtypeStruct((m, MLA_W), BF16)],
        compiler_params=_cp(("parallel",)),
    )(proj, proj, proj, tab, qg, kvg, wuq, wuk, wuv)


def _flash_kernel(*refs, has_lat, scale, nsub, tk):
    if has_lat:
        q_ref, kc_ref, vc_ref, kl_ref, vl_ref, o_ref, m_scr, acc_scr = refs
    else:
        q_ref, kc_ref, vc_ref, o_ref, m_scr, acc_scr = refs
    ts = q_ref.shape[0] // nsub
    dv = vc_ref.shape[1]
    dn = (((1,), (1,)), ((), ()))

    def scores(q, k):
        s = lax.dot_general(q, k, dn, preferred_element_type=F32)
        return s if scale == 1.0 else s * scale

    def with_ones(v):
        lane = lax.broadcasted_iota(jnp.int32, v.shape, 1)
        return jnp.concatenate([v, jnp.where(lane == 0, 1.0, 0.0).astype(v.dtype)], axis=1)

    def lanes(x, n):
        return x if n == LANE else pltpu.repeat(x, n // LANE, axis=1)

    def probs(s, m):
        return jnp.exp2(s - lanes(m, s.shape[1])).astype(BF16)

    kc = kc_ref[...]
    vc = with_ones(vc_ref[...])
    for u in range(nsub):
        s = scores(q_ref[u * ts:(u + 1) * ts, :], kc)
        m = jnp.broadcast_to(jnp.max(s, axis=-1, keepdims=True), (ts, LANE))
        m_scr[u] = m
        acc_scr[u] = jnp.dot(probs(s, m), vc, preferred_element_type=F32)

    if has_lat:
        def body(kb, carry):
            off = pl.multiple_of(kb * tk, tk)
            k = kl_ref[pl.ds(off, tk), :]
            v = with_ones(vl_ref[pl.ds(off, tk), :])
            s_next = scores(q_ref[0:ts, :], k)
            for u in range(nsub):
                s = s_next
                if u + 1 < nsub:
                    s_next = scores(q_ref[(u + 1) * ts:(u + 2) * ts, :], k)
                m_prev = m_scr[u]
                m_new = jnp.maximum(m_prev, jnp.max(s, axis=-1, keepdims=True))
                m_scr[u] = m_new
                acc_scr[u] = (lanes(jnp.exp2(m_prev - m_new), 2 * dv) * acc_scr[u]
                              + jnp.dot(probs(s, m_new), v, preferred_element_type=F32))
            return carry

        lax.fori_loop(0, kl_ref.shape[0] // tk, body, 0)

    for u in range(nsub):
        acc = acc_scr[u]
        o_ref[u * ts:(u + 1) * ts, :] = (acc[:, :dv] / acc[:, dv:dv + 1]).astype(o_ref.dtype)


def _flash(q, qcol, kc, kccol, vc, vccol, heads, dq, dv, batch, lat=None, scale=1.0, tq=2048, nsub=2, tk=2048):
    mq = q.shape[0]
    sq = mq // batch
    tq = min(tq, sq)
    nsub = min(nsub, tq // 256)
    nq = sq // tq
    tc = kc.shape[0] // batch
    has_lat = lat is not None
    in_specs = [pl.BlockSpec((tq, dq), lambda b, h, i: (b * nq + i, qcol + h)),
                pl.BlockSpec((tc, dq), lambda b, h, i: (b, kccol + h)),
                pl.BlockSpec((tc, dv), lambda b, h, i: (b, vccol + h))]
    args = [q, kc, vc]
    if has_lat:
        kl, klcol, vl, vlcol = lat
        sk = kl.shape[0] // batch
        tk = min(tk, sk)
        in_specs += [pl.BlockSpec((sk, dq), lambda b, h, i: (b, klcol + h)),
                     pl.BlockSpec((sk, dv), lambda b, h, i: (b, vlcol + h))]
        args += [kl, vl]
    return pl.pallas_call(
        functools.partial(_flash_kernel, has_lat=has_lat, scale=scale, nsub=nsub, tk=tk),
        grid=(batch, heads, nq),
        in_specs=in_specs,
        out_specs=pl.BlockSpec((tq, dv), lambda b, h, i: (b * nq + i, h)),
        out_shape=jax.ShapeDtypeStruct((mq, heads * dv), BF16),
        scratch_shapes=[pltpu.VMEM((nsub, tq // nsub, LANE), F32), pltpu.VMEM((nsub, tq // nsub, 2 * dv), F32)],
        compiler_params=_cp(("parallel", "parallel", "arbitrary")),
    )(*args)


NA_RB = 8
NA_BLK = NA_RB * GRID_W
NA_GROUP = 2


def _na_kernel(q_ref, kp_ref, kc_ref, kn_ref, vp_ref, vc_ref, vn_ref, kx_ref, vx_ref, bias_ref, o_ref,
               kw_scr, vw_scr, *, rows, scale):
    j = pl.program_id(1)
    for t, (kr, vr) in enumerate(((kp_ref, vp_ref), (kc_ref, vc_ref), (kn_ref, vn_ref))):
        kw_scr[t * NA_BLK:(t + 1) * NA_BLK, :] = kr[...]
        vw_scr[t * NA_BLK:(t + 1) * NA_BLK, :] = vr[...]
    nwin = NA_KR * GRID_W
    dn = (((1,), (1,)), ((), ()))
    heads = [slice(h * NA_DH, (h + 1) * NA_DH) for h in range(NA_HEADS)]
    for i0 in range(0, NA_RB, NA_GROUP):
        units = []
        for i in range(i0, i0 + NA_GROUP):
            r = NA_RB * j + i
            r0 = jnp.clip(r - NA_KR // 2, 0, rows - NA_KR)
            start = pl.multiple_of((r0 - NA_RB * j + NA_RB) * GRID_W, GRID_W)
            for h, cs in enumerate(heads):
                q = q_ref[i * GRID_W:(i + 1) * GRID_W, cs]
                s_loc = (lax.dot_general(q, kw_scr[pl.ds(start, nwin), cs], dn, preferred_element_type=F32) * scale
                         + bias_ref[h, r - r0])
                s_ctx = lax.dot_general(q, kx_ref[:, cs], dn, preferred_element_type=F32) * scale
                units.append((i, cs, start, s_loc, s_ctx))
        outs = {}
        for i, cs, start, s_loc, s_ctx in units:
            m = jnp.maximum(jnp.max(s_loc, axis=-1, keepdims=True), jnp.max(s_ctx, axis=-1, keepdims=True))
            p_loc = jnp.exp(s_loc - m)
            p_ctx = jnp.exp(s_ctx - m)
            l = jnp.sum(p_loc, axis=-1, keepdims=True) + jnp.sum(p_ctx, axis=-1, keepdims=True)
            o = (jnp.dot(p_loc.astype(BF16), vw_scr[pl.ds(start, nwin), cs], preferred_element_type=F32)
                 + jnp.dot(p_ctx.astype(BF16), vx_ref[:, cs], preferred_element_type=F32))
            outs.setdefault(i, []).append((o / l).astype(o_ref.dtype))
        for i, parts in outs.items():
            o_ref[i * GRID_W:(i + 1) * GRID_W, :] = jnp.concatenate(parts, axis=1)


def _na_bias_table(rpb):
    c = np.arange(GRID_W)
    cs = np.clip(c - NA_KC // 2, 0, GRID_W - NA_KC)
    valid = (c[None, :] >= cs[:, None]) & (c[None, :] < cs[:, None] + NA_KC)
    dc = np.clip(c[None, :] - c[:, None] + (NA_KC - 1), 0, 2 * NA_KC - 2)
    dr = np.arange(NA_KR)[None, :] - np.arange(NA_KR)[:, None] + (NA_KR - 1)
    sel_r = (dr[:, :, None] == np.arange(2 * NA_KR - 1)).astype(np.float32)
    sel_c = ((dc[:, :, None] == np.arange(2 * NA_KC - 1)) & valid[:, :, None]).astype(np.float32)
    tab = jnp.einsum('lhab,via,cdb->lhvcid', rpb.astype(F32), sel_r, sel_c, precision=lax.Precision.HIGHEST)
    tab = tab + jnp.where(valid[None, None, None, :, None, :], 0.0, -1e30)
    return tab.reshape(rpb.shape[0], NA_HEADS, NA_KR, GRID_W, NA_KR * GRID_W)


def _na(proj_l, proj_c, bias, l, batch):
    m = proj_l.shape[0]
    s = m // batch
    rows = s // GRID_W
    nb = rows // NA_RB
    cq, ck, cv = C_NAQ // NA_W, C_NAK // NA_W, C_NAV // NA_W
    tc = proj_c.shape[0] // batch
    prev = lambda b, j: b * nb + jnp.maximum(j - 1, 0)
    cur = lambda b, j: b * nb + j
    nxt = lambda b, j: b * nb + jnp.minimum(j + 1, nb - 1)
    blk = lambda rf, col: pl.BlockSpec((NA_BLK, NA_W), lambda b, j: (rf(b, j), col))
    return pl.pallas_call(
        functools.partial(_na_kernel, rows=rows, scale=NA_DH ** -0.5),
        grid=(batch, nb),
        in_specs=[blk(cur, cq), blk(prev, ck), blk(cur, ck), blk(nxt, ck),
                  blk(prev, cv), blk(cur, cv), blk(nxt, cv),
                  pl.BlockSpec((tc, NA_W), lambda b, j: (b, ck)),
                  pl.BlockSpec((tc, NA_W), lambda b, j: (b, cv)),
                  pl.BlockSpec((None,) + bias.shape[1:], lambda b, j: (l, 0, 0, 0, 0))],
        out_specs=pl.BlockSpec((NA_BLK, NA_W), lambda b, j: (b * nb + j, 0)),
        out_shape=jax.ShapeDtypeStruct((m, NA_W), BF16),
        scratch_shapes=[pltpu.VMEM((3 * NA_BLK, NA_W), BF16), pltpu.VMEM((3 * NA_BLK, NA_W), BF16)],
        compiler_params=_cp(("parallel", "arbitrary")),
    )(proj_l, proj_l, proj_l, proj_l, proj_l, proj_l, proj_l, proj_c, proj_c, bias)


def _ml_prep_kernel(x_ref, p_ref, n_ref, w_ref, b_ref, sc_ref, o_ref, *, tiles_per_seg, n_conv):
    j = pl.program_id(1)
    x = x_ref[...].astype(F32)
    tm = x.shape[0]

    def widen(y, fill):
        lane = lax.broadcasted_iota(jnp.int32, (tm, ML_DP - ML_DH), 1)
        spare = jnp.where(lane == 0, fill, 0.0)
        return jnp.concatenate([y[:, :ML_DH], spare, y[:, ML_DH:], spare], axis=1).astype(o_ref.dtype)

    @pl.when(j < n_conv)
    def _():
        t = pl.program_id(0) % tiles_per_seg
        prev_row = jnp.where(t == 0, 0.0, p_ref[HALO - 1:HALO, :].astype(F32))
        next_row = jnp.where(t == tiles_per_seg - 1, 0.0, n_ref[0:1, :].astype(F32))
        rid = lax.broadcasted_iota(jnp.int32, x.shape, 0)
        xp = jnp.where(rid == 0, prev_row, pltpu.roll(x, 1, axis=0))
        xn = jnp.where(rid == tm - 1, next_row, pltpu.roll(x, tm - 1, axis=0))
        w = w_ref[...]
        y = xp * w[0:1, :] + x * w[1:2, :] + xn * w[2:3, :] + b_ref[...]
        o_ref[...] = widen(y * _sigmoid(y) * sc_ref[...], 0.0)

    @pl.when(j >= n_conv)
    def _():
        o_ref[...] = widen(x, 1.0)


def _ml_prep(proj, w, b, colscale, seg):
    m = proj.shape[0]
    tm = min(TM_BIG, seg)
    hb = tm // HALO
    last = m // HALO - 1
    n_conv = w.shape[1] // ML_PAIR
    n_all = 3 * ML_W // ML_PAIR
    vec = lambda r: pl.BlockSpec((r, ML_PAIR), lambda i, j: (0, jnp.minimum(j, n_conv - 1)))
    return pl.pallas_call(
        functools.partial(_ml_prep_kernel, tiles_per_seg=seg // tm, n_conv=n_conv),
        grid=(m // tm, n_all),
        in_specs=[pl.BlockSpec((tm, ML_PAIR), lambda i, j: (i, j)),
                  pl.BlockSpec((HALO, ML_PAIR), lambda i, j: (jnp.maximum(i * hb - 1, 0), j)),
                  pl.BlockSpec((HALO, ML_PAIR), lambda i, j: (jnp.minimum((i + 1) * hb, last), j)),
                  vec(3), vec(1), vec(1)],
        out_specs=pl.BlockSpec((tm, 2 * ML_DP), lambda i, j: (i, j)),
        out_shape=jax.ShapeDtypeStruct((m, 3 * ML_WP), BF16),
        compiler_params=_cp(("parallel", "parallel")),
    )(proj, proj, proj, w, b, colscale)


def _split3(x):
    x1 = x.astype(BF16)
    r = x - x1.astype(F32)
    x2 = r.astype(BF16)
    x3 = (r - x2.astype(F32)).astype(BF16)
    return x1, x2, x3


def _log_sigmoid(x):
    return jnp.minimum(x, 0.0) - jnp.log(1.0 + jnp.exp(-jnp.abs(x)))


def _mlstm_dir(d, q_ref, k_ref, v_ref, gc_ref, gr_ref, bc_ref, br_ref, h_ref, c_scr, m_scr):
    L = ML_L
    ri = lax.broadcasted_iota(jnp.int32, (L, L), 0)
    ci = lax.broadcasted_iota(jnp.int32, (L, L), 1)
    mask = (ci <= ri) if d == 0 else (ci >= ri)
    tri = jnp.where(mask, 1.0, 0.0).astype(BF16)
    tri_t = jnp.where((ri <= ci) if d == 0 else (ri >= ci), 1.0, 0.0).astype(BF16)
    lane = lax.broadcasted_iota(jnp.int32, (L, ML_DP), 1)

    gc = gc_ref[...] + bc_ref[...]
    gr = gr_ref[...] + br_ref[...]
    bc_all = sum(jnp.dot(tri, p, preferred_element_type=F32) for p in _split3(_log_sigmoid(gc)))
    br_all = sum(jnp.dot(p, tri_t, preferred_element_type=F32) for p in _split3(_log_sigmoid(gr)))

    hs = range(ML_HEADS)
    cols = [slice(h * ML_DP, (h + 1) * ML_DP) for h in hs]
    q = [q_ref[:, cs] for cs in cols]
    k = [k_ref[:, cs] for cs in cols]
    v = [v_ref[:, cs] for cs in cols]
    m_prev = [m_scr[h, d][0:1, 0:1] for h in hs]
    ct = [c_scr[h, d] for h in hs]
    qk = [lax.dot_general(q[h], k[h], (((1,), (1,)), ((), ())), preferred_element_type=F32) for h in hs]
    inter = [jnp.dot(q[h], ct[h].astype(BF16), preferred_element_type=F32) for h in hs]

    rep = lambda col: jnp.broadcast_to(col, (L, LANE))
    wide = lambda x: pltpu.repeat(x, ML_DP // LANE, axis=1)
    b_rep = [rep(bc_all[:, 4 * h + 2 + d:4 * h + 3 + d]) for h in hs]
    li_rep = [rep(gc[:, 4 * h + d:4 * h + d + 1]) for h in hs]
    a, w_st, m_t = [], [], []
    for h in hs:
        b_row = br_all[4 * h + 2 + d:4 * h + 3 + d, :]
        li_row = gr[4 * h + d:4 * h + d + 1, :]
        dmat = jnp.where(mask, wide(b_rep[h]) - b_row + li_row, -jnp.inf)
        g = b_rep[h] + m_prev[h]
        m_t.append(jnp.maximum(g, jnp.max(dmat, axis=-1, keepdims=True)))
        w_st.append(jnp.exp(g - m_t[h]))
        a.append((qk[h] * jnp.exp(dmat - wide(m_t[h]))).astype(BF16))

    for h in hs:
        num = wide(w_st[h]) * inter[h] + jnp.dot(a[h], v[h], preferred_element_type=F32)
        den = rep(num[:, ML_DH:ML_DH + 1])
        inv = 1.0 / jnp.maximum(jnp.abs(den), jnp.exp(-m_t[h]))
        h_ref[:, cols[h]] = jnp.where(lane < ML_DH, num * wide(inv), 0.0).astype(h_ref.dtype)

    for h in hs:
        b_end = b_rep[h][L - 1:L, :] if d == 0 else b_rep[h][0:1, :]
        e_log = b_end - b_rep[h] + li_rep[h]
        m_new = jnp.maximum(b_end + m_prev[h], jnp.max(e_log, axis=0, keepdims=True))
        decay = jnp.exp(b_end + m_prev[h] - m_new)[:, 0:1]
        ek = (wide(jnp.exp(e_log - m_new)) * k[h].astype(F32)).astype(BF16)
        c_scr[h, d] = decay * ct[h] + lax.dot_general(ek, v[h], (((0,), (0,)), ((), ())),
                                                      preferred_element_type=F32)
        m_scr[h, d] = jnp.broadcast_to(m_new, m_scr.shape[2:])


def _mlstm_kernel(qf_ref, kf_ref, vf_ref, gcf_ref, grf_ref, qb_ref, kb_ref, vb_ref, gcb_ref, grb_ref,
                  bc_ref, br_ref, c0_ref, m0_ref, hf_ref, hb_ref, c1_ref, m1_ref, c_scr, m_scr):
    c = pl.program_id(1)

    @pl.when(c == 0)
    def _():
        c_scr[...] = c0_ref[...]
        m_scr[...] = m0_ref[...]

    _mlstm_dir(0, qf_ref, kf_ref, vf_ref, gcf_ref, grf_ref, bc_ref, br_ref, hf_ref, c_scr, m_scr)
    _mlstm_dir(1, qb_ref, kb_ref, vb_ref, gcb_ref, grb_ref, bc_ref, br_ref, hb_ref, c_scr, m_scr)

    @pl.when(c == pl.num_programs(1) - 1)
    def _():
        c1_ref[...] = c_scr[...]
        m1_ref[...] = m_scr[...]


def _mlstm(qkv, gates, gates_row, bias_col, bias_row, c0, m0, batch):
    m = qkv.shape[0]
    s = m // batch
    nc = s // ML_L
    fwd = lambda b, c: b * nc + c
    bwd = lambda b, c: b * nc + (nc - 1 - c)

    def dir_specs(rf):
        return [pl.BlockSpec((ML_L, ML_WP), lambda b, c: (rf(b, c), 0)),
                pl.BlockSpec((ML_L, ML_WP), lambda b, c: (rf(b, c), 1)),
                pl.BlockSpec((ML_L, ML_WP), lambda b, c: (rf(b, c), 2)),
                pl.BlockSpec((ML_L, C_GATES), lambda b, c: (rf(b, c), 0)),
                pl.BlockSpec((4 * ML_HEADS, ML_L), lambda b, c: (0, rf(b, c)))]

    st_c = pl.BlockSpec((None, ML_HEADS, 2, ML_DP, ML_DP), lambda b, c: (b, 0, 0, 0, 0))
    st_m = pl.BlockSpec((None, ML_HEADS, 2, 8, LANE), lambda b, c: (b, 0, 0, 0, 0))
    return pl.pallas_call(
        _mlstm_kernel,
        grid=(batch, nc),
        in_specs=dir_specs(fwd) + dir_specs(bwd)
        + [pl.BlockSpec((1, C_GATES), lambda b, c: (0, 0)),
           pl.BlockSpec((4 * ML_HEADS, 1), lambda b, c: (0, 0)), st_c, st_m],
        out_specs=[pl.BlockSpec((ML_L, ML_WP), lambda b, c: (fwd(b, c), 0)),
                   pl.BlockSpec((ML_L, ML_WP), lambda b, c: (bwd(b, c), 0)), st_c, st_m],
        out_shape=[jax.ShapeDtypeStruct((m, ML_WP), BF16), jax.ShapeDtypeStruct((m, ML_WP), BF16),
                   jax.ShapeDtypeStruct(c0.shape, F32), jax.ShapeDtypeStruct(m0.shape, F32)],
        scratch_shapes=[pltpu.VMEM((ML_HEADS, 2, ML_DP, ML_DP), F32), pltpu.VMEM((ML_HEADS, 2, 8, LANE), F32)],
        compiler_params=_cp(("parallel", "arbitrary")),
    )(qkv, qkv, qkv, gates, gates_row, qkv, qkv, qkv, gates, gates_row, bias_col, bias_row, c0, m0)


def _outproj_kernel(mla_ref, hf_ref, hb_ref, og_ref, na_ref, w1_ref, w2_ref, w3_ref, res_ref, gate_ref,
                    out_ref, ml_scr):
    @pl.when(pl.program_id(1) == 0)
    def _():
        og = _sigmoid(og_ref[...].astype(F32))
        for h in range(ML_HEADS):
            cs = slice(h * ML_DP, (h + 1) * ML_DP)
            o_h = jnp.concatenate([og[:, h * ML_DH:(h + 1) * ML_DH],
                                   jnp.zeros((og.shape[0], ML_DP - ML_DH), F32)], axis=1)
            ml_scr[:, cs] = ((hf_ref[:, cs].astype(F32) + hb_ref[:, cs].astype(F32)) * o_h).astype(BF16)

    acc = (jnp.dot(mla_ref[...], w1_ref[...], preferred_element_type=F32)
           + jnp.dot(ml_scr[...], w2_ref[...], preferred_element_type=F32)
           + jnp.dot(na_ref[...], w3_ref[...], preferred_element_type=F32))
    out_ref[...] = res_ref[...] + gate_ref[...] * acc


def _outproj(mla, hf, hb, proj, na, w_out, w2, l, res, mod4, gate_chunk, rowf):
    m, d = res.shape
    tm = min(TM, m)
    tn = d
    rows = lambda width, col: pl.BlockSpec((tm, width), lambda i, j: (i, col))
    wrows = lambda nrows, blk: pl.BlockSpec((None, nrows, tn), lambda i, j: (l, blk, j),
                                            pipeline_mode=pl.Buffered(1))
    return pl.pallas_call(
        _outproj_kernel,
        grid=(m // tm, d // tn),
        in_specs=[rows(MLA_W, 0), rows(ML_WP, 0), rows(ML_WP, 0), rows(ML_W, C_MLO // ML_W), rows(NA_W, 0),
                  wrows(MLA_W, 0), wrows(ML_WP, 0), wrows(NA_W, (MLA_W + ML_W) // NA_W),
                  pl.BlockSpec((tm, tn), lambda i, j: (i, j)), _mod_spec(gate_chunk, rowf, tm, tn)],
        out_specs=pl.BlockSpec((tm, tn), lambda i, j: (i, j)),
        out_shape=jax.ShapeDtypeStruct((m, d), F32),
        scratch_shapes=[pltpu.VMEM((tm, ML_WP), BF16)],
        compiler_params=_cp(("parallel", "arbitrary")),
    )(mla, hf, hb, proj, na, w_out, w2, w_out, res, mod4)


def _ffn_kernel(x_ref, xp_ref, xn_ref, g_ref, sc_ref, sh_ref, wg_ref, wv_ref, cwg_ref, cwv_ref, cbg_ref, cbv_ref,
                wd_ref, gate_ref, *rest, tiles_per_seg, final):
    if final:
        fg_ref, o_ref, hn_scr = rest
    else:
        o_ref, hn_scr = rest
    i = pl.program_id(0)
    j = pl.program_id(1)
    tm = x_ref.shape[0]
    te = hn_scr.shape[0]

    @pl.when(j == 0)
    def _():
        _norm_mod_rows(xp_ref, hn_scr, 0, g_ref, sc_ref, sh_ref)
        _norm_mod_rows(x_ref, hn_scr, HALO, g_ref, sc_ref, sh_ref)
        _norm_mod_rows(xn_ref, hn_scr, HALO + tm, g_ref, sc_ref, sh_ref)
        o_ref[...] = jnp.zeros_like(o_ref)

    t = i % tiles_per_seg
    keep_prev = jnp.where(t == 0, 0.0, 1.0)
    keep_next = jnp.where(t == tiles_per_seg - 1, 0.0, 1.0)
    hn = hn_scr[...]

    def conv(w_ref, cw_ref, cb_ref):
        u = jnp.dot(hn, w_ref[...], preferred_element_type=F32)
        rid = lax.broadcasted_iota(jnp.int32, (tm, u.shape[1]), 0)
        up = pltpu.roll(u, 1, axis=0)[HALO:HALO + tm]
        un = pltpu.roll(u, te - 1, axis=0)[HALO:HALO + tm]
        up = jnp.where(rid == 0, up * keep_prev, up)
        un = jnp.where(rid == tm - 1, un * keep_next, un)
        cw = cw_ref[...]
        return up * cw[0:1, :] + u[HALO:HALO + tm] * cw[1:2, :] + un * cw[2:3, :] + cb_ref[...]

    gg = conv(wg_ref, cwg_ref, cbg_ref)
    vv = conv(wv_ref, cwv_ref, cbv_ref)
    act = (gg * _sigmoid(gg) * vv).astype(BF16)
    o_ref[...] += jnp.dot(act, wd_ref[...], preferred_element_type=F32)

    @pl.when(j == pl.num_programs(1) - 1)
    def _():
        if final:
            o_ref[...] = _rms(x_ref[...] + gate_ref[...] * o_ref[...], fg_ref[...])
        else:
            gate = gate_ref[...]
            n_groups = tm // ROWS_F32

            def body(r, carry):
                rows = pl.ds(pl.multiple_of(r * ROWS_F32, ROWS_F32), ROWS_F32)
                o_ref[rows, :] = x_ref[rows, :] + gate * o_ref[rows, :]
                return carry

            lax.fori_loop(0, n_groups, body, 0, unroll=min(16, n_groups))


def _ffn(x, g, mod4, rowf, w_up, cw, cb, w_down, l, seg, final_g=None, tf=512):
    m, d = x.shape
    dff = w_down.shape[1]
    tm = min(TM_BIG, seg)
    nj = dff // tf
    hb = tm // HALO
    last = m // HALO - 1
    final = final_g is not None
    vec = lambda r, off: pl.BlockSpec((None, r, tf), lambda i, j: (l, 0, off + j))
    in_specs = [pl.BlockSpec((tm, d), lambda i, j: (i, 0)),
                pl.BlockSpec((HALO, d), lambda i, j: (jnp.maximum(i * hb - 1, 0), 0)),
                pl.BlockSpec((HALO, d), lambda i, j: (jnp.minimum((i + 1) * hb, last), 0)),
                pl.BlockSpec((1, d), lambda i, j: (0, 0)),
                _mod_spec(4, rowf, tm), _mod_spec(3, rowf, tm),
                pl.BlockSpec((None, d, tf), lambda i, j: (l, 0, j)),
                pl.BlockSpec((None, d, tf), lambda i, j: (l, 0, nj + j)),
                vec(3, 0), vec(3, nj), vec(1, 0), vec(1, nj),
                pl.BlockSpec((None, tf, d), lambda i, j: (l, j, 0)), _mod_spec(5, rowf, tm)]
    args = [x, x, x, g.reshape(1, d), mod4, mod4, w_up, w_up, cw, cw, cb, cb, w_down, mod4]
    if final:
        in_specs.append(pl.BlockSpec((1, d), lambda i, j: (0, 0)))
        args.append(final_g.reshape(1, d))
    return pl.pallas_call(
        functools.partial(_ffn_kernel, tiles_per_seg=seg // tm, final=final),
        grid=(m // tm, nj),
        in_specs=in_specs,
        out_specs=pl.BlockSpec((tm, d), lambda i, j: (i, 0), pipeline_mode=pl.Buffered(1) if final else None),
        out_shape=jax.ShapeDtypeStruct((m, d), F32),
        scratch_shapes=[pltpu.VMEM((tm + 2 * HALO, d), BF16)],
        compiler_params=_cp(("parallel", "arbitrary")),
    )(*args)


def _pad_heads(a, heads, dh, dp, axis):
    shp = a.shape
    a = a.reshape(shp[:axis] + (heads, dh) + shp[axis + 1:])
    pad = [(0, 0)] * a.ndim
    pad[axis + 1] = (0, dp - dh)
    a = jnp.pad(a, pad)
    return a.reshape(shp[:axis] + (heads * dp,) + shp[axis + 1:])


def _rope_cols(a):
    ev, od = a[..., 0::2], a[..., 1::2]
    return jnp.concatenate([ev, od, od, ev], axis=-1)


def _prep_w_in(w):
    w = w.astype(BF16)
    lead = w.shape[:-1]
    o_rope, o_ml, o_gi, o_na = IN_OFFSETS[1], IN_OFFSETS[2], IN_OFFSETS[6], IN_OFFSETS[8]
    gi = w[..., o_gi:o_gi + 2 * ML_HEADS].reshape(lead + (2, ML_HEADS))
    gf = w[..., o_gi + 2 * ML_HEADS:o_na].reshape(lead + (2, ML_HEADS))
    g4 = jnp.stack([gi[..., 0, :], gi[..., 1, :], gf[..., 0, :], gf[..., 1, :]], axis=-1)
    gates = g4.reshape(lead + (4 * ML_HEADS,))

    def place(a, col):
        return jnp.pad(a, [(0, 0)] * len(lead) + [(col, C_IN - col - a.shape[-1])])

    return (place(w[..., o_ml:o_gi], C_MLQ) + place(w[..., :o_rope], C_QLAT)
            + place(_rope_cols(w[..., o_rope:o_ml]), C_KROPE) + place(gates, C_GATE0) + place(w[..., o_na:], C_NAQ))


def _prep_w_uq(w):
    w = w.reshape(w.shape[0], MLA_HEADS, MLA_NOPE + MLA_ROPE)
    out = jnp.concatenate([w[..., :MLA_NOPE], _rope_cols(w[..., MLA_NOPE:])], axis=-1)
    return out.reshape(w.shape[0], MLA_HEADS * MLA_DQ).astype(BF16)


def _gate_bias(i_bias, f_bias):
    g4 = jnp.stack([i_bias[0], i_bias[1], f_bias[0], f_bias[1]], axis=-1).astype(F32).reshape(4 * ML_HEADS)
    return jnp.pad(g4, (0, LANE - 4 * ML_HEADS)).reshape(1, LANE), g4.reshape(4 * ML_HEADS, 1)


def _rope_table(n_tokens):
    t = jnp.arange(n_tokens)
    row = (t // GRID_W).astype(F32)
    col = (t % GRID_W).astype(F32)
    n_freq = MLA_ROPE // 4
    inv = ROPE_BASE ** (-jnp.arange(n_freq, dtype=F32) / n_freq)
    ang = jnp.concatenate([row[:, None] * inv, col[:, None] * inv], axis=-1)
    cos, sin = jnp.cos(ang), jnp.sin(ang)
    return jnp.concatenate([cos, cos, -sin, sin], axis=-1)


def kernel(x, c, ctx, c_ctx, ada_w, ada_b, norm1_g, norm2_g, w_in, mla_q_norm, mla_kv_norm, mla_w_uq, mla_w_uk, mla_w_uv, ml_conv_w, ml_conv_b, ml_i_bias, ml_f_bias, na_rpb, w_out, ffn_w_up, ffn_conv_w, ffn_conv_b, ffn_w_down, final_norm_g):
    batch, seq, d = x.shape
    tctx = ctx.shape[1]
    depth = ada_w.shape[0]
    assert d == D_MODEL and batch <= 7 and tctx == ML_L and tctx % HALO == 0
    assert seq % TM_BIG == 0 and seq % NA_BLK == 0 and seq // GRID_W >= NA_KR

    h = x.reshape(batch * seq, d)
    hc = ctx.reshape(batch * tctx, d)
    row_lat = lambda i, tm: i // (seq // tm)
    row_ctx = lambda i, tm: batch
    cond8 = jnp.zeros((8, d), F32).at[:batch].set(c).at[batch].set(c_ctx)

    tab_lat = _rope_table(seq)
    tab_ctx = jnp.concatenate([jnp.ones((tctx, MLA_ROPE), F32), jnp.zeros((tctx, MLA_ROPE), F32)], axis=-1)
    ml_scale = jnp.concatenate([jnp.ones((1, ML_W), F32), jnp.full((1, ML_W), ML_DH ** -0.5, F32)], axis=-1)

    w_in_p = _prep_w_in(w_in)
    w_out_p = w_out.astype(BF16)
    w2_p = _pad_heads(w_out_p[:, MLA_W:MLA_W + ML_W], ML_HEADS, ML_DH, ML_DP, 1)
    w_up_p = ffn_w_up.astype(BF16)
    w_down_p = ffn_w_down.astype(BF16)
    fcb = ffn_conv_b.reshape(depth, 1, -1)
    na_bias = _na_bias_table(na_rpb)

    for l in range(depth):
        last = l == depth - 1
        mod4 = _ada(cond8, ada_w, ada_b, l).reshape(8, N_MOD, 1, d)

        wuq = _prep_w_uq(mla_w_uq[l])
        wuk = mla_w_uk[l].astype(BF16)
        wuv = mla_w_uv[l].astype(BF16)
        qg = mla_q_norm[l].reshape(1, -1)
        kvg = mla_kv_norm[l].reshape(1, -1)
        cw = ml_conv_w[l]
        cb = ml_conv_b[l].reshape(1, -1)
        bias_col, bias_row = _gate_bias(ml_i_bias[l], ml_f_bias[l])

        proj_c, gates_c, grow_c = _nm_mm(hc, norm1_g[l], mod4, 1, 0, row_ctx, w_in_p, l, TN_IN, C_GATE0, C_GATES)
        proj_l, gates_l, grow_l = _nm_mm(h, norm1_g[l], mod4, 1, 0, row_lat, w_in_p, l, TN_IN, C_GATE0, C_GATES)

        q_c, k_c, v_c = _mla_up(proj_c, tab_ctx, qg, kvg, wuq, wuk, wuv)
        q_l, k_l, v_l = _mla_up(proj_l, tab_lat, qg, kvg, wuq, wuk, wuv)
        mla_lat = _flash(q_l, 0, k_c, 0, v_c, 0, MLA_HEADS, MLA_DQ, MLA_V, batch, lat=(k_l, 0, v_l, 0))

        qkv_c = _ml_prep(proj_c, cw, cb, ml_scale, tctx)
        qkv_l = _ml_prep(proj_l, cw, cb, ml_scale, seq)
        c0 = jnp.zeros((batch, ML_HEADS, 2, ML_DP, ML_DP), F32)
        m0 = jnp.zeros((batch, ML_HEADS, 2, 8, LANE), F32)
        hf_c, hb_c, c1, m1 = _mlstm(qkv_c, gates_c, grow_c, bias_col, bias_row, c0, m0, batch)
        hf_l, hb_l, _, _ = _mlstm(qkv_l, gates_l, grow_l, bias_col, bias_row, c1, m1, batch)

        na_lat = _na(proj_l, proj_c, na_bias, l, batch)

        h = _outproj(mla_lat, hf_l, hb_l, proj_l, na_lat, w_out_p, w2_p, l, h, mod4, 2, row_lat)
        h = _ffn(h, norm2_g[l], mod4, row_lat, w_up_p, ffn_conv_w, fcb, w_down_p, l, seq,
                 final_g=final_norm_g if last else None)

        if not last:
            mla_ctx = _flash(q_c, 0, k_c, 0, v_c, 0, MLA_HEADS, MLA_DQ, MLA_V, batch)
            na_ctx = _flash(proj_c, C_NAQ // NA_DH, proj_c, C_NAK // NA_DH, proj_c, C_NAV // NA_DH,
                            NA_HEADS, NA_DH, NA_DH, batch, scale=NA_DH ** -0.5 * LOG2E)
            hc = _outproj(mla_ctx, hf_c, hb_c, proj_c, na_ctx, w_out_p, w2_p, l, hc, mod4, 2, row_ctx)
            hc = _ffn(hc, norm2_g[l], mod4, row_ctx, w_up_p, ffn_conv_w, fcb, w_down_p, l, tctx)

    return h.reshape(batch, seq, d)
```

```python
import functools
import math

import numpy as np
import jax
import jax.numpy as jnp
from jax import lax
from jax.experimental import pallas as pl
from jax.experimental.pallas import tpu as pltpu

F32 = jnp.float32
BF16 = jnp.bfloat16

D_MODEL = 2048
GRID_W = 64
MLA_HEADS = 6
MLA_NOPE = 128
MLA_ROPE = 64
MLA_V = 128
MLA_Q_LORA = 512
MLA_KV_LORA = 256
ML_HEADS = 4
ML_DH = 192
NA_HEADS = 4
NA_DH = 128
NA_KR = 8
NA_KC = 16
ROPE_BASE = 10000.0
EPS = 1e-6
N_MOD = 6
ML_W = ML_HEADS * ML_DH
NA_W = NA_HEADS * NA_DH
MLA_W = MLA_HEADS * MLA_V
IN_SIZES = (MLA_Q_LORA, MLA_KV_LORA, MLA_ROPE, ML_W, ML_W, ML_W, ML_W, 2 * ML_HEADS, 2 * ML_HEADS, NA_W, NA_W, NA_W)
IN_OFFSETS = tuple(sum(IN_SIZES[:i + 1]) for i in range(len(IN_SIZES) - 1))
LOG2E = math.log2(math.e)

LANE = 128
ML_DP = 256
ML_WP = ML_HEADS * ML_DP
ML_L = 256
MLA_DQ = 256
TM = 512
TM_BIG = 1024
ROWS_F32 = 8
ROWS_BF16 = 16
HALO = ROWS_BF16
VMEM_LIMIT = 58 * 1024 * 1024

C_MLQ = 0
C_MLK = C_MLQ + ML_W
C_MLV = C_MLK + ML_W
C_MLO = C_MLV + ML_W
C_QLAT = C_MLO + ML_W
C_KVLAT = C_QLAT + MLA_Q_LORA
C_KROPE = C_KVLAT + MLA_KV_LORA
C_GATE0 = C_KROPE + LANE
C_GATES = LANE
C_NAQ = C_GATE0 + C_GATES
C_NAK = C_NAQ + NA_W
C_NAV = C_NAK + NA_W
C_IN = C_NAV + NA_W
TN_IN = 512
ML_PAIR = 2 * ML_DH


def _cp(sem):
    return pltpu.CompilerParams(dimension_semantics=sem, vmem_limit_bytes=VMEM_LIMIT)


def _sigmoid(x):
    return 1.0 / (1.0 + jnp.exp(-x))


def _rms(x, g):
    return x * lax.rsqrt(jnp.mean(x * x, axis=-1, keepdims=True) + EPS) * g


def _norm_mod_rows(x_ref, dst_ref, dst_row0, g_ref, sc_ref, sh_ref):
    gs = g_ref[...] * (1.0 + sc_ref[...])
    sh = sh_ref[...]

    def body(r, carry):
        row = pl.multiple_of(r * ROWS_BF16, ROWS_BF16)
        x = x_ref[pl.ds(row, ROWS_BF16), :]
        y = x * lax.rsqrt(jnp.mean(x * x, axis=-1, keepdims=True) + EPS)
        dst_ref[pl.ds(dst_row0 + row, ROWS_BF16), :] = (y * gs + sh).astype(BF16)
        return carry

    lax.fori_loop(0, x_ref.shape[0] // ROWS_BF16, body, 0, unroll=min(16, x_ref.shape[0] // ROWS_BF16))


def _mod_spec(chunk, rowf, tm, width=D_MODEL):
    return pl.BlockSpec((None, None, 1, width), lambda i, j: (rowf(i, tm), chunk, 0, j if width != D_MODEL else 0))


def _ada_kernel(c_ref, w_ref, b_ref, o_ref):
    c = c_ref[...]
    a = c * _sigmoid(c)
    o_ref[...] = jnp.dot(a, w_ref[...], preferred_element_type=F32,
                         precision=lax.Precision.HIGHEST) + b_ref[...]


def _ada(cond8, w, b):
    depth, d, n = w.shape
    tn = 2048
    return pl.pallas_call(
        _ada_kernel,
        grid=(depth, n // tn),
        in_specs=[pl.BlockSpec((8, d), lambda l, j: (0, 0)),
                  pl.BlockSpec((None, d, tn), lambda l, j: (l, 0, j)),
                  pl.BlockSpec((None, 1, tn), lambda l, j: (l, 0, j))],
        out_specs=pl.BlockSpec((None, 8, tn), lambda l, j: (l, 0, j)),
        out_shape=jax.ShapeDtypeStruct((depth, 8, n), F32),
        compiler_params=_cp(("arbitrary", "arbitrary")),
    )(cond8, w, b.reshape(depth, 1, n))


def _nm_mm_kernel(x_ref, g_ref, sc_ref, sh_ref, w_ref, o_ref, t_ref, tt_ref, hn_ref, *, f32_col):
    j = pl.program_id(1)
    tn = w_ref.shape[1]

    @pl.when(j == 0)
    def _():
        _norm_mod_rows(x_ref, hn_ref, 0, g_ref, sc_ref, sh_ref)

    acc = jnp.dot(hn_ref[...], w_ref[...], preferred_element_type=F32)
    o_ref[...] = acc.astype(o_ref.dtype)

    @pl.when(j == f32_col // tn)
    def _():
        t = acc[:, f32_col % tn:f32_col % tn + t_ref.shape[1]]
        t_ref[...] = t
        tt_ref[...] = t.T


def _nm_mm(x, g, mod4, sc_chunk, sh_chunk, rowf, w, l, tn, f32_col, f32_cols):
    m, d = x.shape
    n = w.shape[2]
    tm = min(TM_BIG, m)
    return pl.pallas_call(
        functools.partial(_nm_mm_kernel, f32_col=f32_col),
        grid=(m // tm, n // tn),
        in_specs=[pl.BlockSpec((tm, d), lambda i, j: (i, 0)),
                  pl.BlockSpec((1, d), lambda i, j: (0, 0)),
                  _mod_spec(sc_chunk, rowf, tm), _mod_spec(sh_chunk, rowf, tm),
                  pl.BlockSpec((None, d, tn), lambda i, j: (l, 0, j))],
        out_specs=[pl.BlockSpec((tm, tn), lambda i, j: (i, j)),
                   pl.BlockSpec((tm, f32_cols), lambda i, j: (i, 0)),
                   pl.BlockSpec((f32_cols, tm), lambda i, j: (0, i))],
        out_shape=[jax.ShapeDtypeStruct((m, n), BF16), jax.ShapeDtypeStruct((m, f32_cols), F32),
                   jax.ShapeDtypeStruct((f32_cols, m), F32)],
        scratch_shapes=[pltpu.VMEM((tm, d), BF16)],
        compiler_params=_cp(("parallel", "arbitrary")),
    )(x, g.reshape(1, d), mod4, mod4, w)


def _mla_up_kernel(ql_ref, kvl_ref, kr_ref, tab_ref, qg_ref, kvg_ref, wuq_ref, wuk_ref, wuv_ref,
                   q_ref, k_ref, v_ref, *, scale):
    tab = tab_ref[...]
    qn = _rms(ql_ref[...].astype(F32), qg_ref[...]).astype(BF16)
    qf = jnp.dot(qn, wuq_ref[...], preferred_element_type=F32)
    for h in range(MLA_HEADS):
        c0 = h * MLA_DQ
        y = qf[:, c0 + MLA_NOPE:c0 + MLA_DQ] * tab
        z = y + pltpu.roll(y, MLA_ROPE, axis=1)
        q_ref[:, c0:c0 + MLA_NOPE] = (qf[:, c0:c0 + MLA_NOPE] * scale).astype(BF16)
        q_ref[:, c0 + MLA_NOPE:c0 + MLA_DQ] = (z * scale).astype(BF16)
    cn = _rms(kvl_ref[...].astype(F32), kvg_ref[...]).astype(BF16)
    kn = jnp.dot(cn, wuk_ref[...], preferred_element_type=F32)
    v_ref[...] = jnp.dot(cn, wuv_ref[...], preferred_element_type=F32).astype(BF16)
    ky = kr_ref[...].astype(F32) * tab
    kz = ky + pltpu.roll(ky, MLA_ROPE, axis=1)
    lane = lax.broadcasted_iota(jnp.int32, kz.shape, 1)
    kz = jnp.where(lane < MLA_ROPE, kz, 0.0).astype(BF16)
    for h in range(MLA_HEADS):
        c0 = h * MLA_DQ
        k_ref[:, c0:c0 + MLA_NOPE] = kn[:, h * MLA_NOPE:(h + 1) * MLA_NOPE].astype(BF16)
        k_ref[:, c0 + MLA_NOPE:c0 + MLA_DQ] = kz


def _mla_up(proj, tab, qg, kvg, wuq, wuk, wuv):
    m = proj.shape[0]
    tm = min(TM, tab.shape[0])
    nt = tab.shape[0] // tm
    full = lambda a: pl.BlockSpec(a.shape, lambda i: (0, 0))
    return pl.pallas_call(
        functools.partial(_mla_up_kernel, scale=(MLA_NOPE + MLA_ROPE) ** -0.5 * LOG2E),
        grid=(m // tm,),
        in_specs=[pl.BlockSpec((tm, MLA_Q_LORA), lambda i: (i, C_QLAT // MLA_Q_LORA)),
                  pl.BlockSpec((tm, MLA_KV_LORA), lambda i: (i, C_KVLAT // MLA_KV_LORA)),
                  pl.BlockSpec((tm, LANE), lambda i: (i, C_KROPE // LANE)),
                  pl.BlockSpec((tm, LANE), lambda i: (i % nt, 0)),
                  full(qg), full(kvg), full(wuq), full(wuk), full(wuv)],
        out_specs=[pl.BlockSpec((tm, MLA_HEADS * MLA_DQ), lambda i: (i, 0)),
                   pl.BlockSpec((tm, MLA_HEADS * MLA_DQ), lambda i: (i, 0)),
                   pl.BlockSpec((tm, MLA_W), lambda i: (i, 0))],
        out_shape=[jax.ShapeDtypeStruct((m, MLA_HEADS * MLA_DQ), BF16),
                   jax.ShapeDtypeStruct((m, MLA_HEADS * MLA_DQ), BF16),
                   jax.ShapeDtypeStruct((m, MLA_W), BF16)],
        compiler_params=_cp(("parallel",)),
    )(proj, proj, proj, tab, qg, kvg, wuq, wuk, wuv)


def _flash_kernel(*refs, has_lat, scale, nsub, tk):
    if has_lat:
        q_ref, kc_ref, vc_ref, kl_ref, vl_ref, o_ref, m_scr, acc_scr = refs
    else:
        q_ref, kc_ref, vc_ref, o_ref, m_scr, acc_scr = refs
    ts = q_ref.shape[0] // nsub
    dv = vc_ref.shape[1]
    dn = (((1,), (1,)), ((), ()))

    def scores(q, k):
        s = lax.dot_general(q, k, dn, preferred_element_type=F32)
        return s if scale == 1.0 else s * scale

    def with_ones(v):
        lane = lax.broadcasted_iota(jnp.int32, v.shape, 1)
        return jnp.concatenate([v, jnp.where(lane == 0, 1.0, 0.0).astype(v.dtype)], axis=1)

    def lanes(x, n):
        return x if n == LANE else pltpu.repeat(x, n // LANE, axis=1)

    def probs(s, m):
        return jnp.exp2(s - lanes(m, s.shape[1])).astype(BF16)

    kc = kc_ref[...]
    vc = with_ones(vc_ref[...])
    for u in range(nsub):
        s = scores(q_ref[u * ts:(u + 1) * ts, :], kc)
        m = jnp.broadcast_to(jnp.max(s, axis=-1, keepdims=True), (ts, LANE))
        m_scr[u] = m
        acc_scr[u] = jnp.dot(probs(s, m), vc, preferred_element_type=F32)

    if has_lat:
        def body(kb, carry):
            off = pl.multiple_of(kb * tk, tk)
            k = kl_ref[pl.ds(off, tk), :]
            v = with_ones(vl_ref[pl.ds(off, tk), :])
            s_next = scores(q_ref[0:ts, :], k)
            for u in range(nsub):
                s = s_next
                if u + 1 < nsub:
                    s_next = scores(q_ref[(u + 1) * ts:(u + 2) * ts, :], k)
                m_prev = m_scr[u]
                m_new = jnp.maximum(m_prev, jnp.max(s, axis=-1, keepdims=True))
                m_scr[u] = m_new
                acc_scr[u] = (lanes(jnp.exp2(m_prev - m_new), 2 * dv) * acc_scr[u]
                              + jnp.dot(probs(s, m_new), v, preferred_element_type=F32))
            return carry

        lax.fori_loop(0, kl_ref.shape[0] // tk, body, 0)

    for u in range(nsub):
        acc = acc_scr[u]
        o_ref[u * ts:(u + 1) * ts, :] = (acc[:, :dv] / acc[:, dv:dv + 1]).astype(o_ref.dtype)


def _flash(q, qcol, kc, kccol, vc, vccol, heads, dq, dv, batch, lat=None, scale=1.0, tq=2048, nsub=2, tk=2048):
    mq = q.shape[0]
    sq = mq // batch
    tq = min(tq, sq)
    nsub = min(nsub, tq // 256)
    nq = sq // tq
    tc = kc.shape[0] // batch
    has_lat = lat is not None
    in_specs = [pl.BlockSpec((tq, dq), lambda b, h, i: (b * nq + i, qcol + h)),
                pl.BlockSpec((tc, dq), lambda b, h, i: (b, kccol + h)),
                pl.BlockSpec((tc, dv), lambda b, h, i: (b, vccol + h))]
    args = [q, kc, vc]
    if has_lat:
        kl, klcol, vl, vlcol = lat
        sk = kl.shape[0] // batch
        tk = min(tk, sk)
        in_specs += [pl.BlockSpec((sk, dq), lambda b, h, i: (b, klcol + h)),
                     pl.BlockSpec((sk, dv), lambda b, h, i: (b, vlcol + h))]
        args += [kl, vl]
    return pl.pallas_call(
        functools.partial(_flash_kernel, has_lat=has_lat, scale=scale, nsub=nsub, tk=tk),
        grid=(batch, heads, nq),
        in_specs=in_specs,
        out_specs=pl.BlockSpec((tq, dv), lambda b, h, i: (b * nq + i, h)),
        out_shape=jax.ShapeDtypeStruct((mq, heads * dv), BF16),
        scratch_shapes=[pltpu.VMEM((nsub, tq // nsub, LANE), F32), pltpu.VMEM((nsub, tq // nsub, 2 * dv), F32)],
        compiler_params=_cp(("parallel", "parallel", "arbitrary")),
    )(*args)


NA_RB = 8
NA_BLK = NA_RB * GRID_W
NA_GROUP = 2


def _na_kernel(q_ref, kp_ref, kc_ref, kn_ref, vp_ref, vc_ref, vn_ref, kx_ref, vx_ref, bias_ref, o_ref,
               kw_scr, vw_scr, *, rows, scale):
    j = pl.program_id(1)
    for t, (kr, vr) in enumerate(((kp_ref, vp_ref), (kc_ref, vc_ref), (kn_ref, vn_ref))):
        kw_scr[t * NA_BLK:(t + 1) * NA_BLK, :] = kr[...]
        vw_scr[t * NA_BLK:(t + 1) * NA_BLK, :] = vr[...]
    nwin = NA_KR * GRID_W
    dn = (((1,), (1,)), ((), ()))
    heads = [slice(h * NA_DH, (h + 1) * NA_DH) for h in range(NA_HEADS)]
    for i0 in range(0, NA_RB, NA_GROUP):
        units = []
        for i in range(i0, i0 + NA_GROUP):
            r = NA_RB * j + i
            r0 = jnp.clip(r - NA_KR // 2, 0, rows - NA_KR)
            start = pl.multiple_of((r0 - NA_RB * j + NA_RB) * GRID_W, GRID_W)
            for h, cs in enumerate(heads):
                q = q_ref[i * GRID_W:(i + 1) * GRID_W, cs]
                s_loc = (lax.dot_general(q, kw_scr[pl.ds(start, nwin), cs], dn, preferred_element_type=F32) * scale
                         + bias_ref[h, r - r0])
                s_ctx = lax.dot_general(q, kx_ref[:, cs], dn, preferred_element_type=F32) * scale
                units.append((i, cs, start, s_loc, s_ctx))
        outs = {}
        for i, cs, start, s_loc, s_ctx in units:
            m = jnp.maximum(jnp.max(s_loc, axis=-1, keepdims=True), jnp.max(s_ctx, axis=-1, keepdims=True))
            p_loc = jnp.exp(s_loc - m)
            p_ctx = jnp.exp(s_ctx - m)
            l = jnp.sum(p_loc, axis=-1, keepdims=True) + jnp.sum(p_ctx, axis=-1, keepdims=True)
            o = (jnp.dot(p_loc.astype(BF16), vw_scr[pl.ds(start, nwin), cs], preferred_element_type=F32)
                 + jnp.dot(p_ctx.astype(BF16), vx_ref[:, cs], preferred_element_type=F32))
            outs.setdefault(i, []).append((o / l).astype(o_ref.dtype))
        for i, parts in outs.items():
            o_ref[i * GRID_W:(i + 1) * GRID_W, :] = jnp.concatenate(parts, axis=1)


def _na_bias_table(rpb):
    c = np.arange(GRID_W)
    cs = np.clip(c - NA_KC // 2, 0, GRID_W - NA_KC)
    valid = (c[None, :] >= cs[:, None]) & (c[None, :] < cs[:, None] + NA_KC)
    dc = np.clip(c[None, :] - c[:, None] + (NA_KC - 1), 0, 2 * NA_KC - 2)
    dr = np.arange(NA_KR)[None, :] - np.arange(NA_KR)[:, None] + (NA_KR - 1)
    sel_r = (dr[:, :, None] == np.arange(2 * NA_KR - 1)).astype(np.float32)
    sel_c = ((dc[:, :, None] == np.arange(2 * NA_KC - 1)) & valid[:, :, None]).astype(np.float32)
    tab = jnp.einsum('lhab,via,cdb->lhvcid', rpb.astype(F32), sel_r, sel_c, precision=lax.Precision.HIGHEST)
    tab = tab + jnp.where(valid[None, None, None, :, None, :], 0.0, -1e30)
    return tab.reshape(rpb.shape[0], NA_HEADS, NA_KR, GRID_W, NA_KR * GRID_W)


def _na(proj_l, proj_c, bias, l, batch):
    m = proj_l.shape[0]
    s = m // batch
    rows = s // GRID_W
    nb = rows // NA_RB
    cq, ck, cv = C_NAQ // NA_W, C_NAK // NA_W, C_NAV // NA_W
    tc = proj_c.shape[0] // batch
    prev = lambda b, j: b * nb + jnp.maximum(j - 1, 0)
    cur = lambda b, j: b * nb + j
    nxt = lambda b, j: b * nb + jnp.minimum(j + 1, nb - 1)
    blk = lambda rf, col: pl.BlockSpec((NA_BLK, NA_W), lambda b, j: (rf(b, j), col))
    return pl.pallas_call(
        functools.partial(_na_kernel, rows=rows, scale=NA_DH ** -0.5),
        grid=(batch, nb),
        in_specs=[blk(cur, cq), blk(prev, ck), blk(cur, ck), blk(nxt, ck),
                  blk(prev, cv), blk(cur, cv), blk(nxt, cv),
                  pl.BlockSpec((tc, NA_W), lambda b, j: (b, ck)),
                  pl.BlockSpec((tc, NA_W), lambda b, j: (b, cv)),
                  pl.BlockSpec((None,) + bias.shape[1:], lambda b, j: (l, 0, 0, 0, 0))],
        out_specs=pl.BlockSpec((NA_BLK, NA_W), lambda b, j: (b * nb + j, 0)),
        out_shape=jax.ShapeDtypeStruct((m, NA_W), BF16),
        scratch_shapes=[pltpu.VMEM((3 * NA_BLK, NA_W), BF16), pltpu.VMEM((3 * NA_BLK, NA_W), BF16)],
        compiler_params=_cp(("parallel", "arbitrary")),
    )(proj_l, proj_l, proj_l, proj_l, proj_l, proj_l, proj_l, proj_c, proj_c, bias)


def _ml_prep_kernel(x_ref, p_ref, n_ref, w_ref, b_ref, sc_ref, o_ref, *, tiles_per_seg, n_conv):
    j = pl.program_id(1)
    x = x_ref[...].astype(F32)
    tm = x.shape[0]

    def widen(y, fill):
        lane = lax.broadcasted_iota(jnp.int32, (tm, ML_DP - ML_DH), 1)
        spare = jnp.where(lane == 0, fill, 0.0)
        return jnp.concatenate([y[:, :ML_DH], spare, y[:, ML_DH:], spare], axis=1).astype(o_ref.dtype)

    @pl.when(j < n_conv)
    def _():
        t = pl.program_id(0) % tiles_per_seg
        prev_row = jnp.where(t == 0, 0.0, p_ref[HALO - 1:HALO, :].astype(F32))
        next_row = jnp.where(t == tiles_per_seg - 1, 0.0, n_ref[0:1, :].astype(F32))
        rid = lax.broadcasted_iota(jnp.int32, x.shape, 0)
        xp = jnp.where(rid == 0, prev_row, pltpu.roll(x, 1, axis=0))
        xn = jnp.where(rid == tm - 1, next_row, pltpu.roll(x, tm - 1, axis=0))
        w = w_ref[...]
        y = xp * w[0:1, :] + x * w[1:2, :] + xn * w[2:3, :] + b_ref[...]
        o_ref[...] = widen(y * _sigmoid(y) * sc_ref[...], 0.0)

    @pl.when(j >= n_conv)
    def _():
        o_ref[...] = widen(x, 1.0)


def _ml_prep(proj, w, b, colscale, seg):
    m = proj.shape[0]
    tm = min(TM_BIG, seg)
    hb = tm // HALO
    last = m // HALO - 1
    n_conv = w.shape[1] // ML_PAIR
    n_all = 3 * ML_W // ML_PAIR
    vec = lambda r: pl.BlockSpec((r, ML_PAIR), lambda i, j: (0, jnp.minimum(j, n_conv - 1)))
    return pl.pallas_call(
        functools.partial(_ml_prep_kernel, tiles_per_seg=seg // tm, n_conv=n_conv),
        grid=(m // tm, n_all),
        in_specs=[pl.BlockSpec((tm, ML_PAIR), lambda i, j: (i, j)),
                  pl.BlockSpec((HALO, ML_PAIR), lambda i, j: (jnp.maximum(i * hb - 1, 0), j)),
                  pl.BlockSpec((HALO, ML_PAIR), lambda i, j: (jnp.minimum((i + 1) * hb, last), j)),
                  vec(3), vec(1), vec(1)],
        out_specs=pl.BlockSpec((tm, 2 * ML_DP), lambda i, j: (i, j)),
        out_shape=jax.ShapeDtypeStruct((m, 3 * ML_WP), BF16),
        compiler_params=_cp(("parallel", "parallel")),
    )(proj, proj, proj, w, b, colscale)


def _split3(x):
    x1 = x.astype(BF16)
    r = x - x1.astype(F32)
    x2 = r.astype(BF16)
    x3 = (r - x2.astype(F32)).astype(BF16)
    return x1, x2, x3


def _log_sigmoid(x):
    return jnp.minimum(x, 0.0) - jnp.log(1.0 + jnp.exp(-jnp.abs(x)))


def _mlstm_dir(d, q_ref, k_ref, v_ref, gc_ref, gr_ref, bc_ref, br_ref, h_ref, c_scr, m_scr):
    L = ML_L
    ri = lax.broadcasted_iota(jnp.int32, (L, L), 0)
    ci = lax.broadcasted_iota(jnp.int32, (L, L), 1)
    mask = (ci <= ri) if d == 0 else (ci >= ri)
    tri = jnp.where(mask, 1.0, 0.0).astype(BF16)
    tri_t = jnp.where((ri <= ci) if d == 0 else (ri >= ci), 1.0, 0.0).astype(BF16)
    lane = lax.broadcasted_iota(jnp.int32, (L, ML_DP), 1)

    gc = gc_ref[...] + bc_ref[...]
    gr = gr_ref[...] + br_ref[...]
    bc_all = sum(jnp.dot(tri, p, preferred_element_type=F32) for p in _split3(_log_sigmoid(gc)))
    br_all = sum(jnp.dot(p, tri_t, preferred_element_type=F32) for p in _split3(_log_sigmoid(gr)))

    hs = range(ML_HEADS)
    cols = [slice(h * ML_DP, (h + 1) * ML_DP) for h in hs]
    q = [q_ref[:, cs] for cs in cols]
    k = [k_ref[:, cs] for cs in cols]
    v = [v_ref[:, cs] for cs in cols]
    m_prev = [m_scr[h, d][0:1, 0:1] for h in hs]
    ct = [c_scr[h, d] for h in hs]
    qk = [lax.dot_general(q[h], k[h], (((1,), (1,)), ((), ())), preferred_element_type=F32) for h in hs]
    inter = [jnp.dot(q[h], ct[h].astype(BF16), preferred_element_type=F32) for h in hs]

    rep = lambda col: jnp.broadcast_to(col, (L, LANE))
    wide = lambda x: pltpu.repeat(x, ML_DP // LANE, axis=1)
    b_rep = [rep(bc_all[:, 4 * h + 2 + d:4 * h + 3 + d]) for h in hs]
    li_rep = [rep(gc[:, 4 * h + d:4 * h + d + 1]) for h in hs]
    a, w_st, m_t = [], [], []
    for h in hs:
        b_row = br_all[4 * h + 2 + d:4 * h + 3 + d, :]
        li_row = gr[4 * h + d:4 * h + d + 1, :]
        dmat = jnp.where(mask, wide(b_rep[h]) - b_row + li_row, -jnp.inf)
        g = b_rep[h] + m_prev[h]
        m_t.append(jnp.maximum(g, jnp.max(dmat, axis=-1, keepdims=True)))
        w_st.append(jnp.exp(g - m_t[h]))
        a.append((qk[h] * jnp.exp(dmat - wide(m_t[h]))).astype(BF16))

    for h in hs:
        num = wide(w_st[h]) * inter[h] + jnp.dot(a[h], v[h], preferred_element_type=F32)
        den = rep(num[:, ML_DH:ML_DH + 1])
        inv = 1.0 / jnp.maximum(jnp.abs(den), jnp.exp(-m_t[h]))
        h_ref[:, cols[h]] = jnp.where(lane < ML_DH, num * wide(inv), 0.0).astype(h_ref.dtype)

    for h in hs:
        b_end = b_rep[h][L - 1:L, :] if d == 0 else b_rep[h][0:1, :]
        e_log = b_end - b_rep[h] + li_rep[h]
        m_new = jnp.maximum(b_end + m_prev[h], jnp.max(e_log, axis=0, keepdims=True))
        decay = jnp.exp(b_end + m_prev[h] - m_new)[:, 0:1]
        ek = (wide(jnp.exp(e_log - m_new)) * k[h].astype(F32)).astype(BF16)
        c_scr[h, d] = decay * ct[h] + lax.dot_general(ek, v[h], (((0,), (0,)), ((), ())),
                                                      preferred_element_type=F32)
        m_scr[h, d] = jnp.broadcast_to(m_new, m_scr.shape[2:])


def _mlstm_kernel(qf_ref, kf_ref, vf_ref, gcf_ref, grf_ref, qb_ref, kb_ref, vb_ref, gcb_ref, grb_ref,
                  bc_ref, br_ref, c0_ref, m0_ref, hf_ref, hb_ref, c1_ref, m1_ref, c_scr, m_scr):
    c = pl.program_id(1)

    @pl.when(c == 0)
    def _():
        c_scr[...] = c0_ref[...]
        m_scr[...] = m0_ref[...]

    _mlstm_dir(0, qf_ref, kf_ref, vf_ref, gcf_ref, grf_ref, bc_ref, br_ref, hf_ref, c_scr, m_scr)
    _mlstm_dir(1, qb_ref, kb_ref, vb_ref, gcb_ref, grb_ref, bc_ref, br_ref, hb_ref, c_scr, m_scr)

    @pl.when(c == pl.num_programs(1) - 1)
    def _():
        c1_ref[...] = c_scr[...]
        m1_ref[...] = m_scr[...]


def _mlstm(qkv, gates, gates_row, bias_col, bias_row, c0, m0, batch):
    m = qkv.shape[0]
    s = m // batch
    nc = s // ML_L
    fwd = lambda b, c: b * nc + c
    bwd = lambda b, c: b * nc + (nc - 1 - c)

    def dir_specs(rf):
        return [pl.BlockSpec((ML_L, ML_WP), lambda b, c: (rf(b, c), 0)),
                pl.BlockSpec((ML_L, ML_WP), lambda b, c: (rf(b, c), 1)),
                pl.BlockSpec((ML_L, ML_WP), lambda b, c: (rf(b, c), 2)),
                pl.BlockSpec((ML_L, C_GATES), lambda b, c: (rf(b, c), 0)),
                pl.BlockSpec((4 * ML_HEADS, ML_L), lambda b, c: (0, rf(b, c)))]

    st_c = pl.BlockSpec((None, ML_HEADS, 2, ML_DP, ML_DP), lambda b, c: (b, 0, 0, 0, 0))
    st_m = pl.BlockSpec((None, ML_HEADS, 2, 8, LANE), lambda b, c: (b, 0, 0, 0, 0))
    return pl.pallas_call(
        _mlstm_kernel,
        grid=(batch, nc),
        in_specs=dir_specs(fwd) + dir_specs(bwd)
        + [pl.BlockSpec((1, C_GATES), lambda b, c: (0, 0)),
           pl.BlockSpec((4 * ML_HEADS, 1), lambda b, c: (0, 0)), st_c, st_m],
        out_specs=[pl.BlockSpec((ML_L, ML_WP), lambda b, c: (fwd(b, c), 0)),
                   pl.BlockSpec((ML_L, ML_WP), lambda b, c: (bwd(b, c), 0)), st_c, st_m],
        out_shape=[jax.ShapeDtypeStruct((m, ML_WP), BF16), jax.ShapeDtypeStruct((m, ML_WP), BF16),
                   jax.ShapeDtypeStruct(c0.shape, F32), jax.ShapeDtypeStruct(m0.shape, F32)],
        scratch_shapes=[pltpu.VMEM((ML_HEADS, 2, ML_DP, ML_DP), F32), pltpu.VMEM((ML_HEADS, 2, 8, LANE), F32)],
        compiler_params=_cp(("parallel", "arbitrary")),
    )(qkv, qkv, qkv, gates, gates_row, qkv, qkv, qkv, gates, gates_row, bias_col, bias_row, c0, m0)


def _outproj_kernel(mla_ref, hf_ref, hb_ref, og_ref, na_ref, w1_ref, w2_ref, w3_ref, res_ref, gate_ref,
                    out_ref, ml_scr):
    @pl.when(pl.program_id(1) == 0)
    def _():
        og = _sigmoid(og_ref[...].astype(F32))
        for h in range(ML_HEADS):
            cs = slice(h * ML_DP, (h + 1) * ML_DP)
            o_h = jnp.concatenate([og[:, h * ML_DH:(h + 1) * ML_DH],
                                   jnp.zeros((og.shape[0], ML_DP - ML_DH), F32)], axis=1)
            ml_scr[:, cs] = ((hf_ref[:, cs].astype(F32) + hb_ref[:, cs].astype(F32)) * o_h).astype(BF16)

    acc = (jnp.dot(mla_ref[...], w1_ref[...], preferred_element_type=F32)
           + jnp.dot(ml_scr[...], w2_ref[...], preferred_element_type=F32)
           + jnp.dot(na_ref[...], w3_ref[...], preferred_element_type=F32))
    out_ref[...] = res_ref[...] + gate_ref[...] * acc


def _outproj(mla, hf, hb, proj, na, w_out, w2, l, res, mod4, gate_chunk, rowf):
    m, d = res.shape
    tm = min(TM, m)
    tn = d
    rows = lambda width, col: pl.BlockSpec((tm, width), lambda i, j: (i, col))
    wrows = lambda nrows, blk: pl.BlockSpec((None, nrows, tn), lambda i, j: (l, blk, j),
                                            pipeline_mode=pl.Buffered(1))
    return pl.pallas_call(
        _outproj_kernel,
        grid=(m // tm, d // tn),
        in_specs=[rows(MLA_W, 0), rows(ML_WP, 0), rows(ML_WP, 0), rows(ML_W, C_MLO // ML_W), rows(NA_W, 0),
                  wrows(MLA_W, 0), wrows(ML_WP, 0), wrows(NA_W, (MLA_W + ML_W) // NA_W),
                  pl.BlockSpec((tm, tn), lambda i, j: (i, j)), _mod_spec(gate_chunk, rowf, tm, tn)],
        out_specs=pl.BlockSpec((tm, tn), lambda i, j: (i, j)),
        out_shape=jax.ShapeDtypeStruct((m, d), F32),
        scratch_shapes=[pltpu.VMEM((tm, ML_WP), BF16)],
        compiler_params=_cp(("parallel", "arbitrary")),
    )(mla, hf, hb, proj, na, w_out, w2, w_out, res, mod4)


def _ffn_kernel(x_ref, xp_ref, xn_ref, g_ref, sc_ref, sh_ref, wg_ref, wv_ref, cwg_ref, cwv_ref, cbg_ref, cbv_ref,
                wd_ref, gate_ref, *rest, tiles_per_seg, final):
    if final:
        fg_ref, o_ref, hn_scr = rest
    else:
        o_ref, hn_scr = rest
    i = pl.program_id(0)
    j = pl.program_id(1)
    tm = x_ref.shape[0]
    te = hn_scr.shape[0]

    @pl.when(j == 0)
    def _():
        _norm_mod_rows(xp_ref, hn_scr, 0, g_ref, sc_ref, sh_ref)
        _norm_mod_rows(x_ref, hn_scr, HALO, g_ref, sc_ref, sh_ref)
        _norm_mod_rows(xn_ref, hn_scr, HALO + tm, g_ref, sc_ref, sh_ref)
        o_ref[...] = jnp.zeros_like(o_ref)

    t = i % tiles_per_seg
    keep_prev = jnp.where(t == 0, 0.0, 1.0)
    keep_next = jnp.where(t == tiles_per_seg - 1, 0.0, 1.0)
    hn = hn_scr[...]

    def conv(w_ref, cw_ref, cb_ref):
        u = jnp.dot(hn, w_ref[...], preferred_element_type=F32)
        rid = lax.broadcasted_iota(jnp.int32, (tm, u.shape[1]), 0)
        up = pltpu.roll(u, 1, axis=0)[HALO:HALO + tm]
        un = pltpu.roll(u, te - 1, axis=0)[HALO:HALO + tm]
        up = jnp.where(rid == 0, up * keep_prev, up)
        un = jnp.where(rid == tm - 1, un * keep_next, un)
        cw = cw_ref[...]
        return up * cw[0:1, :] + u[HALO:HALO + tm] * cw[1:2, :] + un * cw[2:3, :] + cb_ref[...]

    gg = conv(wg_ref, cwg_ref, cbg_ref)
    vv = conv(wv_ref, cwv_ref, cbv_ref)
    act = (gg * _sigmoid(gg) * vv).astype(BF16)
    o_ref[...] += jnp.dot(act, wd_ref[...], preferred_element_type=F32)

    @pl.when(j == pl.num_programs(1) - 1)
    def _():
        if final:
            o_ref[...] = _rms(x_ref[...] + gate_ref[...] * o_ref[...], fg_ref[...])
        else:
            gate = gate_ref[...]
            n_groups = tm // ROWS_F32

            def body(r, carry):
                rows = pl.ds(pl.multiple_of(r * ROWS_F32, ROWS_F32), ROWS_F32)
                o_ref[rows, :] = x_ref[rows, :] + gate * o_ref[rows, :]
                return carry

            lax.fori_loop(0, n_groups, body, 0, unroll=min(16, n_groups))


def _ffn(x, g, mod4, rowf, w_up, cw, cb, w_down, l, seg, final_g=None, tf=512):
    m, d = x.shape
    dff = w_down.shape[1]
    tm = min(TM_BIG, seg)
    nj = dff // tf
    hb = tm // HALO
    last = m // HALO - 1
    final = final_g is not None
    vec = lambda r, off: pl.BlockSpec((None, r, tf), lambda i, j: (l, 0, off + j))
    in_specs = [pl.BlockSpec((tm, d), lambda i, j: (i, 0)),
                pl.BlockSpec((HALO, d), lambda i, j: (jnp.maximum(i * hb - 1, 0), 0)),
                pl.BlockSpec((HALO, d), lambda i, j: (jnp.minimum((i + 1) * hb, last), 0)),
                pl.BlockSpec((1, d), lambda i, j: (0, 0)),
                _mod_spec(4, rowf, tm), _mod_spec(3, rowf, tm),
                pl.BlockSpec((None, d, tf), lambda i, j: (l, 0, j)),
                pl.BlockSpec((None, d, tf), lambda i, j: (l, 0, nj + j)),
                vec(3, 0), vec(3, nj), vec(1, 0), vec(1, nj),
                pl.BlockSpec((None, tf, d), lambda i, j: (l, j, 0)), _mod_spec(5, rowf, tm)]
    args = [x, x, x, g.reshape(1, d), mod4, mod4, w_up, w_up, cw, cw, cb, cb, w_down, mod4]
    if final:
        in_specs.append(pl.BlockSpec((1, d), lambda i, j: (0, 0)))
        args.append(final_g.reshape(1, d))
    return pl.pallas_call(
        functools.partial(_ffn_kernel, tiles_per_seg=seg // tm, final=final),
        grid=(m // tm, nj),
        in_specs=in_specs,
        out_specs=pl.BlockSpec((tm, d), lambda i, j: (i, 0), pipeline_mode=pl.Buffered(1) if final else None),
        out_shape=jax.ShapeDtypeStruct((m, d), F32),
        scratch_shapes=[pltpu.VMEM((tm + 2 * HALO, d), BF16)],
        compiler_params=_cp(("parallel", "arbitrary")),
    )(*args)


def _pad_heads(a, heads, dh, dp, axis):
    shp = a.shape
    a = a.reshape(shp[:axis] + (heads, dh) + shp[axis + 1:])
    pad = [(0, 0)] * a.ndim
    pad[axis + 1] = (0, dp - dh)
    a = jnp.pad(a, pad)
    return a.reshape(shp[:axis] + (heads * dp,) + shp[axis + 1:])


def _rope_cols(a):
    ev, od = a[..., 0::2], a[..., 1::2]
    return jnp.concatenate([ev, od, od, ev], axis=-1)


def _prep_w_in(w):
    w = w.astype(BF16)
    lead = w.shape[:-1]
    o_rope, o_ml, o_gi, o_na = IN_OFFSETS[1], IN_OFFSETS[2], IN_OFFSETS[6], IN_OFFSETS[8]
    gi = w[..., o_gi:o_gi + 2 * ML_HEADS].reshape(lead + (2, ML_HEADS))
    gf = w[..., o_gi + 2 * ML_HEADS:o_na].reshape(lead + (2, ML_HEADS))
    g4 = jnp.stack([gi[..., 0, :], gi[..., 1, :], gf[..., 0, :], gf[..., 1, :]], axis=-1)
    gates = g4.reshape(lead + (4 * ML_HEADS,))

    def place(a, col):
        return jnp.pad(a, [(0, 0)] * len(lead) + [(col, C_IN - col - a.shape[-1])])

    return (place(w[..., o_ml:o_gi], C_MLQ) + place(w[..., :o_rope], C_QLAT)
            + place(_rope_cols(w[..., o_rope:o_ml]), C_KROPE) + place(gates, C_GATE0) + place(w[..., o_na:], C_NAQ))


def _prep_w_uq(w):
    w = w.reshape(w.shape[0], MLA_HEADS, MLA_NOPE + MLA_ROPE)
    out = jnp.concatenate([w[..., :MLA_NOPE], _rope_cols(w[..., MLA_NOPE:])], axis=-1)
    return out.reshape(w.shape[0], MLA_HEADS * MLA_DQ).astype(BF16)


def _gate_bias(i_bias, f_bias):
    g4 = jnp.stack([i_bias[0], i_bias[1], f_bias[0], f_bias[1]], axis=-1).astype(F32).reshape(4 * ML_HEADS)
    return jnp.pad(g4, (0, LANE - 4 * ML_HEADS)).reshape(1, LANE), g4.reshape(4 * ML_HEADS, 1)


def _rope_table(n_tokens):
    t = jnp.arange(n_tokens)
    row = (t // GRID_W).astype(F32)
    col = (t % GRID_W).astype(F32)
    n_freq = MLA_ROPE // 4
    inv = ROPE_BASE ** (-jnp.arange(n_freq, dtype=F32) / n_freq)
    ang = jnp.concatenate([row[:, None] * inv, col[:, None] * inv], axis=-1)
    cos, sin = jnp.cos(ang), jnp.sin(ang)
    return jnp.concatenate([cos, cos, -sin, sin], axis=-1)


def kernel(x, c, ctx, c_ctx, ada_w, ada_b, norm1_g, norm2_g, w_in, mla_q_norm, mla_kv_norm, mla_w_uq, mla_w_uk, mla_w_uv, ml_conv_w, ml_conv_b, ml_i_bias, ml_f_bias, na_rpb, w_out, ffn_w_up, ffn_conv_w, ffn_conv_b, ffn_w_down, final_norm_g):
    batch, seq, d = x.shape
    tctx = ctx.shape[1]
    depth = ada_w.shape[0]
    assert d == D_MODEL and batch <= 7 and tctx == ML_L and tctx % HALO == 0
    assert seq % TM_BIG == 0 and seq % NA_BLK == 0 and seq // GRID_W >= NA_KR

    h = x.reshape(batch * seq, d)
    hc = ctx.reshape(batch * tctx, d)
    row_lat = lambda i, tm: i // (seq // tm)
    row_ctx = lambda i, tm: batch
    cond8 = jnp.zeros((8, d), F32).at[:batch].set(c).at[batch].set(c_ctx)

    tab_lat = _rope_table(seq)
    tab_ctx = jnp.concatenate([jnp.ones((tctx, MLA_ROPE), F32), jnp.zeros((tctx, MLA_ROPE), F32)], axis=-1)
    ml_scale = jnp.concatenate([jnp.ones((1, ML_W), F32), jnp.full((1, ML_W), ML_DH ** -0.5, F32)], axis=-1)

    w_in_p = _prep_w_in(w_in)
    w_out_p = w_out.astype(BF16)
    w2_p = _pad_heads(w_out_p[:, MLA_W:MLA_W + ML_W], ML_HEADS, ML_DH, ML_DP, 1)
    w_up_p = ffn_w_up.astype(BF16)
    w_down_p = ffn_w_down.astype(BF16)
    fcb = ffn_conv_b.reshape(depth, 1, -1)
    na_bias = _na_bias_table(na_rpb)
    mod_all = _ada(cond8, ada_w, ada_b).reshape(depth, 8, N_MOD, 1, d)

    for l in range(depth):
        last = l == depth - 1
        mod4 = mod_all[l]

        wuq = _prep_w_uq(mla_w_uq[l])
        wuk = mla_w_uk[l].astype(BF16)
        wuv = mla_w_uv[l].astype(BF16)
        qg = mla_q_norm[l].reshape(1, -1)
        kvg = mla_kv_norm[l].reshape(1, -1)
        cw = ml_conv_w[l]
        cb = ml_conv_b[l].reshape(1, -1)
        bias_col, bias_row = _gate_bias(ml_i_bias[l], ml_f_bias[l])

        proj_c, gates_c, grow_c = _nm_mm(hc, norm1_g[l], mod4, 1, 0, row_ctx, w_in_p, l, TN_IN, C_GATE0, C_GATES)
        proj_l, gates_l, grow_l = _nm_mm(h, norm1_g[l], mod4, 1, 0, row_lat, w_in_p, l, TN_IN, C_GATE0, C_GATES)

        q_c, k_c, v_c = _mla_up(proj_c, tab_ctx, qg, kvg, wuq, wuk, wuv)
        q_l, k_l, v_l = _mla_up(proj_l, tab_lat, qg, kvg, wuq, wuk, wuv)
        mla_lat = _flash(q_l, 0, k_c, 0, v_c, 0, MLA_HEADS, MLA_DQ, MLA_V, batch, lat=(k_l, 0, v_l, 0))

        qkv_c = _ml_prep(proj_c, cw, cb, ml_scale, tctx)
        qkv_l = _ml_prep(proj_l, cw, cb, ml_scale, seq)
        c0 = jnp.zeros((batch, ML_HEADS, 2, ML_DP, ML_DP), F32)
        m0 = jnp.zeros((batch, ML_HEADS, 2, 8, LANE), F32)
        hf_c, hb_c, c1, m1 = _mlstm(qkv_c, gates_c, grow_c, bias_col, bias_row, c0, m0, batch)
        hf_l, hb_l, _, _ = _mlstm(qkv_l, gates_l, grow_l, bias_col, bias_row, c1, m1, batch)

        na_lat = _na(proj_l, proj_c, na_bias, l, batch)

        h = _outproj(mla_lat, hf_l, hb_l, proj_l, na_lat, w_out_p, w2_p, l, h, mod4, 2, row_lat)
        h = _ffn(h, norm2_g[l], mod4, row_lat, w_up_p, ffn_conv_w, fcb, w_down_p, l, seq,
                 final_g=final_norm_g if last else None)

        if not last:
            mla_ctx = _flash(q_c, 0, k_c, 0, v_c, 0, MLA_HEADS, MLA_DQ, MLA_V, batch)
            na_ctx = _flash(proj_c, C_NAQ // NA_DH, proj_c, C_NAK // NA_DH, proj_c, C_NAV // NA_DH,
                            NA_HEADS, NA_DH, NA_DH, batch, scale=NA_DH ** -0.5 * LOG2E)
            hc = _outproj(mla_ctx, hf_c, hb_c, proj_c, na_ctx, w_out_p, w2_p, l, hc, mod4, 2, row_ctx)
            hc = _ffn(hc, norm2_g[l], mod4, row_ctx, w_up_p, ffn_conv_w, fcb, w_down_p, l, tctx)

    return h.reshape(batch, seq, d)
```
